```python
import math
import jax, jax.numpy as jnp
from jax import lax
import numpy as np

D_MODEL = 1024
BATCH = 4
SEQ = 4096
DEPTH = 1
DEC_BATCH = 128
DEC_SEQ = 1
PAST_LEN = 8192
PAGE_SIZE = 128

H_A = 4
DK_A = 64
DV_A = 128
A_QK = H_A * 2 * DK_A
W_A = H_A * DV_A
H_B = 4
DK_B = 128
DV_B = 128
W_B = H_B * DV_B
D_MIX = W_A + W_B
IN_SIZES = (A_QK, A_QK, W_A, W_A, H_B * DK_B, H_B * DK_B, W_B, W_B, H_B, H_B, W_B)
N_IN = 2 * A_QK + 2 * W_A + 2 * H_B * DK_B + 2 * W_B + 2 * H_B + W_B
F_OFF = 2 * A_QK + 2 * W_A + 2 * H_B * DK_B + 2 * W_B + H_B
Q_BLOCK = 128
CHUNK = 128
EPS = 1e-6
NEG = -1e30

kernel_name = 'hymba_diffattn_mlstm_decode_step'


def rmsnorm(x, g):
    x32 = x.astype(jnp.float32)
    y = x32 * lax.rsqrt(jnp.mean(x32 * x32, axis=-1, keepdims=True) + EPS) * g.astype(jnp.float32)
    return y.astype(x.dtype)


def alibi_slopes():
    return 2.0 ** (-8.0 * jnp.arange(1, H_A + 1, dtype=jnp.float32) / H_A)


def split_cols(z):
    out, start = [], 0
    for size in IN_SIZES:
        out.append(z[..., start:start + size])
        start += size
    return out


def diff_lambda(lam_q, lam_k, lam_init):
    lq = lam_q.astype(jnp.float32)
    lk = lam_k.astype(jnp.float32)
    return jnp.exp(jnp.sum(lq[0] * lk[0])) - jnp.exp(jnp.sum(lq[1] * lk[1])) + lam_init


def diff_qkv(qa, ka, va, g_q, g_k):
    B, T = qa.shape[:2]
    q = rmsnorm(qa.reshape(B, T, H_A, 2, DK_A), g_q) * (DK_A ** -0.5)
    k = rmsnorm(ka.reshape(B, T, H_A, 2, DK_A), g_k)
    v = va.reshape(B, T, H_A, DV_A)
    return q, k, v


def diff_attn_prompt(q, k, v, slopes):
    B, T = q.shape[:2]
    nb = T // Q_BLOCK
    qb = q.reshape(B, nb, Q_BLOCK, H_A, 2, DK_A).swapaxes(0, 1)
    kpos = jnp.arange(T)

    def block(args):
        qi, bi = args
        qpos = bi * Q_BLOCK + jnp.arange(Q_BLOCK)
        s = jnp.einsum('bqhcd,bkhcd->bhcqk', qi, k).astype(jnp.float32)
        dist = (qpos[:, None] - kpos[None, :]).astype(jnp.float32)
        s = s - slopes[None, :, None, None, None] * dist
        s = jnp.where(dist >= 0, s, NEG)
        p = jax.nn.softmax(s, axis=-1)
        return jnp.einsum('bhcqk,bkhv->bqhcv', p.astype(v.dtype), v)

    o = lax.map(block, (qb, jnp.arange(nb)))
    return o.swapaxes(0, 1).reshape(B, T, H_A, 2, DV_A)


def diff_attn_sample(q, k_new, v_new, cache_k, cache_v, layer, page_table, slopes):
    DB, T = q.shape[:2]
    n_pages = page_table.shape[1]
    qpos = n_pages * PAGE_SIZE + jnp.arange(T)

    def scores(kb, kpos):
        s = jnp.einsum('bqhcd,bkhcd->bhcqk', q, kb).astype(jnp.float32)
        dist = (qpos[:, None] - kpos[None, :]).astype(jnp.float32)
        s = s - slopes[None, :, None, None, None] * dist
        return jnp.where(dist >= 0, s, NEG)

    def online(carry, s, vb):
        m, l, acc = carry
        m_new = jnp.maximum(m, s.max(-1))
        alpha = jnp.exp(m - m_new)
        p = jnp.exp(s - m_new[..., None])
        l = l * alpha + p.sum(-1)
        acc = acc * alpha[..., None] + jnp.einsum('bhcqk,bkhv->bhcqv', p, vb.astype(jnp.float32))
        return (m_new, l, acc)

    def body(carry, pg):
        phys = page_table[:, pg]
        kb = cache_k[layer, phys]
        vb = cache_v[layer, phys]
        kpos = pg * PAGE_SIZE + jnp.arange(PAGE_SIZE)
        return online(carry, scores(kb, kpos), vb), None

    init = (jnp.full((DB, H_A, 2, T), NEG, jnp.float32),
            jnp.zeros((DB, H_A, 2, T), jnp.float32),
            jnp.zeros((DB, H_A, 2, T, DV_A), jnp.float32))
    carry, _ = lax.scan(body, init, jnp.arange(n_pages))
    m, l, acc = online(carry, scores(k_new, qpos), v_new)
    o = acc / l[..., None]
    return o.transpose(0, 3, 1, 2, 4).astype(q.dtype)


def diff_out(o, lam, lam_init, gn_a):
    B, T = o.shape[:2]
    d = o[..., 0, :] - lam.astype(o.dtype) * o[..., 1, :]
    return (rmsnorm(d, gn_a) * (1.0 - lam_init)).reshape(B, T, W_A)


def mlstm_inputs(qb, kb, vb, ob, ib, fb):
    B, T = qb.shape[:2]
    f32 = jnp.float32
    q = qb.reshape(B, T, H_B, DK_B).astype(f32)
    k = kb.reshape(B, T, H_B, DK_B).astype(f32) * (DK_B ** -0.5)
    v = vb.reshape(B, T, H_B, DV_B).astype(f32)
    o = jax.nn.sigmoid(ob.astype(f32))
    i_pre = ib.astype(f32)
    logf = jax.nn.log_sigmoid(fb.astype(f32))
    return q, k, v, o, i_pre, logf


def mlstm_chunkwise(q, k, v, i_pre, logf, C0, n0, m0):
    B, T = q.shape[:2]
    L = math.gcd(T, CHUNK)
    nc = T // L

    def to_chunks(a):
        return a.reshape(B, nc, L, *a.shape[2:]).swapaxes(0, 1)

    causal = jnp.tril(jnp.ones((L, L), dtype=bool))

    def body(carry, xs):
        C, n, m = carry
        qc, kc, vc, ic, fc = xs
        b = jnp.cumsum(fc, axis=1)
        D = b[:, :, None, :] - b[:, None, :, :] + ic[:, None, :, :]
        D = jnp.where(causal[None, :, :, None], D, NEG)
        inter = b + m[:, None, :]
        m_t = jnp.maximum(inter, D.max(axis=2))
        w_intra = jnp.exp(D - m_t[:, :, None, :])
        w_inter = jnp.exp(inter - m_t)
        sw = jnp.einsum('bthd,bshd->btsh', qc, kc) * w_intra
        num = jnp.einsum('btsh,bshv->bthv', sw, vc) \
            + w_inter[..., None] * jnp.einsum('bhvd,bthd->bthv', C, qc)
        den = sw.sum(axis=2) + w_inter * jnp.einsum('bhd,bthd->bth', n, qc)
        h = num / jnp.maximum(jnp.abs(den), jnp.exp(-m_t))[..., None]
        m_new = m_t[:, -1]
        w_s = jnp.exp(b[:, -1:, :] - b + ic - m_new[:, None, :])
        decay = jnp.exp(b[:, -1] + m - m_new)
        C_new = decay[..., None, None] * C + jnp.einsum('bsh,bshv,bshd->bhvd', w_s, vc, kc)
        n_new = decay[..., None] * n + jnp.einsum('bsh,bshd->bhd', w_s, kc)
        return (C_new, n_new, m_new), h

    xs = (to_chunks(q), to_chunks(k), to_chunks(v), to_chunks(i_pre), to_chunks(logf))
    (C1, n1, m1), hs = lax.scan(body, (C0, n0, m0), xs)
    h = hs.swapaxes(0, 1).reshape(B, T, H_B, DV_B)
    return h, C1, n1, m1


def mlstm_out(h, o, gn_b):
    B, T = h.shape[:2]
    hg = (o * h.reshape(B, T, W_B)).reshape(B, T, H_B, DV_B)
    return rmsnorm(hg, gn_b).reshape(B, T, W_B)


def mixer_layer(x, c, attend, C0, n0, m0, lam, lam_init,
                norm_g, w_ada, b_ada, w_in, b_in, g_q, g_k, gn_a, gn_b, w_out):
    shift, scale, gate = jnp.split(jax.nn.silu(c) @ w_ada + b_ada, 3, axis=-1)
    h = rmsnorm(x, norm_g) * (1.0 + scale[:, None]) + shift[:, None]
    z = h @ w_in + b_in
    qa, ka, va, ga, qb, kb, vb, ob, ib, fb, gb = split_cols(z)
    q, k, v = diff_qkv(qa, ka, va, g_q, g_k)
    ya = diff_out(attend(q, k, v), lam, lam_init, gn_a)
    mq, mk, mv, mo, mi, mf = mlstm_inputs(qb, kb, vb, ob, ib, fb)
    f32 = jnp.float32
    hm, C1, n1, m1 = mlstm_chunkwise(mq, mk, mv, mi, mf, C0.astype(f32), n0.astype(f32), m0.astype(f32))
    yb = mlstm_out(hm, mo, gn_b).astype(x.dtype)
    mix = jnp.concatenate([ya * jax.nn.silu(ga), yb * jax.nn.silu(gb)], axis=-1)
    y = x + gate[:, None] * (mix @ w_out)
    return y, k, v, C1, n1, m1


def setup_inputs(seed: int = 0) -> dict:
    key = jax.random.key(seed)
    ks = jax.random.split(key, 24)
    f32 = jnp.float32
    n_pages = PAST_LEN // PAGE_SIZE
    n_used = DEC_BATCH * n_pages
    n_phys = n_used + max(1, n_used // 4)

    def nrm(k, shape, s=1.0):
        return s * jax.random.normal(k, shape, f32)

    page_table = jax.random.permutation(ks[9], n_phys)[:n_used].reshape(DEC_BATCH, n_pages).astype(jnp.int32)
    b_in = nrm(ks[14], (DEPTH, N_IN), 0.02).at[:, F_OFF:F_OFF + H_B].add(jnp.linspace(3.0, 6.0, H_B))
    return {
        'x_prompt': nrm(ks[0], (BATCH, SEQ, D_MODEL)),
        'x_sample': nrm(ks[1], (DEC_BATCH, DEC_SEQ, D_MODEL)),
        'c_prompt': nrm(ks[2], (BATCH, D_MODEL)),
        'c_sample': nrm(ks[3], (DEC_BATCH, D_MODEL)),
        'cache_k': nrm(ks[4], (DEPTH, n_phys, PAGE_SIZE, H_A, 2, DK_A)),
        'cache_v': nrm(ks[5], (DEPTH, n_phys, PAGE_SIZE, H_A, DV_A)),
        'state_C': nrm(ks[6], (DEPTH, DEC_BATCH, H_B, DV_B, DK_B), 0.1),
        'state_n': nrm(ks[7], (DEPTH, DEC_BATCH, H_B, DK_B), 0.1),
        'state_m': nrm(ks[8], (DEPTH, DEC_BATCH, H_B), 0.5),
        'page_table': page_table,
        'norm_g': 1.0 + nrm(ks[10], (DEPTH, D_MODEL), 0.05),
        'w_ada': nrm(ks[11], (DEPTH, D_MODEL, 3 * D_MODEL), 0.5 * D_MODEL ** -0.5),
        'b_ada': nrm(ks[12], (DEPTH, 3 * D_MODEL), 0.02),
        'w_in': nrm(ks[13], (DEPTH, D_MODEL, N_IN), D_MODEL ** -0.5),
        'b_in': b_in,
        'g_q': 1.0 + nrm(ks[15], (DEPTH, DK_A), 0.05),
        'g_k': 1.0 + nrm(ks[16], (DEPTH, DK_A), 0.05),
        'lam_q': nrm(ks[17], (DEPTH, 2, DK_A), 0.1),
        'lam_k': nrm(ks[18], (DEPTH, 2, DK_A), 0.1),
        'gn_a': 1.0 + nrm(ks[19], (DEPTH, H_A, DV_A), 0.05),
        'gn_b': 1.0 + nrm(ks[20], (DEPTH, H_B, DV_B), 0.05),
        'w_out': nrm(ks[21], (DEPTH, D_MIX, D_MODEL), D_MIX ** -0.5),
    }


def reference(x_prompt, x_sample, c_prompt, c_sample, cache_k, cache_v, state_C, state_n, state_m,
              page_table, norm_g, w_ada, b_ada, w_in, b_in, g_q, g_k, lam_q, lam_k, gn_a, gn_b, w_out):
    slopes = alibi_slopes()
    yp, ys = x_prompt, x_sample
    Bp = x_prompt.shape[0]
    kp_l, vp_l, ks_l, vs_l = [], [], [], []
    Cp_l, np_l, mp_l, Cs_l, ns_l, ms_l = [], [], [], [], [], []
    for l in range(DEPTH):
        lam_init = 0.8 - 0.6 * math.exp(-0.3 * l)
        lam = diff_lambda(lam_q[l], lam_k[l], lam_init)
        weights = (norm_g[l], w_ada[l], b_ada[l], w_in[l], b_in[l], g_q[l], g_k[l], gn_a[l], gn_b[l], w_out[l])

        def attend_prompt(q, k, v):
            return diff_attn_prompt(q, k, v, slopes)

        def attend_sample(q, k, v, layer=l):
            return diff_attn_sample(q, k, v, cache_k, cache_v, layer, page_table, slopes)

        C0 = jnp.zeros((Bp, H_B, DV_B, DK_B), jnp.float32)
        n0 = jnp.zeros((Bp, H_B, DK_B), jnp.float32)
        m0 = jnp.zeros((Bp, H_B), jnp.float32)
        yp, kp, vp, Cp, npr, mp = mixer_layer(yp, c_prompt, attend_prompt, C0, n0, m0, lam, lam_init, *weights)
        ys, kss, vss, Cs, nss, mss = mixer_layer(ys, c_sample, attend_sample, state_C[l], state_n[l], state_m[l],
                                                 lam, lam_init, *weights)
        kp_l.append(kp); vp_l.append(vp); ks_l.append(kss); vs_l.append(vss)
        Cp_l.append(Cp); np_l.append(npr); mp_l.append(mp)
        Cs_l.append(Cs); ns_l.append(nss); ms_l.append(mss)
    k_prompt = jnp.stack(kp_l)
    v_prompt = jnp.stack(vp_l)
    k_sample = jnp.stack(ks_l)
    v_sample = jnp.stack(vs_l)
    C_prompt = jnp.stack(Cp_l)
    n_prompt = jnp.stack(np_l)
    m_prompt = jnp.stack(mp_l)
    C_sample = jnp.stack(Cs_l)
    n_sample = jnp.stack(ns_l)
    m_sample = jnp.stack(ms_l)
    return (yp, ys, k_prompt, v_prompt, k_sample, v_sample, C_prompt, n_prompt, m_prompt, C_sample, n_sample, m_sample)
```

```python
import functools
import math

import jax
import jax.numpy as jnp
from jax import lax
from jax.experimental import pallas as pl
from jax.experimental.pallas import tpu as pltpu

F32 = jnp.float32
BF16 = jnp.bfloat16

D_MODEL = 1024
H_A = 4
DK_A = 64
DV_A = 128
A_QK = H_A * 2 * DK_A
W_A = H_A * DV_A
H_B = 4
DK_B = 128
DV_B = 128
W_B = H_B * DV_B
N_MAIN = 8 * 512
GATE_PAD = 128
PAGE = 128
CHUNK = 128
EPS = 1e-6
NEG = -1e30
LAM_INIT = 0.8 - 0.6 * math.exp(-0.3 * 0)
ALIBI_SLOPES = tuple(2.0 ** (-8.0 * (h + 1) / H_A) for h in range(H_A))

VMEM_LIMIT = 56 * 1024 * 1024

PROJ_TM = 256
OUT_TM = 512
ATT_T = 512
PAGES_PER_STEP = 8
STEP_RB = 8


def _nt_dot(a, b):
    return lax.dot_general(a, b, (((1,), (1,)), ((), ())), preferred_element_type=F32)


def _dot(a, b):
    return jnp.dot(a, b, preferred_element_type=F32)


def _sigmoid(x):
    return 1.0 / (1.0 + jnp.exp(-x))


def _silu(x):
    return x * _sigmoid(x)


def _log_sigmoid(x):
    return jnp.minimum(x, 0.0) - jnp.log1p(jnp.exp(-jnp.abs(x)))


def _split3(a):
    a1 = a.astype(BF16)
    r1 = a - a1.astype(F32)
    a2 = r1.astype(BF16)
    a3 = (r1 - a2.astype(F32)).astype(BF16)
    return a1, a2, a3


def _ada_kernel(c_ref, w_ref, b_ref, lq_ref, lk_ref, mod_ref, lam_ref):
    a = _silu(c_ref[...])
    w = w_ref[...]
    a1, a2, _ = _split3(a)
    w1, w2, _ = _split3(w)
    mod_ref[...] = (_dot(a1, w1) + (_dot(a1, w2) + _dot(a2, w1))) + b_ref[...]

    @pl.when(pl.program_id(0) == 0)
    def _():
        s = jnp.sum(lq_ref[...] * lk_ref[...], axis=1, keepdims=True)
        e = jnp.exp(s)
        lam = e[0:1, :] - e[1:2, :] + LAM_INIT
        lam_ref[...] = jnp.broadcast_to(lam, lam_ref.shape)


def _ada(c_all, w_ada, b_ada, lam_q, lam_k):
    rows = c_all.shape[0]
    nblk = 3
    return pl.pallas_call(
        _ada_kernel,
        grid=(nblk,),
        in_specs=[
            pl.BlockSpec((rows, D_MODEL), lambda j: (0, 0)),
            pl.BlockSpec((D_MODEL, D_MODEL), lambda j: (0, j)),
            pl.BlockSpec((1, D_MODEL), lambda j: (0, j)),
            pl.BlockSpec((2, DK_A), lambda j: (0, 0)),
            pl.BlockSpec((2, DK_A), lambda j: (0, 0)),
        ],
        out_specs=[
            pl.BlockSpec((rows, D_MODEL), lambda j: (0, j)),
            pl.BlockSpec((8, 128), lambda j: (0, 0)),
        ],
        out_shape=[
            jax.ShapeDtypeStruct((rows, 3 * D_MODEL), F32),
            jax.ShapeDtypeStruct((8, 128), F32),
        ],
        compiler_params=pltpu.CompilerParams(
            dimension_semantics=("arbitrary",), vmem_limit_bytes=VMEM_LIMIT),
        name="ada",
    )(c_all, w_ada, b_ada, lam_q, lam_k)


def _proj_kernel(x_ref, shift_ref, scale_ref, ng_ref, wm_ref, wif_ref, wgb_ref,
                 bm_ref, bif_ref, bgb_ref, gq_ref, gk_ref, seg_ref,
                 q_ref, kf_ref, kb_ref, vf_ref, vb_ref, ga_ref,
                 mq_ref, mk_ref, mv_ref, mo_ref, zif_ref, gb_ref):
    x = x_ref[...]
    ms = jnp.mean(x * x, axis=-1, keepdims=True)
    h = x * lax.rsqrt(ms + EPS) * ng_ref[...]
    h = h * (1.0 + scale_ref[...]) + shift_ref[...]
    hb = h.astype(BF16)

    def col(j):
        sl = slice(j * 512, (j + 1) * 512)
        return _dot(hb, wm_ref[:, sl]) + bm_ref[:, sl]

    def headnorm(z, g):
        ss = _dot((z * z).astype(BF16), seg_ref[...])
        return z * lax.rsqrt(ss * (1.0 / DK_A) + EPS) * g

    q_ref[...] = (headnorm(col(0), gq_ref[...]) * (DK_A ** -0.5)).astype(BF16)
    k = headnorm(col(1), gk_ref[...])
    kf_ref[...] = k
    kb_ref[...] = k.astype(BF16)
    v = col(2)
    vf_ref[...] = v
    vb_ref[...] = v.astype(BF16)
    ga_ref[...] = col(3).astype(BF16)
    mq_ref[...] = col(4).astype(BF16)
    mk_ref[...] = (col(5) * (DK_B ** -0.5)).astype(BF16)
    mv_ref[...] = col(6).astype(BF16)
    mo_ref[...] = _sigmoid(col(7)).astype(BF16)
    zif_ref[...] = _dot(hb, wif_ref[...]) + bif_ref[...]
    gb_ref[...] = (_dot(hb, wgb_ref[...]) + bgb_ref[...]).astype(BF16)


def _proj(x2d, mod, per_row, tm, weights):
    (ng, wm, wif, wgb, bm, bif, bgb, gq, gk, seg) = weights
    m_rows = x2d.shape[0]
    nt = m_rows // tm
    if per_row:
        shift_spec = pl.BlockSpec((tm, D_MODEL), lambda i: (i, 0))
        scale_spec = pl.BlockSpec((tm, D_MODEL), lambda i: (i, 1))
    else:
        tiles_per_batch = nt // mod.shape[0]
        shift_spec = pl.BlockSpec((None, 1, D_MODEL), lambda i: (i // tiles_per_batch, 0, 0))
        scale_spec = pl.BlockSpec((None, 1, D_MODEL), lambda i: (i // tiles_per_batch, 0, 1))

    def const(shape):
        return pl.BlockSpec(shape, lambda i: (0, 0))

    def rows(width):
        return pl.BlockSpec((tm, width), lambda i: (i, 0))

    out_dtypes = [BF16, F32, BF16, F32, BF16, BF16, BF16, BF16, BF16, BF16, F32, BF16]
    out_widths = [512, 512, 512, 512, 512, 512, 512, 512, 512, 512, GATE_PAD, 512]
    return pl.pallas_call(
        _proj_kernel,
        grid=(nt,),
        in_specs=[
            rows(D_MODEL), shift_spec, scale_spec, const((1, D_MODEL)),
            const((D_MODEL, N_MAIN)), const((D_MODEL, GATE_PAD)), const((D_MODEL, W_B)),
            const((1, N_MAIN)), const((1, GATE_PAD)), const((1, W_B)),
            const((1, A_QK)), const((1, A_QK)), const((A_QK, A_QK)),
        ],
        out_specs=[rows(w) for w in out_widths],
        out_shape=[jax.ShapeDtypeStruct((m_rows, w), d) for w, d in zip(out_widths, out_dtypes)],
        compiler_params=pltpu.CompilerParams(
            dimension_semantics=("arbitrary",), vmem_limit_bytes=VMEM_LIMIT),
        name="proj_rows" if per_row else "proj_bcast",
    )(x2d, mod, mod, ng, wm, wif, wgb, bm, bif, bgb, gq, gk, seg)


def _diff_norm_gate(o0, o1, lam, gna, ga):
    d = o0 - lam * o1
    ya = d * lax.rsqrt(jnp.mean(d * d, axis=-1, keepdims=True) + EPS) * gna
    ya = ya * (1.0 - LAM_INIT)
    return ya * _silu(ga)


def _mlstm_out_gate(hh, o, gnb, gb):
    hg = o * hh
    yb = hg * lax.rsqrt(jnp.mean(hg * hg, axis=-1, keepdims=True) + EPS) * gnb
    return yb * _silu(gb)


def _attn_kernel(slopes_ref, q_ref, k_ref, v_ref, ga_ref, gna_ref, lam_ref, out_ref):
    t = ATT_T
    hidx = pl.program_id(1)
    qi = pl.program_id(2)
    slope = slopes_ref[hidx]

    q = q_ref[...].astype(F32)
    lane = lax.broadcasted_iota(jnp.int32, (t, 2 * DK_A), 1)
    qq = jnp.concatenate([jnp.where(lane < DK_A, q, 0.0), jnp.where(lane >= DK_A, q, 0.0)],
                         axis=0).astype(BF16)

    row = lax.broadcasted_iota(jnp.int32, (2 * t, t), 0)
    colv = lax.broadcasted_iota(jnp.int32, (2 * t, t), 1)
    rel = colv - jnp.where(row >= t, row - t, row)
    bias0 = slope * rel.astype(F32)

    def block(kj, carry, masked):
        m, l, acc = carry
        off = pl.multiple_of(kj * t, t)
        kb = k_ref[pl.ds(off, t), :]
        vb = v_ref[pl.ds(off, t), :]
        s1 = _nt_dot(qq, kb) + bias0
        if masked:
            s1 = jnp.where(rel <= 0, s1, NEG)
        c = slope * ((kj - qi) * t).astype(F32)
        m_new = jnp.maximum(m, jnp.max(s1, axis=1, keepdims=True) + c)
        alpha = jnp.exp(m - m_new)
        p = jnp.exp(s1 - (m_new - c))
        l = alpha * l + jnp.sum(p, axis=1, keepdims=True)
        acc = alpha * acc + _dot(p.astype(BF16), vb)
        return m_new, l, acc

    init = (jnp.full((2 * t, 1), NEG, F32), jnp.zeros((2 * t, 1), F32), jnp.zeros((2 * t, DV_A), F32))
    carry = lax.fori_loop(0, qi, lambda kj, cr: block(kj, cr, False), init)
    m, l, acc = block(qi, carry, True)
    o = acc / l
    lam = lam_ref[0:1, :]
    out = _diff_norm_gate(o[:t], o[t:], lam, gna_ref[...], ga_ref[...].astype(F32))
    out_ref[...] = out.astype(BF16)


def _attn_prompt(slopes, q, k, v, ga, gna, lam_tile, nb, seq):
    nq = seq // ATT_T
    return pl.pallas_call(
        _attn_kernel,
        grid_spec=pltpu.PrefetchScalarGridSpec(
            num_scalar_prefetch=1,
            grid=(nb, H_A, nq),
            in_specs=[
                pl.BlockSpec((ATT_T, 128), lambda b, h, i, s: (b * nq + i, h)),
                pl.BlockSpec((seq, 128), lambda b, h, i, s: (b, h)),
                pl.BlockSpec((seq, 128), lambda b, h, i, s: (b, h)),
                pl.BlockSpec((ATT_T, 128), lambda b, h, i, s: (b * nq + i, h)),
                pl.BlockSpec((1, 128), lambda b, h, i, s: (0, h)),
                pl.BlockSpec((8, 128), lambda b, h, i, s: (0, 0)),
            ],
            out_specs=pl.BlockSpec((ATT_T, 128), lambda b, h, i, s: (b * nq + i, h)),
        ),
        out_shape=jax.ShapeDtypeStruct((nb * seq, W_A), BF16),
        compiler_params=pltpu.CompilerParams(
            dimension_semantics=("arbitrary", "arbitrary", "arbitrary"),
            vmem_limit_bytes=VMEM_LIMIT),
        name="attn_prompt",
    )(slopes, q, k, v, ga, gna, lam_tile)


def _mlstm_kernel(q_ref, k_ref, v_ref, o_ref, gb_ref, zif_ref, gnb_ref,
                  y_ref, c_ref, n_ref, m_ref):
    L = CHUNK

    @pl.when(pl.program_id(1) == 0)
    def _():
        c_ref[...] = jnp.zeros_like(c_ref)
        n_ref[...] = jnp.zeros_like(n_ref)
        m_ref[...] = jnp.zeros_like(m_ref)

    x = zif_ref[...]
    logf = _log_sigmoid(x)
    row = lax.broadcasted_iota(jnp.int32, (L, L), 0)
    colv = lax.broadcasted_iota(jnp.int32, (L, L), 1)
    causal = row >= colv
    tri = jnp.where(causal, 1.0, 0.0).astype(BF16)
    f1, f2, f3 = _split3(logf)
    bmat = _dot(tri, f1) + (_dot(tri, f2) + _dot(tri, f3))
    xt = x.T
    bt = bmat.T

    for h in range(H_B):
        sl = slice(h * 128, (h + 1) * 128)
        qh = q_ref[:, sl]
        kh = k_ref[:, sl]
        vh = v_ref[:, sl]
        icol = x[:, h:h + 1]
        bcol = bmat[:, H_B + h:H_B + h + 1]
        irow = xt[h:h + 1, :]
        brow = bt[H_B + h:H_B + h + 1, :]
        m_prev = m_ref[h:h + 1, 0:1]
        nrow = n_ref[h:h + 1, :]
        c_old = c_ref[h]

        d = jnp.where(causal, (bcol - brow) + irow, NEG)
        inter = bcol + m_prev
        m_t = jnp.maximum(inter, jnp.max(d, axis=1, keepdims=True))
        w_intra = jnp.exp(d - m_t)
        w_inter = jnp.exp(inter - m_t)
        sw = _nt_dot(qh, kh) * w_intra
        num = _dot(sw.astype(BF16), vh) + w_inter * _nt_dot(qh, c_old.astype(BF16))
        den = jnp.sum(sw, axis=1, keepdims=True) \
            + w_inter * jnp.sum(qh.astype(F32) * nrow, axis=1, keepdims=True)
        hh = num / jnp.maximum(jnp.abs(den), jnp.exp(-m_t))
        y = _mlstm_out_gate(hh, o_ref[:, sl].astype(F32), gnb_ref[:, sl], gb_ref[:, sl].astype(F32))
        y_ref[:, sl] = y.astype(BF16)

        m_new = m_t[L - 1:L, :]
        b_last = bcol[L - 1:L, :]
        w_s = jnp.exp(((b_last - bcol) + icol) - m_new)
        decay = jnp.exp((b_last + m_prev) - m_new)
        wv = w_s * vh.astype(F32)
        c_ref[h] = decay * c_old + _dot(wv.T.astype(BF16), kh)
        n_ref[h:h + 1, :] = decay * nrow + jnp.sum(w_s * kh.astype(F32), axis=0, keepdims=True)
        m_ref[h:h + 1, :] = jnp.broadcast_to(m_new, (1, 128))


def _mlstm_prompt(mq, mk, mv, mo, gb, zif, gnb, nb, seq):
    nc = seq // CHUNK

    def rows(width):
        return pl.BlockSpec((CHUNK, width), lambda b, c: (b * nc + c, 0))

    return pl.pallas_call(
        _mlstm_kernel,
        grid=(nb, nc),
        in_specs=[rows(512), rows(512), rows(512), rows(512), rows(512), rows(GATE_PAD),
                  pl.BlockSpec((1, W_B), lambda b, c: (0, 0))],
        out_specs=[
            rows(512),
            pl.BlockSpec((None, H_B, DV_B, DK_B), lambda b, c: (b, 0, 0, 0)),
            pl.BlockSpec((None, H_B, DK_B), lambda b, c: (b, 0, 0)),
            pl.BlockSpec((None, H_B, 128), lambda b, c: (b, 0, 0)),
        ],
        out_shape=[
            jax.ShapeDtypeStruct((nb * seq, W_B), BF16),
            jax.ShapeDtypeStruct((nb, H_B, DV_B, DK_B), F32),
            jax.ShapeDtypeStruct((nb, H_B, DK_B), F32),
            jax.ShapeDtypeStruct((nb, H_B, 128), F32),
        ],
        compiler_params=pltpu.CompilerParams(
            dimension_semantics=("arbitrary", "arbitrary"), vmem_limit_bytes=VMEM_LIMIT),
        name="mlstm_prompt",
    )(mq, mk, mv, mo, gb, zif, gnb)


def _paged_kernel(pt_ref, q_ref, kn_ref, vn_ref, ga_ref, gna_ref, lam_ref, *rest):
    pp = PAGES_PER_STEP
    k_refs = rest[:pp]
    v_refs = rest[pp:2 * pp]
    out_ref = rest[2 * pp]
    kb_s, vb_s, m_s, l_s, acc_s = rest[2 * pp + 1:]
    g = pl.program_id(1)
    ng = pl.num_programs(1)
    ntok = pp * PAGE

    @pl.when(g == 0)
    def _():
        m_s[...] = jnp.full_like(m_s, NEG)
        l_s[...] = jnp.zeros_like(l_s)
        acc_s[...] = jnp.zeros_like(acc_s)

    for i in range(pp):
        kb_s[i * PAGE:(i + 1) * PAGE, :] = k_refs[i][...].astype(BF16)
        vb_s[i * PAGE:(i + 1) * PAGE, :] = v_refs[i][...].astype(BF16)

    q = q_ref[...].astype(F32)
    sub = lax.broadcasted_iota(jnp.int32, (8, A_QK), 0)
    lane = lax.broadcasted_iota(jnp.int32, (8, A_QK), 1)
    qbd32 = jnp.where((lane >> 6) == sub, jnp.broadcast_to(q, (8, A_QK)), 0.0)
    qbd = qbd32.astype(BF16)

    j = lax.broadcasted_iota(jnp.int32, (8, 1), 0)
    slope = jnp.where(j < 2, ALIBI_SLOPES[0],
                      jnp.where(j < 4, ALIBI_SLOPES[1],
                                jnp.where(j < 6, ALIBI_SLOPES[2], ALIBI_SLOPES[3]))).astype(F32)
    kpos = g * ntok + lax.broadcasted_iota(jnp.int32, (1, ntok), 1)
    dist = (ng * ntok - kpos).astype(F32)

    s = _nt_dot(qbd, kb_s[...]) - slope * dist
    m_old = m_s[:, 0:1]
    l_old = l_s[:, 0:1]
    m_new = jnp.maximum(m_old, jnp.max(s, axis=1, keepdims=True))
    alpha = jnp.exp(m_old - m_new)
    p = jnp.exp(s - m_new)
    l_new = alpha * l_old + jnp.sum(p, axis=1, keepdims=True)
    acc = alpha * acc_s[...] + _dot(p.astype(BF16), vb_s[...])
    m_s[...] = jnp.broadcast_to(m_new, m_s.shape)
    l_s[...] = jnp.broadcast_to(l_new, l_s.shape)
    acc_s[...] = acc

    @pl.when(g == ng - 1)
    def _():
        kn = kn_ref[...].astype(F32)
        s_new = jnp.sum(qbd32 * kn, axis=1, keepdims=True)
        m_fin = jnp.maximum(m_new, s_new)
        a2 = jnp.exp(m_new - m_fin)
        p_new = jnp.exp(s_new - m_fin)
        l_fin = a2 * l_new + p_new
        o = (a2 * acc + p_new * vn_ref[...].astype(F32)) / l_fin
        lam = lam_ref[0:1, :]
        for h in range(H_A):
            sl = slice(h * DV_A, (h + 1) * DV_A)
            out = _diff_norm_gate(o[2 * h:2 * h + 1, sl], o[2 * h + 1:2 * h + 2, sl], lam,
                                  gna_ref[:, sl], ga_ref[:, sl].astype(F32))
            out_ref[:, sl] = out.astype(BF16)


def _paged_attn(page_table, q3, kn3, vn3, ga3, gna, lam_tile, ck, cv):
    nreq, npages = page_table.shape
    pp = PAGES_PER_STEP
    ng = npages // pp

    def req(width):
        return pl.BlockSpec((None, 1, width), lambda r, g, pt: (r, 0, 0))

    def page(i):
        return pl.BlockSpec((None, PAGE, 512), lambda r, g, pt: (pt[r, g * pp + i], 0, 0))

    return pl.pallas_call(
        _paged_kernel,
        grid_spec=pltpu.PrefetchScalarGridSpec(
            num_scalar_prefetch=1,
            grid=(nreq, ng),
            in_specs=[req(512), req(512), req(512), req(512),
                      pl.BlockSpec((1, W_A), lambda r, g, pt: (0, 0)),
                      pl.BlockSpec((8, 128), lambda r, g, pt: (0, 0))]
                     + [page(i) for i in range(pp)] + [page(i) for i in range(pp)],
            out_specs=req(512),
            scratch_shapes=[
                pltpu.VMEM((pp * PAGE, 512), BF16),
                pltpu.VMEM((pp * PAGE, 512), BF16),
                pltpu.VMEM((8, 128), F32),
                pltpu.VMEM((8, 128), F32),
                pltpu.VMEM((8, W_A), F32),
            ],
        ),
        out_shape=jax.ShapeDtypeStruct((nreq, 1, W_A), BF16),
        compiler_params=pltpu.CompilerParams(
            dimension_semantics=("arbitrary", "arbitrary"), vmem_limit_bytes=VMEM_LIMIT),
        name="paged_attn",
    )(page_table, q3, kn3, vn3, ga3, gna, lam_tile, *([ck] * pp), *([cv] * pp))


def _mstep_kernel(q_ref, k_ref, v_ref, o_ref, gb_ref, zif_ref, c_ref, n_ref, m_ref, gnb_ref,
                  y_ref, co_ref, no_ref, mo_ref):
    rb = STEP_RB
    x = zif_ref[...]
    sub = lax.broadcasted_iota(jnp.int32, (rb, 128), 0)
    lane = lax.broadcasted_iota(jnp.int32, (rb, 128), 1)
    m_out = jnp.zeros((rb, 128), F32)
    for h in range(H_B):
        sl = slice(h * 128, (h + 1) * 128)
        qb = q_ref[:, sl]
        qh = qb.astype(F32)
        kh = k_ref[:, sl].astype(F32)
        vh = v_ref[:, sl].astype(F32)
        i_c = x[:, h:h + 1]
        b = _log_sigmoid(x[:, H_B + h:H_B + h + 1])
        m_prev = m_ref[:, h:h + 1]
        inter = b + m_prev
        m_t = jnp.maximum(inter, (b - b) + i_c)
        w_intra = jnp.exp(((b - b) + i_c) - m_t)
        w_inter = jnp.exp(inter - m_t)
        sw = jnp.sum(qh * kh, axis=1, keepdims=True) * w_intra
        cq = jnp.zeros((rb, 128), F32)
        for r in range(rb):
            res = _nt_dot(qb, c_ref[r, h].astype(BF16))
            cq = jnp.where(sub == r, res, cq)
        nh = n_ref[:, sl]
        num = sw * vh + w_inter * cq
        den = sw + w_inter * jnp.sum(nh * qh, axis=1, keepdims=True)
        hh = num / jnp.maximum(jnp.abs(den), jnp.exp(-m_t))
        y = _mlstm_out_gate(hh, o_ref[:, sl].astype(F32), gnb_ref[:, sl], gb_ref[:, sl].astype(F32))
        y_ref[:, sl] = y.astype(BF16)

        w_s = jnp.exp(((b - b) + i_c) - m_t)
        decay = jnp.exp((b + m_prev) - m_t)
        wv = w_s * vh
        for r in range(rb):
            vcol = jnp.broadcast_to(wv[r:r + 1, :], (DV_B, DK_B)).T
            co_ref[r, h] = decay[r:r + 1, :] * c_ref[r, h] + vcol * kh[r:r + 1, :]
        no_ref[:, sl] = decay * nh + w_s * kh
        m_out = jnp.where(lane == h, m_t, m_out)
    mo_ref[...] = m_out


def _mlstm_step(mq, mk, mv, mo, gb, zif, c0, n0, m0, gnb):
    nreq = mq.shape[0]
    rb = STEP_RB

    def rows(width):
        return pl.BlockSpec((rb, width), lambda i: (i, 0))

    cspec = pl.BlockSpec((rb, H_B, DV_B, DK_B), lambda i: (i, 0, 0, 0))
    return pl.pallas_call(
        _mstep_kernel,
        grid=(nreq // rb,),
        in_specs=[rows(512), rows(512), rows(512), rows(512), rows(512), rows(GATE_PAD),
                  cspec, rows(512), rows(H_B), pl.BlockSpec((1, W_B), lambda i: (0, 0))],
        out_specs=[rows(512), cspec, rows(512), rows(128)],
        out_shape=[
            jax.ShapeDtypeStruct((nreq, W_B), BF16),
            jax.ShapeDtypeStruct((nreq, H_B, DV_B, DK_B), F32),
            jax.ShapeDtypeStruct((nreq, H_B * DK_B), F32),
            jax.ShapeDtypeStruct((nreq, 128), F32),
        ],
        compiler_params=pltpu.CompilerParams(
            dimension_semantics=("arbitrary",), vmem_limit_bytes=VMEM_LIMIT),
        name="mlstm_step",
    )(mq, mk, mv, mo, gb, zif, c0, n0, m0, gnb)


def _out_kernel(ma_ref, mb_ref, w_ref, x_ref, gate_ref, y_ref):
    acc = _dot(ma_ref[...], w_ref[0:W_A, :]) + _dot(mb_ref[...], w_ref[W_A:W_A + W_B, :])
    y_ref[...] = x_ref[...] + gate_ref[...] * acc


def _out_proj(mix_a, mix_b, w_out, x2d, mod, per_row, tm):
    m_rows = x2d.shape[0]
    nt = m_rows // tm
    if per_row:
        gate_spec = pl.BlockSpec((tm, D_MODEL), lambda i: (i, 2))
    else:
        tiles_per_batch = nt // mod.shape[0]
        gate_spec = pl.BlockSpec((None, 1, D_MODEL), lambda i: (i // tiles_per_batch, 0, 2))
    return pl.pallas_call(
        _out_kernel,
        grid=(nt,),
        in_specs=[
            pl.BlockSpec((tm, W_A), lambda i: (i, 0)),
            pl.BlockSpec((tm, W_B), lambda i: (i, 0)),
            pl.BlockSpec((W_A + W_B, D_MODEL), lambda i: (0, 0)),
            pl.BlockSpec((tm, D_MODEL), lambda i: (i, 0)),
            gate_spec,
        ],
        out_specs=pl.BlockSpec((tm, D_MODEL), lambda i: (i, 0)),
        out_shape=jax.ShapeDtypeStruct((m_rows, D_MODEL), F32),
        compiler_params=pltpu.CompilerParams(
            dimension_semantics=("arbitrary",), vmem_limit_bytes=VMEM_LIMIT),
        name="out_rows" if per_row else "out_bcast",
    )(mix_a, mix_b, w_out, x2d, mod)


def kernel(x_prompt, x_sample, c_prompt, c_sample, cache_k, cache_v, state_C, state_n, state_m,
           page_table, norm_g, w_ada, b_ada, w_in, b_in, g_q, g_k, lam_q, lam_k, gn_a, gn_b, w_out):
    assert w_in.shape[0] == 1, "single-layer model"
    nb, seq, _ = x_prompt.shape
    nreq = x_sample.shape[0]
    assert x_sample.shape[1] == 1

    w = w_in[0]
    bvec = b_in[0]
    gate_lo = N_MAIN
    gb_lo = N_MAIN + 2 * H_B
    wm = w[:, :N_MAIN].astype(BF16)
    wif = jnp.pad(w[:, gate_lo:gb_lo], ((0, 0), (0, GATE_PAD - 2 * H_B))).astype(BF16)
    wgb = w[:, gb_lo:].astype(BF16)
    bm = bvec[None, :N_MAIN]
    bif = jnp.pad(bvec[gate_lo:gb_lo], (0, GATE_PAD - 2 * H_B))[None, :]
    bgb = bvec[None, gb_lo:]
    gq = jnp.tile(g_q[0], 2 * H_A)[None, :]
    gk = jnp.tile(g_k[0], 2 * H_A)[None, :]
    grp = jnp.arange(A_QK) // DK_A
    seg = (grp[:, None] == grp[None, :]).astype(BF16)
    weights = (norm_g[0][None, :], wm, wif, wgb, bm, bif, bgb, gq, gk, seg)
    gna = gn_a[0].reshape(1, W_A)
    gnb = gn_b[0].reshape(1, W_B)
    wo = w_out[0].astype(BF16)
    slopes = jnp.asarray(ALIBI_SLOPES, F32)

    pad_rows = (-(nb + nreq)) % 8
    c_all = jnp.concatenate([c_prompt, c_sample, jnp.zeros((pad_rows, D_MODEL), F32)], axis=0)
    mod, lam_tile = _ada(c_all, w_ada[0], b_ada[0][None, :], lam_q[0], lam_k[0])
    mod_p = mod[:nb].reshape(nb, 1, 3 * D_MODEL)
    mod_s = mod[nb:nb + nreq]

    xp = x_prompt.reshape(nb * seq, D_MODEL)
    (q, kf, kb, vf, vb, ga, mq, mk, mv, mo, zif, gb) = _proj(xp, mod_p, False, PROJ_TM, weights)
    mix_a = _attn_prompt(slopes, q, kb, vb, ga, gna, lam_tile, nb, seq)
    mix_b, c_p, n_p, m_p = _mlstm_prompt(mq, mk, mv, mo, gb, zif, gnb, nb, seq)
    y_p = _out_proj(mix_a, mix_b, wo, xp, mod_p, False, OUT_TM)

    xs = x_sample.reshape(nreq, D_MODEL)
    (qs, kfs, kbs, vfs, vbs, gas, mqs, mks, mvs, mos, zifs, gbs) = _proj(xs, mod_s, True, nreq, weights)
    ck = cache_k[0].reshape(cache_k.shape[1], PAGE, A_QK)
    cv = cache_v[0].reshape(cache_v.shape[1], PAGE, W_A)
    r3 = lambda a: a.reshape(nreq, 1, a.shape[-1])
    mix_as = _paged_attn(page_table, r3(qs), r3(kbs), r3(vbs), r3(gas), gna, lam_tile, ck, cv)
    mix_bs, c_s, n_s, m_s = _mlstm_step(mqs, mks, mvs, mos, gbs, zifs, state_C[0],
                                        state_n[0].reshape(nreq, H_B * DK_B), state_m[0], gnb)
    y_s = _out_proj(mix_as.reshape(nreq, W_A), mix_bs, wo, xs, mod_s, True, nreq)

    return (
        y_p.reshape(nb, seq, D_MODEL),
        y_s.reshape(nreq, 1, D_MODEL),
        kf.reshape(1, nb, seq, H_A, 2, DK_A),
        vf.reshape(1, nb, seq, H_A, DV_A),
        kfs.reshape(1, nreq, 1, H_A, 2, DK_A),
        vfs.reshape(1, nreq, 1, H_A, DV_A),
        c_p[None],
        n_p[None],
        m_p[:, :, 0][None],
        c_s[None],
        n_s.reshape(1, nreq, H_B, DK_B),
        m_s[:, :H_B][None],
    )
```

```python
import functools
import math

import jax
import jax.numpy as jnp
from jax import lax
from jax.experimental import pallas as pl
from jax.experimental.pallas import tpu as pltpu

F32 = jnp.float32
BF16 = jnp.bfloat16

D_MODEL = 1024
H_A = 4
DK_A = 64
DV_A = 128
A_QK = H_A * 2 * DK_A
W_A = H_A * DV_A
H_B = 4
DK_B = 128
DV_B = 128
W_B = H_B * DV_B
N_MAIN = 8 * 512
GATE_PAD = 128
PAGE = 128
CHUNK = 128
EPS = 1e-6
NEG = -1e30
LAM_INIT = 0.8 - 0.6 * math.exp(-0.3 * 0)
ALIBI_SLOPES = tuple(2.0 ** (-8.0 * (h + 1) / H_A) for h in range(H_A))

VMEM_LIMIT = 56 * 1024 * 1024

PROJ_TM = 256
OUT_TM = 512
ATT_T = 512
PAGES_PER_STEP = 8
STEP_RB = 8


def _nt_dot(a, b):
    return lax.dot_general(a, b, (((1,), (1,)), ((), ())), preferred_element_type=F32)


def _dot(a, b):
    return jnp.dot(a, b, preferred_element_type=F32)


def _sigmoid(x):
    return 1.0 / (1.0 + jnp.exp(-x))


def _silu(x):
    return x * _sigmoid(x)


def _log_sigmoid(x):
    return jnp.minimum(x, 0.0) - jnp.log1p(jnp.exp(-jnp.abs(x)))


def _split3(a):
    a1 = a.astype(BF16)
    r1 = a - a1.astype(F32)
    a2 = r1.astype(BF16)
    a3 = (r1 - a2.astype(F32)).astype(BF16)
    return a1, a2, a3


def _ada_kernel(c_ref, w_ref, b_ref, lq_ref, lk_ref, mod_ref, lam_ref):
    a = _silu(c_ref[...])
    w = w_ref[...]
    a1, a2, _ = _split3(a)
    w1, w2, _ = _split3(w)
    mod_ref[...] = (_dot(a1, w1) + (_dot(a1, w2) + _dot(a2, w1))) + b_ref[...]

    @pl.when(pl.program_id(0) == 0)
    def _():
        s = jnp.sum(lq_ref[...] * lk_ref[...], axis=1, keepdims=True)
        e = jnp.exp(s)
        lam = e[0:1, :] - e[1:2, :] + LAM_INIT
        lam_ref[...] = jnp.broadcast_to(lam, lam_ref.shape)


def _ada(c_all, w_ada, b_ada, lam_q, lam_k):
    rows = c_all.shape[0]
    nblk = 3
    return pl.pallas_call(
        _ada_kernel,
        grid=(nblk,),
        in_specs=[
            pl.BlockSpec((rows, D_MODEL), lambda j: (0, 0)),
            pl.BlockSpec((D_MODEL, D_MODEL), lambda j: (0, j)),
            pl.BlockSpec((1, D_MODEL), lambda j: (0, j)),
            pl.BlockSpec((2, DK_A), lambda j: (0, 0)),
            pl.BlockSpec((2, DK_A), lambda j: (0, 0)),
        ],
        out_specs=[
            pl.BlockSpec((rows, D_MODEL), lambda j: (0, j)),
            pl.BlockSpec((8, 128), lambda j: (0, 0)),
        ],
        out_shape=[
            jax.ShapeDtypeStruct((rows, 3 * D_MODEL), F32),
            jax.ShapeDtypeStruct((8, 128), F32),
        ],
        compiler_params=pltpu.CompilerParams(
            dimension_semantics=("arbitrary",), vmem_limit_bytes=VMEM_LIMIT),
        name="ada",
    )(c_all, w_ada, b_ada, lam_q, lam_k)


def _proj_kernel(x_ref, shift_ref, scale_ref, ng_ref, wm_ref, wif_ref, wgb_ref,
                 bm_ref, bif_ref, bgb_ref, gq_ref, gk_ref, seg_ref,
                 q_ref, kf_ref, kb_ref, vf_ref, vb_ref, ga_ref,
                 mq_ref, mk_ref, mv_ref, mo_ref, zif_ref, gb_ref):
    x = x_ref[...]
    ms = jnp.mean(x * x, axis=-1, keepdims=True)
    h = x * lax.rsqrt(ms + EPS) * ng_ref[...]
    h = h * (1.0 + scale_ref[...]) + shift_ref[...]
    hb = h.astype(BF16)

    def col(j):
        sl = slice(j * 512, (j + 1) * 512)
        return _dot(hb, wm_ref[:, sl]) + bm_ref[:, sl]

    def headnorm(z, g):
        ss = _dot((z * z).astype(BF16), seg_ref[...])
        return z * lax.rsqrt(ss * (1.0 / DK_A) + EPS) * g

    q_ref[...] = (headnorm(col(0), gq_ref[...]) * (DK_A ** -0.5)).astype(BF16)
    k = headnorm(col(1), gk_ref[...])
    kf_ref[...] = k
    kb_ref[...] = k.astype(BF16)
    v = col(2)
    vf_ref[...] = v
    vb_ref[...] = v.astype(BF16)
    ga_ref[...] = col(3).astype(BF16)
    mq_ref[...] = col(4).astype(BF16)
    mk_ref[...] = (col(5) * (DK_B ** -0.5)).astype(BF16)
    mv_ref[...] = col(6).astype(BF16)
    mo_ref[...] = _sigmoid(col(7)).astype(BF16)
    zif_ref[...] = _dot(hb, wif_ref[...]) + bif_ref[...]
    gb_ref[...] = (_dot(hb, wgb_ref[...]) + bgb_ref[...]).astype(BF16)


def _proj(x2d, mod, per_row, tm, weights):
    (ng, wm, wif, wgb, bm, bif, bgb, gq, gk, seg) = weights
    m_rows = x2d.shape[0]
    nt = m_rows // tm
    if per_row:
        shift_spec = pl.BlockSpec((tm, D_MODEL), lambda i: (i, 0))
        scale_spec = pl.BlockSpec((tm, D_MODEL), lambda i: (i, 1))
    else:
        tiles_per_batch = nt // mod.shape[0]
        shift_spec = pl.BlockSpec((None, 1, D_MODEL), lambda i: (i // tiles_per_batch, 0, 0))
        scale_spec = pl.BlockSpec((None, 1, D_MODEL), lambda i: (i // tiles_per_batch, 0, 1))

    def const(shape):
        return pl.BlockSpec(shape, lambda i: (0, 0))

    def rows(width):
        return pl.BlockSpec((tm, width), lambda i: (i, 0))

    out_dtypes = [BF16, F32, BF16, F32, BF16, BF16, BF16, BF16, BF16, BF16, F32, BF16]
    out_widths = [512, 512, 512, 512, 512, 512, 512, 512, 512, 512, GATE_PAD, 512]
    return pl.pallas_call(
        _proj_kernel,
        grid=(nt,),
        in_specs=[
            rows(D_MODEL), shift_spec, scale_spec, const((1, D_MODEL)),
            const((D_MODEL, N_MAIN)), const((D_MODEL, GATE_PAD)), const((D_MODEL, W_B)),
            const((1, N_MAIN)), const((1, GATE_PAD)), const((1, W_B)),
            const((1, A_QK)), const((1, A_QK)), const((A_QK, A_QK)),
        ],
        out_specs=[rows(w) for w in out_widths],
        out_shape=[jax.ShapeDtypeStruct((m_rows, w), d) for w, d in zip(out_widths, out_dtypes)],
        compiler_params=pltpu.CompilerParams(
            dimension_semantics=("arbitrary",), vmem_limit_bytes=VMEM_LIMIT),
        name="proj_rows" if per_row else "proj_bcast",
    )(x2d, mod, mod, ng, wm, wif, wgb, bm, bif, bgb, gq, gk, seg)


def _diff_norm_gate(o0, o1, lam, gna, ga):
    d = o0 - lam * o1
    ya = d * lax.rsqrt(jnp.mean(d * d, axis=-1, keepdims=True) + EPS) * gna
    ya = ya * (1.0 - LAM_INIT)
    return ya * _silu(ga)


def _mlstm_out_gate(hh, o, gnb, gb):
    hg = o * hh
    yb = hg * lax.rsqrt(jnp.mean(hg * hg, axis=-1, keepdims=True) + EPS) * gnb
    return yb * _silu(gb)


def _attn_kernel(slopes_ref, q_ref, k_ref, v_ref, ga_ref, gna_ref, lam_ref, out_ref):
    t = ATT_T
    hidx = pl.program_id(1)
    qi = pl.program_id(2)
    slope = slopes_ref[hidx]

    q = q_ref[...].astype(F32)
    lane = lax.broadcasted_iota(jnp.int32, (t, 2 * DK_A), 1)
    qq = jnp.concatenate([jnp.where(lane < DK_A, q, 0.0), jnp.where(lane >= DK_A, q, 0.0)],
                         axis=0).astype(BF16)

    row = lax.broadcasted_iota(jnp.int32, (2 * t, t), 0)
    colv = lax.broadcasted_iota(jnp.int32, (2 * t, t), 1)
    rel = colv - jnp.where(row >= t, row - t, row)
    bias0 = slope * rel.astype(F32)

    def block(kj, carry, masked):
        m, l, acc = carry
        off = pl.multiple_of(kj * t, t)
        kb = k_ref[pl.ds(off, t), :]
        vb = v_ref[pl.ds(off, t), :]
        s1 = _nt_dot(qq, kb) + bias0
        if masked:
            s1 = jnp.where(rel <= 0, s1, NEG)
        c = slope * ((kj - qi) * t).astype(F32)
        m_new = jnp.maximum(m, jnp.max(s1, axis=1, keepdims=True) + c)
        alpha = jnp.exp(m - m_new)
        p = jnp.exp(s1 - (m_new - c))
        l = alpha * l + jnp.sum(p, axis=1, keepdims=True)
        acc = alpha * acc + _dot(p.astype(BF16), vb)
        return m_new, l, acc

    init = (jnp.full((2 * t, 1), NEG, F32), jnp.zeros((2 * t, 1), F32), jnp.zeros((2 * t, DV_A), F32))
    carry = lax.fori_loop(0, qi, lambda kj, cr: block(kj, cr, False), init)
    m, l, acc = block(qi, carry, True)
    o = acc / l
    lam = lam_ref[0:1, :]
    out = _diff_norm_gate(o[:t], o[t:], lam, gna_ref[...], ga_ref[...].astype(F32))
    out_ref[...] = out.astype(BF16)


def _attn_prompt(slopes, q, k, v, ga, gna, lam_tile, nb, seq):
    nq = seq // ATT_T
    return pl.pallas_call(
        _attn_kernel,
        grid_spec=pltpu.PrefetchScalarGridSpec(
            num_scalar_prefetch=1,
            grid=(nb, H_A, nq),
            in_specs=[
                pl.BlockSpec((ATT_T, 128), lambda b, h, i, s: (b * nq + i, h)),
                pl.BlockSpec((seq, 128), lambda b, h, i, s: (b, h)),
                pl.BlockSpec((seq, 128), lambda b, h, i, s: (b, h)),
                pl.BlockSpec((ATT_T, 128), lambda b, h, i, s: (b * nq + i, h)),
                pl.BlockSpec((1, 128), lambda b, h, i, s: (0, h)),
                pl.BlockSpec((8, 128), lambda b, h, i, s: (0, 0)),
            ],
            out_specs=pl.BlockSpec((ATT_T, 128), lambda b, h, i, s: (b * nq + i, h)),
        ),
        out_shape=jax.ShapeDtypeStruct((nb * seq, W_A), BF16),
        compiler_params=pltpu.CompilerParams(
            dimension_semantics=("arbitrary", "arbitrary", "arbitrary"),
            vmem_limit_bytes=VMEM_LIMIT),
        name="attn_prompt",
    )(slopes, q, k, v, ga, gna, lam_tile)


def _mlstm_kernel(q_ref, k_ref, v_ref, o_ref, gb_ref, zif_ref, gnb_ref,
                  y_ref, c_ref, n_ref, m_ref):
    L = CHUNK

    @pl.when(pl.program_id(1) == 0)
    def _():
        c_ref[...] = jnp.zeros_like(c_ref)
        n_ref[...] = jnp.zeros_like(n_ref)
        m_ref[...] = jnp.zeros_like(m_ref)

    x = zif_ref[...]
    logf = _log_sigmoid(x)
    row = lax.broadcasted_iota(jnp.int32, (L, L), 0)
    colv = lax.broadcasted_iota(jnp.int32, (L, L), 1)
    causal = row >= colv
    tri = jnp.where(causal, 1.0, 0.0).astype(BF16)
    f1, f2, f3 = _split3(logf)
    bmat = _dot(tri, f1) + (_dot(tri, f2) + _dot(tri, f3))
    xt = x.T
    bt = bmat.T

    for h in range(H_B):
        sl = slice(h * 128, (h + 1) * 128)
        qh = q_ref[:, sl]
        kh = k_ref[:, sl]
        vh = v_ref[:, sl]
        icol = x[:, h:h + 1]
        bcol = bmat[:, H_B + h:H_B + h + 1]
        irow = xt[h:h + 1, :]
        brow = bt[H_B + h:H_B + h + 1, :]
        m_prev = m_ref[h:h + 1, 0:1]
        nrow = n_ref[h:h + 1, :]
        c_old = c_ref[h]

        d = jnp.where(causal, (bcol - brow) + irow, NEG)
        inter = bcol + m_prev
        m_t = jnp.maximum(inter, jnp.max(d, axis=1, keepdims=True))
        w_intra = jnp.exp(d - m_t)
        w_inter = jnp.exp(inter - m_t)
        sw = _nt_dot(qh, kh) * w_intra
        num = _dot(sw.astype(BF16), vh) + w_inter * _nt_dot(qh, c_old.astype(BF16))
        den = jnp.sum(sw, axis=1, keepdims=True) \
            + w_inter * jnp.sum(qh.astype(F32) * nrow, axis=1, keepdims=True)
        hh = num / jnp.maximum(jnp.abs(den), jnp.exp(-m_t))
        y = _mlstm_out_gate(hh, o_ref[:, sl].astype(F32), gnb_ref[:, sl], gb_ref[:, sl].astype(F32))
        y_ref[:, sl] = y.astype(BF16)

        m_new = m_t[L - 1:L, :]
        b_last = bcol[L - 1:L, :]
        w_s = jnp.exp(((b_last - bcol) + icol) - m_new)
        decay = jnp.exp((b_last + m_prev) - m_new)
        wv = w_s * vh.astype(F32)
        c_ref[h] = decay * c_old + _dot(wv.T.astype(BF16), kh)
        n_ref[h:h + 1, :] = decay * nrow + jnp.sum(w_s * kh.astype(F32), axis=0, keepdims=True)
        m_ref[h:h + 1, :] = jnp.broadcast_to(m_new, (1, 128))


def _mlstm_prompt(mq, mk, mv, mo, gb, zif, gnb, nb, seq):
    nc = seq // CHUNK

    def rows(width):
        return pl.BlockSpec((CHUNK, width), lambda b, c: (b * nc + c, 0))

    return pl.pallas_call(
        _mlstm_kernel,
        grid=(nb, nc),
        in_specs=[rows(512), rows(512), rows(512), rows(512), rows(512), rows(GATE_PAD),
                  pl.BlockSpec((1, W_B), lambda b, c: (0, 0))],
        out_specs=[
            rows(512),
            pl.BlockSpec((None, H_B, DV_B, DK_B), lambda b, c: (b, 0, 0, 0)),
            pl.BlockSpec((None, H_B, DK_B), lambda b, c: (b, 0, 0)),
            pl.BlockSpec((None, H_B, 128), lambda b, c: (b, 0, 0)),
        ],
        out_shape=[
            jax.ShapeDtypeStruct((nb * seq, W_B), BF16),
            jax.ShapeDtypeStruct((nb, H_B, DV_B, DK_B), F32),
            jax.ShapeDtypeStruct((nb, H_B, DK_B), F32),
            jax.ShapeDtypeStruct((nb, H_B, 128), F32),
        ],
        compiler_params=pltpu.CompilerParams(
            dimension_semantics=("arbitrary", "arbitrary"), vmem_limit_bytes=VMEM_LIMIT),
        name="mlstm_prompt",
    )(mq, mk, mv, mo, gb, zif, gnb)


def _paged_kernel(pt_ref, q_ref, kn_ref, vn_ref, ga_ref, gna_ref, lam_ref, *rest):
    pp = PAGES_PER_STEP
    k_refs = rest[:pp]
    v_refs = rest[pp:2 * pp]
    out_ref = rest[2 * pp]
    kb_s, vb_s, m_s, l_s, acc_s = rest[2 * pp + 1:]
    g = pl.program_id(1)
    ng = pl.num_programs(1)
    ntok = pp * PAGE

    @pl.when(g == 0)
    def _():
        m_s[...] = jnp.full_like(m_s, NEG)
        l_s[...] = jnp.zeros_like(l_s)
        acc_s[...] = jnp.zeros_like(acc_s)

    for i in range(pp):
        kb_s[:, i * PAGE:(i + 1) * PAGE] = k_refs[i][...].astype(BF16)
        for h in range(H_A):
            vb_s[h, i * PAGE:(i + 1) * PAGE, :] = v_refs[i][pl.ds(h, PAGE, stride=H_A), :].astype(BF16)

    q = q_ref[...].astype(F32)
    sub = lax.broadcasted_iota(jnp.int32, (8, A_QK), 0)
    lane = lax.broadcasted_iota(jnp.int32, (8, A_QK), 1)
    qbd32 = jnp.where((lane >> 6) == sub, jnp.broadcast_to(q, (8, A_QK)), 0.0)
    qbd = qbd32.astype(BF16)

    j = lax.broadcasted_iota(jnp.int32, (8, 1), 0)
    slope = jnp.where(j < 2, ALIBI_SLOPES[0],
                      jnp.where(j < 4, ALIBI_SLOPES[1],
                                jnp.where(j < 6, ALIBI_SLOPES[2], ALIBI_SLOPES[3]))).astype(F32)
    kpos = g * ntok + lax.broadcasted_iota(jnp.int32, (1, ntok), 1)
    dist = (ng * ntok - kpos).astype(F32)

    s = _dot(qbd, kb_s[...]) - slope * dist
    m_old = m_s[:, 0:1]
    l_old = l_s[:, 0:1]
    m_new = jnp.maximum(m_old, jnp.max(s, axis=1, keepdims=True))
    alpha = jnp.exp(m_old - m_new)
    p = jnp.exp(s - m_new)
    l_new = alpha * l_old + jnp.sum(p, axis=1, keepdims=True)
    pb = p.astype(BF16)
    head_of_row = lax.broadcasted_iota(jnp.int32, (8, DV_A), 0) >> 1
    pv = jnp.zeros((8, DV_A), F32)
    for h in range(H_A):
        pv = jnp.where(head_of_row == h, _dot(pb, vb_s[h]), pv)
    acc = alpha * acc_s[...] + pv
    m_s[...] = jnp.broadcast_to(m_new, m_s.shape)
    l_s[...] = jnp.broadcast_to(l_new, l_s.shape)
    acc_s[...] = acc

    @pl.when(g == ng - 1)
    def _():
        kn = kn_ref[...].astype(F32)
        s_new = jnp.sum(qbd32 * kn, axis=1, keepdims=True)
        m_fin = jnp.maximum(m_new, s_new)
        a2 = jnp.exp(m_new - m_fin)
        p_new = jnp.exp(s_new - m_fin)
        l_fin = a2 * l_new + p_new
        vn = jnp.zeros((8, DV_A), F32)
        for h in range(H_A):
            vrow = vn_ref[:, h * DV_A:(h + 1) * DV_A].astype(F32)
            vn = jnp.where(head_of_row == h, jnp.broadcast_to(vrow, (8, DV_A)), vn)
        o = (a2 * acc + p_new * vn) / l_fin
        lam = lam_ref[0:1, :]
        for h in range(H_A):
            sl = slice(h * DV_A, (h + 1) * DV_A)
            out = _diff_norm_gate(o[2 * h:2 * h + 1, :], o[2 * h + 1:2 * h + 2, :], lam,
                                  gna_ref[:, sl], ga_ref[:, sl].astype(F32))
            out_ref[:, sl] = out.astype(BF16)


def _paged_attn(page_table, q3, kn3, vn3, ga3, gna, lam_tile, ck, cv):
    nreq, npages = page_table.shape
    pp = PAGES_PER_STEP
    ng = npages // pp

    def req(width):
        return pl.BlockSpec((None, 1, width), lambda r, g, pt: (r, 0, 0))

    def page(i):
        return pl.BlockSpec((None, 512, PAGE), lambda r, g, pt: (pt[r, g * pp + i], 0, 0))

    return pl.pallas_call(
        _paged_kernel,
        grid_spec=pltpu.PrefetchScalarGridSpec(
            num_scalar_prefetch=1,
            grid=(nreq, ng),
            in_specs=[req(512), req(512), req(512), req(512),
                      pl.BlockSpec((1, W_A), lambda r, g, pt: (0, 0)),
                      pl.BlockSpec((8, 128), lambda r, g, pt: (0, 0))]
                     + [page(i) for i in range(pp)] + [page(i) for i in range(pp)],
            out_specs=req(512),
            scratch_shapes=[
                pltpu.VMEM((A_QK, pp * PAGE), BF16),
                pltpu.VMEM((H_A, pp * PAGE, DV_A), BF16),
                pltpu.VMEM((8, 128), F32),
                pltpu.VMEM((8, 128), F32),
                pltpu.VMEM((8, DV_A), F32),
            ],
        ),
        out_shape=jax.ShapeDtypeStruct((nreq, 1, W_A), BF16),
        compiler_params=pltpu.CompilerParams(
            dimension_semantics=("arbitrary", "arbitrary"), vmem_limit_bytes=VMEM_LIMIT),
        name="paged_attn",
    )(page_table, q3, kn3, vn3, ga3, gna, lam_tile, *([ck] * pp), *([cv] * pp))


def _mstep_kernel(q_ref, k_ref, v_ref, o_ref, gb_ref, zif_ref, c_ref, n_ref, m_ref, gnb_ref,
                  y_ref, co_ref, no_ref, mo_ref):
    rb = STEP_RB
    x = zif_ref[...]
    sub = lax.broadcasted_iota(jnp.int32, (rb, 128), 0)
    lane = lax.broadcasted_iota(jnp.int32, (rb, 128), 1)
    m_out = jnp.zeros((rb, 128), F32)
    for h in range(H_B):
        sl = slice(h * 128, (h + 1) * 128)
        qb = q_ref[:, sl]
        qh = qb.astype(F32)
        kh = k_ref[:, sl].astype(F32)
        vh = v_ref[:, sl].astype(F32)
        i_c = x[:, h:h + 1]
        b = _log_sigmoid(x[:, H_B + h:H_B + h + 1])
        m_prev = m_ref[:, h:h + 1]
        inter = b + m_prev
        m_t = jnp.maximum(inter, (b - b) + i_c)
        w_intra = jnp.exp(((b - b) + i_c) - m_t)
        w_inter = jnp.exp(inter - m_t)
        sw = jnp.sum(qh * kh, axis=1, keepdims=True) * w_intra
        cq = jnp.zeros((rb, 128), F32)
        for r in range(rb):
            res = _nt_dot(qb, c_ref[r, h].astype(BF16))
            cq = jnp.where(sub == r, res, cq)
        nh = n_ref[:, sl]
        num = sw * vh + w_inter * cq
        den = sw + w_inter * jnp.sum(nh * qh, axis=1, keepdims=True)
        hh = num / jnp.maximum(jnp.abs(den), jnp.exp(-m_t))
        y = _mlstm_out_gate(hh, o_ref[:, sl].astype(F32), gnb_ref[:, sl], gb_ref[:, sl].astype(F32))
        y_ref[:, sl] = y.astype(BF16)

        w_s = jnp.exp(((b - b) + i_c) - m_t)
        decay = jnp.exp((b + m_prev) - m_t)
        wv = w_s * vh
        for r in range(rb):
            vcol = jnp.broadcast_to(wv[r:r + 1, :], (DV_B, DK_B)).T
            co_ref[r, h] = decay[r:r + 1, :] * c_ref[r, h] + vcol * kh[r:r + 1, :]
        no_ref[:, sl] = decay * nh + w_s * kh
        m_out = jnp.where(lane == h, m_t, m_out)
    mo_ref[...] = m_out


def _mlstm_step(mq, mk, mv, mo, gb, zif, c0, n0, m0, gnb):
    nreq = mq.shape[0]
    rb = STEP_RB

    def rows(width):
        return pl.BlockSpec((rb, width), lambda i: (i, 0))

    cspec = pl.BlockSpec((rb, H_B, DV_B, DK_B), lambda i: (i, 0, 0, 0))
    return pl.pallas_call(
        _mstep_kernel,
        grid=(nreq // rb,),
        in_specs=[rows(512), rows(512), rows(512), rows(512), rows(512), rows(GATE_PAD),
                  cspec, rows(512), rows(H_B), pl.BlockSpec((1, W_B), lambda i: (0, 0))],
        out_specs=[rows(512), cspec, rows(512), rows(128)],
        out_shape=[
            jax.ShapeDtypeStruct((nreq, W_B), BF16),
            jax.ShapeDtypeStruct((nreq, H_B, DV_B, DK_B), F32),
            jax.ShapeDtypeStruct((nreq, H_B * DK_B), F32),
            jax.ShapeDtypeStruct((nreq, 128), F32),
        ],
        compiler_params=pltpu.CompilerParams(
            dimension_semantics=("arbitrary",), vmem_limit_bytes=VMEM_LIMIT),
        name="mlstm_step",
    )(mq, mk, mv, mo, gb, zif, c0, n0, m0, gnb)


def _out_kernel(ma_ref, mb_ref, w_ref, x_ref, gate_ref, y_ref):
    acc = _dot(ma_ref[...], w_ref[0:W_A, :]) + _dot(mb_ref[...], w_ref[W_A:W_A + W_B, :])
    y_ref[...] = x_ref[...] + gate_ref[...] * acc


def _out_proj(mix_a, mix_b, w_out, x2d, mod, per_row, tm):
    m_rows = x2d.shape[0]
    nt = m_rows // tm
    if per_row:
        gate_spec = pl.BlockSpec((tm, D_MODEL), lambda i: (i, 2))
    else:
        tiles_per_batch = nt // mod.shape[0]
        gate_spec = pl.BlockSpec((None, 1, D_MODEL), lambda i: (i // tiles_per_batch, 0, 2))
    return pl.pallas_call(
        _out_kernel,
        grid=(nt,),
        in_specs=[
            pl.BlockSpec((tm, W_A), lambda i: (i, 0)),
            pl.BlockSpec((tm, W_B), lambda i: (i, 0)),
            pl.BlockSpec((W_A + W_B, D_MODEL), lambda i: (0, 0)),
            pl.BlockSpec((tm, D_MODEL), lambda i: (i, 0)),
            gate_spec,
        ],
        out_specs=pl.BlockSpec((tm, D_MODEL), lambda i: (i, 0)),
        out_shape=jax.ShapeDtypeStruct((m_rows, D_MODEL), F32),
        compiler_params=pltpu.CompilerParams(
            dimension_semantics=("arbitrary",), vmem_limit_bytes=VMEM_LIMIT),
        name="out_rows" if per_row else "out_bcast",
    )(mix_a, mix_b, w_out, x2d, mod)


def kernel(x_prompt, x_sample, c_prompt, c_sample, cache_k, cache_v, state_C, state_n, state_m,
           page_table, norm_g, w_ada, b_ada, w_in, b_in, g_q, g_k, lam_q, lam_k, gn_a, gn_b, w_out):
    assert w_in.shape[0] == 1, "single-layer model"
    nb, seq, _ = x_prompt.shape
    nreq = x_sample.shape[0]
    assert x_sample.shape[1] == 1

    w = w_in[0]
    bvec = b_in[0]
    gate_lo = N_MAIN
    gb_lo = N_MAIN + 2 * H_B
    wm = w[:, :N_MAIN].astype(BF16)
    wif = jnp.pad(w[:, gate_lo:gb_lo], ((0, 0), (0, GATE_PAD - 2 * H_B))).astype(BF16)
    wgb = w[:, gb_lo:].astype(BF16)
    bm = bvec[None, :N_MAIN]
    bif = jnp.pad(bvec[gate_lo:gb_lo], (0, GATE_PAD - 2 * H_B))[None, :]
    bgb = bvec[None, gb_lo:]
    gq = jnp.tile(g_q[0], 2 * H_A)[None, :]
    gk = jnp.tile(g_k[0], 2 * H_A)[None, :]
    grp = jnp.arange(A_QK) // DK_A
    seg = (grp[:, None] == grp[None, :]).astype(BF16)
    weights = (norm_g[0][None, :], wm, wif, wgb, bm, bif, bgb, gq, gk, seg)
    gna = gn_a[0].reshape(1, W_A)
    gnb = gn_b[0].reshape(1, W_B)
    wo = w_out[0].astype(BF16)
    slopes = jnp.asarray(ALIBI_SLOPES, F32)

    pad_rows = (-(nb + nreq)) % 8
    c_all = jnp.concatenate([c_prompt, c_sample, jnp.zeros((pad_rows, D_MODEL), F32)], axis=0)
    mod, lam_tile = _ada(c_all, w_ada[0], b_ada[0][None, :], lam_q[0], lam_k[0])
    mod_p = mod[:nb].reshape(nb, 1, 3 * D_MODEL)
    mod_s = mod[nb:nb + nreq]

    xp = x_prompt.reshape(nb * seq, D_MODEL)
    (q, kf, kb, vf, vb, ga, mq, mk, mv, mo, zif, gb) = _proj(xp, mod_p, False, PROJ_TM, weights)
    mix_a = _attn_prompt(slopes, q, kb, vb, ga, gna, lam_tile, nb, seq)
    mix_b, c_p, n_p, m_p = _mlstm_prompt(mq, mk, mv, mo, gb, zif, gnb, nb, seq)
    y_p = _out_proj(mix_a, mix_b, wo, xp, mod_p, False, OUT_TM)

    xs = x_sample.reshape(nreq, D_MODEL)
    (qs, kfs, kbs, vfs, vbs, gas, mqs, mks, mvs, mos, zifs, gbs) = _proj(xs, mod_s, True, nreq, weights)
    ck = jnp.transpose(cache_k[0], (0, 2, 3, 4, 1)).reshape(cache_k.shape[1], A_QK, PAGE)
    cv = cache_v[0].reshape(cache_v.shape[1], PAGE * H_A, DV_A)
    r3 = lambda a: a.reshape(nreq, 1, a.shape[-1])
    mix_as = _paged_attn(page_table, r3(qs), r3(kbs), r3(vbs), r3(gas), gna, lam_tile, ck, cv)
    mix_bs, c_s, n_s, m_s = _mlstm_step(mqs, mks, mvs, mos, gbs, zifs, state_C[0],
                                        state_n[0].reshape(nreq, H_B * DK_B), state_m[0], gnb)
    y_s = _out_proj(mix_as.reshape(nreq, W_A), mix_bs, wo, xs, mod_s, True, nreq)

    return (
        y_p.reshape(nb, seq, D_MODEL),
        y_s.reshape(nreq, 1, D_MODEL),
        kf.reshape(1, nb, seq, H_A, 2, DK_A),
        vf.reshape(1, nb, seq, H_A, DV_A),
        kfs.reshape(1, nreq, 1, H_A, 2, DK_A),
        vfs.reshape(1, nreq, 1, H_A, DV_A),
        c_p[None],
        n_p[None],
        m_p[:, :, 0][None],
        c_s[None],
        n_s.reshape(1, nreq, H_B, DK_B),
        m_s[:, :H_B][None],
    )
```

```python
import functools
import math

import jax
import jax.numpy as jnp
from jax import lax
from jax.experimental import pallas as pl
from jax.experimental.pallas import tpu as pltpu

F32 = jnp.float32
BF16 = jnp.bfloat16

D_MODEL = 1024
H_A = 4
DK_A = 64
DV_A = 128
A_QK = H_A * 2 * DK_A
W_A = H_A * DV_A
H_B = 4
DK_B = 128
DV_B = 128
W_B = H_B * DV_B
N_MAIN = 8 * 512
GATE_PAD = 128
PAGE = 128
CHUNK = 128
EPS = 1e-6
NEG = -1e30
LAM_INIT = 0.8 - 0.6 * math.exp(-0.3 * 0)
ALIBI_SLOPES = tuple(2.0 ** (-8.0 * (h + 1) / H_A) for h in range(H_A))

VMEM_LIMIT = 56 * 1024 * 1024

PROJ_TM = 256
OUT_TM = 512
ATT_T = 512
PAGES_PER_STEP = 32
STEP_RB = 8


def _nt_dot(a, b):
    return lax.dot_general(a, b, (((1,), (1,)), ((), ())), preferred_element_type=F32)


def _dot(a, b):
    return jnp.dot(a, b, preferred_element_type=F32)


def _sigmoid(x):
    return 1.0 / (1.0 + jnp.exp(-x))


def _silu(x):
    return x * _sigmoid(x)


def _log_sigmoid(x):
    return jnp.minimum(x, 0.0) - jnp.log1p(jnp.exp(-jnp.abs(x)))


def _split3(a):
    a1 = a.astype(BF16)
    r1 = a - a1.astype(F32)
    a2 = r1.astype(BF16)
    a3 = (r1 - a2.astype(F32)).astype(BF16)
    return a1, a2, a3


def _ada_kernel(c_ref, w_ref, b_ref, lq_ref, lk_ref, mod_ref, lam_ref):
    a = _silu(c_ref[...])
    w = w_ref[...]
    a1, a2, _ = _split3(a)
    w1, w2, _ = _split3(w)
    mod_ref[...] = (_dot(a1, w1) + (_dot(a1, w2) + _dot(a2, w1))) + b_ref[...]

    @pl.when(pl.program_id(0) == 0)
    def _():
        s = jnp.sum(lq_ref[...] * lk_ref[...], axis=1, keepdims=True)
        e = jnp.exp(s)
        lam = e[0:1, :] - e[1:2, :] + LAM_INIT
        lam_ref[...] = jnp.broadcast_to(lam, lam_ref.shape)


def _ada(c_all, w_ada, b_ada, lam_q, lam_k):
    rows = c_all.shape[0]
    nblk = 3
    return pl.pallas_call(
        _ada_kernel,
        grid=(nblk,),
        in_specs=[
            pl.BlockSpec((rows, D_MODEL), lambda j: (0, 0)),
            pl.BlockSpec((D_MODEL, D_MODEL), lambda j: (0, j)),
            pl.BlockSpec((1, D_MODEL), lambda j: (0, j)),
            pl.BlockSpec((2, DK_A), lambda j: (0, 0)),
            pl.BlockSpec((2, DK_A), lambda j: (0, 0)),
        ],
        out_specs=[
            pl.BlockSpec((rows, D_MODEL), lambda j: (0, j)),
            pl.BlockSpec((8, 128), lambda j: (0, 0)),
        ],
        out_shape=[
            jax.ShapeDtypeStruct((rows, 3 * D_MODEL), F32),
            jax.ShapeDtypeStruct((8, 128), F32),
        ],
        compiler_params=pltpu.CompilerParams(
            dimension_semantics=("arbitrary",), vmem_limit_bytes=VMEM_LIMIT),
        name="ada",
    )(c_all, w_ada, b_ada, lam_q, lam_k)


def _proj_kernel(x_ref, shift_ref, scale_ref, ng_ref, wm_ref, wif_ref, wgb_ref,
                 bm_ref, bif_ref, bgb_ref, gq_ref, gk_ref, seg_ref,
                 q_ref, kf_ref, kb_ref, vf_ref, vb_ref, ga_ref,
                 mq_ref, mk_ref, mv_ref, mo_ref, zif_ref, gb_ref, *, k_transposed):
    x = x_ref[...]
    ms = jnp.mean(x * x, axis=-1, keepdims=True)
    h = x * lax.rsqrt(ms + EPS) * ng_ref[...]
    h = h * (1.0 + scale_ref[...]) + shift_ref[...]
    hb = h.astype(BF16)

    def col(j):
        sl = slice(j * 512, (j + 1) * 512)
        return _dot(hb, wm_ref[:, sl]) + bm_ref[:, sl]

    def headnorm(z, g):
        ss = _dot((z * z).astype(BF16), seg_ref[...])
        return z * lax.rsqrt(ss * (1.0 / DK_A) + EPS) * g

    q_ref[...] = (headnorm(col(0), gq_ref[...]) * (DK_A ** -0.5)).astype(BF16)
    k = headnorm(col(1), gk_ref[...])
    kf_ref[...] = k.T if k_transposed else k
    kb_ref[...] = k.astype(BF16)
    v = col(2)
    for hh in range(H_A):
        vf_ref[pl.ds(hh, x.shape[0], stride=H_A), :] = v[:, hh * DV_A:(hh + 1) * DV_A]
    vb_ref[...] = v.astype(BF16)
    ga_ref[...] = col(3).astype(BF16)
    mq_ref[...] = col(4).astype(BF16)
    mk_ref[...] = (col(5) * (DK_B ** -0.5)).astype(BF16)
    mv_ref[...] = col(6).astype(BF16)
    mo_ref[...] = _sigmoid(col(7)).astype(BF16)
    zif_ref[...] = _dot(hb, wif_ref[...]) + bif_ref[...]
    gb_ref[...] = (_dot(hb, wgb_ref[...]) + bgb_ref[...]).astype(BF16)


def _proj(x2d, mod, per_row, tm, weights):
    (ng, wm, wif, wgb, bm, bif, bgb, gq, gk, seg) = weights
    m_rows = x2d.shape[0]
    nt = m_rows // tm
    if per_row:
        shift_spec = pl.BlockSpec((tm, D_MODEL), lambda i: (i, 0))
        scale_spec = pl.BlockSpec((tm, D_MODEL), lambda i: (i, 1))
    else:
        tiles_per_batch = nt // mod.shape[0]
        shift_spec = pl.BlockSpec((None, 1, D_MODEL), lambda i: (i // tiles_per_batch, 0, 0))
        scale_spec = pl.BlockSpec((None, 1, D_MODEL), lambda i: (i // tiles_per_batch, 0, 1))

    def const(shape):
        return pl.BlockSpec(shape, lambda i: (0, 0))

    def rows(width):
        return pl.BlockSpec((tm, width), lambda i: (i, 0))

    out_dtypes = [BF16, F32, BF16, F32, BF16, BF16, BF16, BF16, BF16, BF16, F32, BF16]
    out_widths = [512, 512, 512, DV_A, 512, 512, 512, 512, 512, 512, GATE_PAD, 512]
    out_rows = [1, 1, 1, H_A, 1, 1, 1, 1, 1, 1, 1, 1]
    out_specs = [pl.BlockSpec((tm * r, w), lambda i: (i, 0)) for w, r in zip(out_widths, out_rows)]
    out_shape = [jax.ShapeDtypeStruct((m_rows * r, w), d)
                 for w, r, d in zip(out_widths, out_rows, out_dtypes)]
    if not per_row:
        out_specs[1] = pl.BlockSpec((None, A_QK, tm),
                                    lambda i: (i // tiles_per_batch, 0, i % tiles_per_batch))
        out_shape[1] = jax.ShapeDtypeStruct((mod.shape[0], A_QK, m_rows // mod.shape[0]), F32)
    return pl.pallas_call(
        functools.partial(_proj_kernel, k_transposed=not per_row),
        grid=(nt,),
        in_specs=[
            rows(D_MODEL), shift_spec, scale_spec, const((1, D_MODEL)),
            const((D_MODEL, N_MAIN)), const((D_MODEL, GATE_PAD)), const((D_MODEL, W_B)),
            const((1, N_MAIN)), const((1, GATE_PAD)), const((1, W_B)),
            const((1, A_QK)), const((1, A_QK)), const((A_QK, A_QK)),
        ],
        out_specs=out_specs,
        out_shape=out_shape,
        compiler_params=pltpu.CompilerParams(
            dimension_semantics=("arbitrary",), vmem_limit_bytes=VMEM_LIMIT),
        name="proj_rows" if per_row else "proj_bcast",
    )(x2d, mod, mod, ng, wm, wif, wgb, bm, bif, bgb, gq, gk, seg)


def _diff_norm_gate(o0, o1, lam, gna, ga):
    d = o0 - lam * o1
    ya = d * lax.rsqrt(jnp.mean(d * d, axis=-1, keepdims=True) + EPS) * gna
    ya = ya * (1.0 - LAM_INIT)
    return ya * _silu(ga)


def _mlstm_out_gate(hh, o, gnb, gb):
    hg = o * hh
    yb = hg * lax.rsqrt(jnp.mean(hg * hg, axis=-1, keepdims=True) + EPS) * gnb
    return yb * _silu(gb)


def _attn_kernel(slopes_ref, q_ref, k_ref, v_ref, ga_ref, gna_ref, lam_ref, out_ref):
    t = ATT_T
    hidx = pl.program_id(1)
    qi = pl.program_id(2)
    slope = slopes_ref[hidx]

    q = q_ref[...].astype(F32)
    lane = lax.broadcasted_iota(jnp.int32, (t, 2 * DK_A), 1)
    qq = jnp.concatenate([jnp.where(lane < DK_A, q, 0.0), jnp.where(lane >= DK_A, q, 0.0)],
                         axis=0).astype(BF16)

    row = lax.broadcasted_iota(jnp.int32, (2 * t, t), 0)
    colv = lax.broadcasted_iota(jnp.int32, (2 * t, t), 1)
    rel = colv - jnp.where(row >= t, row - t, row)
    bias0 = slope * rel.astype(F32)

    def block(kj, carry, masked):
        m, l, acc = carry
        off = pl.multiple_of(kj * t, t)
        kb = k_ref[pl.ds(off, t), :]
        vb = v_ref[pl.ds(off, t), :]
        s1 = _nt_dot(qq, kb) + bias0
        if masked:
            s1 = jnp.where(rel <= 0, s1, NEG)
        c = slope * ((kj - qi) * t).astype(F32)
        m_new = jnp.maximum(m, jnp.max(s1, axis=1, keepdims=True) + c)
        alpha = jnp.exp(m - m_new)
        p = jnp.exp(s1 - (m_new - c))
        l = alpha * l + jnp.sum(p, axis=1, keepdims=True)
        acc = alpha * acc + _dot(p.astype(BF16), vb)
        return m_new, l, acc

    init = (jnp.full((2 * t, 1), NEG, F32), jnp.zeros((2 * t, 1), F32), jnp.zeros((2 * t, DV_A), F32))
    carry = lax.fori_loop(0, qi, lambda kj, cr: block(kj, cr, False), init)
    m, l, acc = block(qi, carry, True)
    o = acc / l
    lam = lam_ref[0:1, :]
    out = _diff_norm_gate(o[:t], o[t:], lam, gna_ref[...], ga_ref[...].astype(F32))
    out_ref[...] = out.astype(BF16)


def _attn_prompt(slopes, q, k, v, ga, gna, lam_tile, nb, seq):
    nq = seq // ATT_T
    return pl.pallas_call(
        _attn_kernel,
        grid_spec=pltpu.PrefetchScalarGridSpec(
            num_scalar_prefetch=1,
            grid=(nb, H_A, nq),
            in_specs=[
                pl.BlockSpec((ATT_T, 128), lambda b, h, i, s: (b * nq + i, h)),
                pl.BlockSpec((seq, 128), lambda b, h, i, s: (b, h)),
                pl.BlockSpec((seq, 128), lambda b, h, i, s: (b, h)),
                pl.BlockSpec((ATT_T, 128), lambda b, h, i, s: (b * nq + i, h)),
                pl.BlockSpec((1, 128), lambda b, h, i, s: (0, h)),
                pl.BlockSpec((8, 128), lambda b, h, i, s: (0, 0)),
            ],
            out_specs=pl.BlockSpec((ATT_T, 128), lambda b, h, i, s: (b * nq + i, h)),
        ),
        out_shape=jax.ShapeDtypeStruct((nb * seq, W_A), BF16),
        compiler_params=pltpu.CompilerParams(
            dimension_semantics=("arbitrary", "arbitrary", "arbitrary"),
            vmem_limit_bytes=VMEM_LIMIT),
        name="attn_prompt",
    )(slopes, q, k, v, ga, gna, lam_tile)


def _mlstm_kernel(q_ref, k_ref, v_ref, o_ref, gb_ref, zif_ref, gnb_ref,
                  y_ref, c_ref, n_ref, m_ref):
    L = CHUNK

    @pl.when(pl.program_id(1) == 0)
    def _():
        c_ref[...] = jnp.zeros_like(c_ref)
        n_ref[...] = jnp.zeros_like(n_ref)
        m_ref[...] = jnp.zeros_like(m_ref)

    x = zif_ref[...]
    logf = _log_sigmoid(x)
    row = lax.broadcasted_iota(jnp.int32, (L, L), 0)
    colv = lax.broadcasted_iota(jnp.int32, (L, L), 1)
    causal = row >= colv
    tri = jnp.where(causal, 1.0, 0.0).astype(BF16)
    f1, f2, f3 = _split3(logf)
    bmat = _dot(tri, f1) + (_dot(tri, f2) + _dot(tri, f3))
    xt = x.T
    bt = bmat.T

    for h in range(H_B):
        sl = slice(h * 128, (h + 1) * 128)
        qh = q_ref[:, sl]
        kh = k_ref[:, sl]
        vh = v_ref[:, sl]
        icol = x[:, h:h + 1]
        bcol = bmat[:, H_B + h:H_B + h + 1]
        irow = xt[h:h + 1, :]
        brow = bt[H_B + h:H_B + h + 1, :]
        m_prev = m_ref[h:h + 1, 0:1]
        nrow = n_ref[h:h + 1, :]
        c_old = c_ref[h]

        d = jnp.where(causal, (bcol - brow) + irow, NEG)
        inter = bcol + m_prev
        m_t = jnp.maximum(inter, jnp.max(d, axis=1, keepdims=True))
        w_intra = jnp.exp(d - m_t)
        w_inter = jnp.exp(inter - m_t)
        sw = _nt_dot(qh, kh) * w_intra
        num = _dot(sw.astype(BF16), vh) + w_inter * _nt_dot(qh, c_old.astype(BF16))
        den = jnp.sum(sw, axis=1, keepdims=True) \
            + w_inter * jnp.sum(qh.astype(F32) * nrow, axis=1, keepdims=True)
        hh = num / jnp.maximum(jnp.abs(den), jnp.exp(-m_t))
        y = _mlstm_out_gate(hh, o_ref[:, sl].astype(F32), gnb_ref[:, sl], gb_ref[:, sl].astype(F32))
        y_ref[:, sl] = y.astype(BF16)

        m_new = m_t[L - 1:L, :]
        b_last = bcol[L - 1:L, :]
        w_s = jnp.exp(((b_last - bcol) + icol) - m_new)
        decay = jnp.exp((b_last + m_prev) - m_new)
        wv = w_s * vh.astype(F32)
        c_ref[h] = decay * c_old + _dot(wv.T.astype(BF16), kh)
        n_ref[h:h + 1, :] = decay * nrow + jnp.sum(w_s * kh.astype(F32), axis=0, keepdims=True)
        m_ref[h:h + 1, :] = jnp.broadcast_to(m_new, (1, 128))


def _mlstm_prompt(mq, mk, mv, mo, gb, zif, gnb, nb, seq):
    nc = seq // CHUNK

    def rows(width):
        return pl.BlockSpec((CHUNK, width), lambda b, c: (b * nc + c, 0))

    return pl.pallas_call(
        _mlstm_kernel,
        grid=(nb, nc),
        in_specs=[rows(512), rows(512), rows(512), rows(512), rows(512), rows(GATE_PAD),
                  pl.BlockSpec((1, W_B), lambda b, c: (0, 0))],
        out_specs=[
            rows(512),
            pl.BlockSpec((None, H_B, DV_B, DK_B), lambda b, c: (b, 0, 0, 0)),
            pl.BlockSpec((None, H_B, DK_B), lambda b, c: (b, 0, 0)),
            pl.BlockSpec((None, H_B, 128), lambda b, c: (b, 0, 0)),
        ],
        out_shape=[
            jax.ShapeDtypeStruct((nb * seq, W_B), BF16),
            jax.ShapeDtypeStruct((nb, H_B, DV_B, DK_B), F32),
            jax.ShapeDtypeStruct((nb, H_B, DK_B), F32),
            jax.ShapeDtypeStruct((nb, H_B, 128), F32),
        ],
        compiler_params=pltpu.CompilerParams(
            dimension_semantics=("arbitrary", "arbitrary"), vmem_limit_bytes=VMEM_LIMIT),
        name="mlstm_prompt",
    )(mq, mk, mv, mo, gb, zif, gnb)


def _paged_kernel(pt_ref, q_ref, kn_ref, vn_ref, ga_ref, gna_ref, lam_ref, *rest):
    pp = PAGES_PER_STEP
    k_refs = rest[:pp]
    v_refs = rest[pp:2 * pp]
    out_ref = rest[2 * pp]
    kb_s, vb_s, m_s, l_s, acc_s = rest[2 * pp + 1:]
    g = pl.program_id(1)
    ng = pl.num_programs(1)
    ntok = pp * PAGE

    @pl.when(g == 0)
    def _():
        m_s[...] = jnp.full_like(m_s, NEG)
        l_s[...] = jnp.zeros_like(l_s)
        acc_s[...] = jnp.zeros_like(acc_s)

    for i in range(pp):
        kb_s[:, i * PAGE:(i + 1) * PAGE] = k_refs[i][...].astype(BF16)
        for h in range(H_A):
            vb_s[h, i * PAGE:(i + 1) * PAGE, :] = v_refs[i][pl.ds(h, PAGE, stride=H_A), :].astype(BF16)

    q = q_ref[...].astype(F32)
    sub = lax.broadcasted_iota(jnp.int32, (8, A_QK), 0)
    lane = lax.broadcasted_iota(jnp.int32, (8, A_QK), 1)
    qbd32 = jnp.where((lane >> 6) == sub, jnp.broadcast_to(q, (8, A_QK)), 0.0)
    qbd = qbd32.astype(BF16)

    j = lax.broadcasted_iota(jnp.int32, (8, 1), 0)
    slope = jnp.where(j < 2, ALIBI_SLOPES[0],
                      jnp.where(j < 4, ALIBI_SLOPES[1],
                                jnp.where(j < 6, ALIBI_SLOPES[2], ALIBI_SLOPES[3]))).astype(F32)
    kpos = g * ntok + lax.broadcasted_iota(jnp.int32, (1, ntok), 1)
    dist = (ng * ntok - kpos).astype(F32)

    s = _dot(qbd, kb_s[...]) - slope * dist
    m_old = m_s[:, 0:1]
    l_old = l_s[:, 0:1]
    m_new = jnp.maximum(m_old, jnp.max(s, axis=1, keepdims=True))
    alpha = jnp.exp(m_old - m_new)
    p = jnp.exp(s - m_new)
    l_new = alpha * l_old + jnp.sum(p, axis=1, keepdims=True)
    pb = p.astype(BF16)
    head_of_row = lax.broadcasted_iota(jnp.int32, (8, DV_A), 0) >> 1
    pv = jnp.zeros((8, DV_A), F32)
    for h in range(H_A):
        pv = jnp.where(head_of_row == h, _dot(pb, vb_s[h]), pv)
    acc = alpha * acc_s[...] + pv
    m_s[...] = jnp.broadcast_to(m_new, m_s.shape)
    l_s[...] = jnp.broadcast_to(l_new, l_s.shape)
    acc_s[...] = acc

    @pl.when(g == ng - 1)
    def _():
        kn = kn_ref[...].astype(F32)
        s_new = jnp.sum(qbd32 * kn, axis=1, keepdims=True)
        m_fin = jnp.maximum(m_new, s_new)
        a2 = jnp.exp(m_new - m_fin)
        p_new = jnp.exp(s_new - m_fin)
        l_fin = a2 * l_new + p_new
        vn = jnp.zeros((8, DV_A), F32)
        for h in range(H_A):
            vrow = vn_ref[:, h * DV_A:(h + 1) * DV_A].astype(F32)
            vn = jnp.where(head_of_row == h, jnp.broadcast_to(vrow, (8, DV_A)), vn)
        o = (a2 * acc + p_new * vn) / l_fin
        lam = lam_ref[0:1, :]
        for h in range(H_A):
            sl = slice(h * DV_A, (h + 1) * DV_A)
            out = _diff_norm_gate(o[2 * h:2 * h + 1, :], o[2 * h + 1:2 * h + 2, :], lam,
                                  gna_ref[:, sl], ga_ref[:, sl].astype(F32))
            out_ref[:, sl] = out.astype(BF16)


def _paged_attn(page_table, q3, kn3, vn3, ga3, gna, lam_tile, ck, cv):
    nreq, npages = page_table.shape
    pp = PAGES_PER_STEP
    ng = npages // pp

    def req(width):
        return pl.BlockSpec((None, 1, width), lambda r, g, pt: (r, 0, 0))

    def page(i):
        return pl.BlockSpec((None, 512, PAGE), lambda r, g, pt: (pt[r, g * pp + i], 0, 0))

    return pl.pallas_call(
        _paged_kernel,
        grid_spec=pltpu.PrefetchScalarGridSpec(
            num_scalar_prefetch=1,
            grid=(nreq, ng),
            in_specs=[req(512), req(512), req(512), req(512),
                      pl.BlockSpec((1, W_A), lambda r, g, pt: (0, 0)),
                      pl.BlockSpec((8, 128), lambda r, g, pt: (0, 0))]
                     + [page(i) for i in range(pp)] + [page(i) for i in range(pp)],
            out_specs=req(512),
            scratch_shapes=[
                pltpu.VMEM((A_QK, pp * PAGE), BF16),
                pltpu.VMEM((H_A, pp * PAGE, DV_A), BF16),
                pltpu.VMEM((8, 128), F32),
                pltpu.VMEM((8, 128), F32),
                pltpu.VMEM((8, DV_A), F32),
            ],
        ),
        out_shape=jax.ShapeDtypeStruct((nreq, 1, W_A), BF16),
        compiler_params=pltpu.CompilerParams(
            dimension_semantics=("arbitrary", "arbitrary"), vmem_limit_bytes=VMEM_LIMIT),
        name="paged_attn",
    )(page_table, q3, kn3, vn3, ga3, gna, lam_tile, *([ck] * pp), *([cv] * pp))


def _mstep_kernel(q_ref, k_ref, v_ref, o_ref, gb_ref, zif_ref, c_ref, n_ref, m_ref, gnb_ref,
                  y_ref, co_ref, no_ref, mo_ref):
    rb = STEP_RB
    x = zif_ref[...]
    sub = lax.broadcasted_iota(jnp.int32, (rb, 128), 0)
    lane = lax.broadcasted_iota(jnp.int32, (rb, 128), 1)
    m_out = jnp.zeros((rb, 128), F32)
    for h in range(H_B):
        sl = slice(h * 128, (h + 1) * 128)
        qb = q_ref[:, sl]
        qh = qb.astype(F32)
        kh = k_ref[:, sl].astype(F32)
        vh = v_ref[:, sl].astype(F32)
        i_c = x[:, h:h + 1]
        b = _log_sigmoid(x[:, H_B + h:H_B + h + 1])
        m_prev = m_ref[:, h:h + 1]
        inter = b + m_prev
        m_t = jnp.maximum(inter, (b - b) + i_c)
        w_intra = jnp.exp(((b - b) + i_c) - m_t)
        w_inter = jnp.exp(inter - m_t)
        sw = jnp.sum(qh * kh, axis=1, keepdims=True) * w_intra
        cq = jnp.zeros((rb, 128), F32)
        for r in range(rb):
            res = _nt_dot(qb, c_ref[r, h].astype(BF16))
            cq = jnp.where(sub == r, res, cq)
        nh = n_ref[:, sl]
        num = sw * vh + w_inter * cq
        den = sw + w_inter * jnp.sum(nh * qh, axis=1, keepdims=True)
        hh = num / jnp.maximum(jnp.abs(den), jnp.exp(-m_t))
        y = _mlstm_out_gate(hh, o_ref[:, sl].astype(F32), gnb_ref[:, sl], gb_ref[:, sl].astype(F32))
        y_ref[:, sl] = y.astype(BF16)

        w_s = jnp.exp(((b - b) + i_c) - m_t)
        decay = jnp.exp((b + m_prev) - m_t)
        wv = w_s * vh
        for r in range(rb):
            vcol = jnp.broadcast_to(wv[r:r + 1, :], (DV_B, DK_B)).T
            co_ref[r, h] = decay[r:r + 1, :] * c_ref[r, h] + vcol * kh[r:r + 1, :]
        no_ref[:, sl] = decay * nh + w_s * kh
        m_out = jnp.where(lane == h, m_t, m_out)
    mo_ref[...] = m_out


def _mlstm_step(mq, mk, mv, mo, gb, zif, c0, n0, m0, gnb):
    nreq = mq.shape[0]
    rb = STEP_RB

    def rows(width):
        return pl.BlockSpec((rb, width), lambda i: (i, 0))

    cspec = pl.BlockSpec((rb, H_B, DV_B, DK_B), lambda i: (i, 0, 0, 0))
    return pl.pallas_call(
        _mstep_kernel,
        grid=(nreq // rb,),
        in_specs=[rows(512), rows(512), rows(512), rows(512), rows(512), rows(GATE_PAD),
                  cspec, rows(512), rows(H_B), pl.BlockSpec((1, W_B), lambda i: (0, 0))],
        out_specs=[rows(512), cspec, rows(512), rows(128)],
        out_shape=[
            jax.ShapeDtypeStruct((nreq, W_B), BF16),
            jax.ShapeDtypeStruct((nreq, H_B, DV_B, DK_B), F32),
            jax.ShapeDtypeStruct((nreq, H_B * DK_B), F32),
            jax.ShapeDtypeStruct((nreq, 128), F32),
        ],
        compiler_params=pltpu.CompilerParams(
            dimension_semantics=("arbitrary",), vmem_limit_bytes=VMEM_LIMIT),
        name="mlstm_step",
    )(mq, mk, mv, mo, gb, zif, c0, n0, m0, gnb)


def _out_kernel(ma_ref, mb_ref, w_ref, x_ref, gate_ref, y_ref):
    acc = _dot(ma_ref[...], w_ref[0:W_A, :]) + _dot(mb_ref[...], w_ref[W_A:W_A + W_B, :])
    y_ref[...] = x_ref[...] + gate_ref[...] * acc


def _out_proj(mix_a, mix_b, w_out, x2d, mod, per_row, tm):
    m_rows = x2d.shape[0]
    nt = m_rows // tm
    if per_row:
        gate_spec = pl.BlockSpec((tm, D_MODEL), lambda i: (i, 2))
    else:
        tiles_per_batch = nt // mod.shape[0]
        gate_spec = pl.BlockSpec((None, 1, D_MODEL), lambda i: (i // tiles_per_batch, 0, 2))
    return pl.pallas_call(
        _out_kernel,
        grid=(nt,),
        in_specs=[
            pl.BlockSpec((tm, W_A), lambda i: (i, 0)),
            pl.BlockSpec((tm, W_B), lambda i: (i, 0)),
            pl.BlockSpec((W_A + W_B, D_MODEL), lambda i: (0, 0)),
            pl.BlockSpec((tm, D_MODEL), lambda i: (i, 0)),
            gate_spec,
        ],
        out_specs=pl.BlockSpec((tm, D_MODEL), lambda i: (i, 0)),
        out_shape=jax.ShapeDtypeStruct((m_rows, D_MODEL), F32),
        compiler_params=pltpu.CompilerParams(
            dimension_semantics=("arbitrary",), vmem_limit_bytes=VMEM_LIMIT),
        name="out_rows" if per_row else "out_bcast",
    )(mix_a, mix_b, w_out, x2d, mod)


def kernel(x_prompt, x_sample, c_prompt, c_sample, cache_k, cache_v, state_C, state_n, state_m,
           page_table, norm_g, w_ada, b_ada, w_in, b_in, g_q, g_k, lam_q, lam_k, gn_a, gn_b, w_out):
    assert w_in.shape[0] == 1, "single-layer model"
    nb, seq, _ = x_prompt.shape
    nreq = x_sample.shape[0]
    assert x_sample.shape[1] == 1

    w = w_in[0]
    bvec = b_in[0]
    gate_lo = N_MAIN
    gb_lo = N_MAIN + 2 * H_B
    wm = w[:, :N_MAIN].astype(BF16)
    wif = jnp.pad(w[:, gate_lo:gb_lo], ((0, 0), (0, GATE_PAD - 2 * H_B))).astype(BF16)
    wgb = w[:, gb_lo:].astype(BF16)
    bm = bvec[None, :N_MAIN]
    bif = jnp.pad(bvec[gate_lo:gb_lo], (0, GATE_PAD - 2 * H_B))[None, :]
    bgb = bvec[None, gb_lo:]
    gq = jnp.tile(g_q[0], 2 * H_A)[None, :]
    gk = jnp.tile(g_k[0], 2 * H_A)[None, :]
    grp = jnp.arange(A_QK) // DK_A
    seg = (grp[:, None] == grp[None, :]).astype(BF16)
    weights = (norm_g[0][None, :], wm, wif, wgb, bm, bif, bgb, gq, gk, seg)
    gna = gn_a[0].reshape(1, W_A)
    gnb = gn_b[0].reshape(1, W_B)
    wo = w_out[0].astype(BF16)
    slopes = jnp.asarray(ALIBI_SLOPES, F32)

    pad_rows = (-(nb + nreq)) % 8
    c_all = jnp.concatenate([c_prompt, c_sample, jnp.zeros((pad_rows, D_MODEL), F32)], axis=0)
    mod, lam_tile = _ada(c_all, w_ada[0], b_ada[0][None, :], lam_q[0], lam_k[0])
    mod_p = mod[:nb].reshape(nb, 1, 3 * D_MODEL)
    mod_s = mod[nb:nb + nreq]

    xp = x_prompt.reshape(nb * seq, D_MODEL)
    (q, kf, kb, vf, vb, ga, mq, mk, mv, mo, zif, gb) = _proj(xp, mod_p, False, PROJ_TM, weights)
    mix_a = _attn_prompt(slopes, q, kb, vb, ga, gna, lam_tile, nb, seq)
    mix_b, c_p, n_p, m_p = _mlstm_prompt(mq, mk, mv, mo, gb, zif, gnb, nb, seq)
    y_p = _out_proj(mix_a, mix_b, wo, xp, mod_p, False, OUT_TM)

    xs = x_sample.reshape(nreq, D_MODEL)
    (qs, kfs, kbs, vfs, vbs, gas, mqs, mks, mvs, mos, zifs, gbs) = _proj(xs, mod_s, True, nreq, weights)
    ck = jnp.transpose(cache_k[0], (0, 2, 3, 4, 1)).reshape(cache_k.shape[1], A_QK, PAGE)
    cv = cache_v[0].reshape(cache_v.shape[1], PAGE * H_A, DV_A)
    r3 = lambda a: a.reshape(nreq, 1, a.shape[-1])
    mix_as = _paged_attn(page_table, r3(qs), r3(kbs), r3(vbs), r3(gas), gna, lam_tile, ck, cv)
    mix_bs, c_s, n_s, m_s = _mlstm_step(mqs, mks, mvs, mos, gbs, zifs, state_C[0],
                                        state_n[0].reshape(nreq, H_B * DK_B), state_m[0], gnb)
    y_s = _out_proj(mix_as.reshape(nreq, W_A), mix_bs, wo, xs, mod_s, True, nreq)

    return (
        y_p.reshape(nb, seq, D_MODEL),
        y_s.reshape(nreq, 1, D_MODEL),
        jnp.transpose(kf.reshape(nb, H_A, 2, DK_A, seq), (0, 4, 1, 2, 3))[None],
        vf.reshape(1, nb, seq, H_A, DV_A),
        kfs.reshape(1, nreq, 1, H_A, 2, DK_A),
        vfs.reshape(1, nreq, 1, H_A, DV_A),
        c_p[None],
        n_p[None],
        m_p[:, :, 0][None],
        c_s[None],
        n_s.reshape(1, nreq, H_B, DK_B),
        m_s[:, :H_B][None],
    )
```

```python
import functools
import math

import jax
import jax.numpy as jnp
from jax import lax
from jax.experimental import pallas as pl
from jax.experimental.pallas import tpu as pltpu

F32 = jnp.float32
BF16 = jnp.bfloat16

D_MODEL = 1024
H_A = 4
DK_A = 64
DV_A = 128
A_QK = H_A * 2 * DK_A
W_A = H_A * DV_A
H_B = 4
DK_B = 128
DV_B = 128
W_B = H_B * DV_B
N_MAIN = 8 * 512
GATE_PAD = 128
PAGE = 128
CHUNK = 128
EPS = 1e-6
NEG = -1e30
LAM_INIT = 0.8 - 0.6 * math.exp(-0.3 * 0)
ALIBI_SLOPES = tuple(2.0 ** (-8.0 * (h + 1) / H_A) for h in range(H_A))
LOG2E = math.log2(math.e)
POS_RADIX = 64

VMEM_LIMIT = 56 * 1024 * 1024

PROJ_TM = 256
OUT_TM = 512
ATT_T = 512
PAGES_PER_STEP = 32
STEP_RB = 8


def _nt_dot(a, b):
    return lax.dot_general(a, b, (((1,), (1,)), ((), ())), preferred_element_type=F32)


def _dot(a, b):
    return jnp.dot(a, b, preferred_element_type=F32)


def _sigmoid(x):
    return 1.0 / (1.0 + jnp.exp(-x))


def _silu(x):
    return x * _sigmoid(x)


def _log_sigmoid(x):
    return jnp.minimum(x, 0.0) - jnp.log1p(jnp.exp(-jnp.abs(x)))


def _split3(a):
    a1 = a.astype(BF16)
    r1 = a - a1.astype(F32)
    a2 = r1.astype(BF16)
    a3 = (r1 - a2.astype(F32)).astype(BF16)
    return a1, a2, a3


def _ada_kernel(c_ref, w_ref, b_ref, lq_ref, lk_ref, mod_ref, lam_ref):
    a = _silu(c_ref[...])
    w = w_ref[...]
    a1, a2, _ = _split3(a)
    w1, w2, _ = _split3(w)
    mod_ref[...] = (_dot(a1, w1) + (_dot(a1, w2) + _dot(a2, w1))) + b_ref[...]

    @pl.when(pl.program_id(0) == 0)
    def _():
        s = jnp.sum(lq_ref[...] * lk_ref[...], axis=1, keepdims=True)
        e = jnp.exp(s)
        lam = e[0:1, :] - e[1:2, :] + LAM_INIT
        lam_ref[...] = jnp.broadcast_to(lam, lam_ref.shape)


def _ada(c_all, w_ada, b_ada, lam_q, lam_k):
    rows = c_all.shape[0]
    nblk = 3
    return pl.pallas_call(
        _ada_kernel,
        grid=(nblk,),
        in_specs=[
            pl.BlockSpec((rows, D_MODEL), lambda j: (0, 0)),
            pl.BlockSpec((D_MODEL, D_MODEL), lambda j: (0, j)),
            pl.BlockSpec((1, D_MODEL), lambda j: (0, j)),
            pl.BlockSpec((2, DK_A), lambda j: (0, 0)),
            pl.BlockSpec((2, DK_A), lambda j: (0, 0)),
        ],
        out_specs=[
            pl.BlockSpec((rows, D_MODEL), lambda j: (0, j)),
            pl.BlockSpec((8, 128), lambda j: (0, 0)),
        ],
        out_shape=[
            jax.ShapeDtypeStruct((rows, 3 * D_MODEL), F32),
            jax.ShapeDtypeStruct((8, 128), F32),
        ],
        compiler_params=pltpu.CompilerParams(
            dimension_semantics=("arbitrary",), vmem_limit_bytes=VMEM_LIMIT),
        name="ada",
    )(c_all, w_ada, b_ada, lam_q, lam_k)


def _proj_kernel(x_ref, shift_ref, scale_ref, ng_ref, wm_ref, wif_ref, wgb_ref,
                 bm_ref, bif_ref, bgb_ref, gq_ref, gk_ref, seg_ref,
                 q_ref, kf_ref, kb_ref, vf_ref, vb_ref, ga_ref,
                 mq_ref, mk_ref, mv_ref, mo_ref, zif_ref, gb_ref, *, k_transposed):
    x = x_ref[...]
    ms = jnp.mean(x * x, axis=-1, keepdims=True)
    h = x * lax.rsqrt(ms + EPS) * ng_ref[...]
    h = h * (1.0 + scale_ref[...]) + shift_ref[...]
    hb = h.astype(BF16)

    def col(j):
        sl = slice(j * 512, (j + 1) * 512)
        return _dot(hb, wm_ref[:, sl]) + bm_ref[:, sl]

    def headnorm(z, g):
        ss = _dot((z * z).astype(BF16), seg_ref[...])
        return z * lax.rsqrt(ss * (1.0 / DK_A) + EPS) * g

    q_ref[...] = (headnorm(col(0), gq_ref[...]) * (DK_A ** -0.5 * LOG2E)).astype(BF16)
    k = headnorm(col(1), gk_ref[...])
    kf_ref[...] = k.T if k_transposed else k
    kb_ref[...] = k.astype(BF16)
    v = col(2)
    for hh in range(H_A):
        vf_ref[pl.ds(hh, x.shape[0], stride=H_A), :] = v[:, hh * DV_A:(hh + 1) * DV_A]
    vb_ref[...] = v.astype(BF16)
    ga_ref[...] = col(3).astype(BF16)
    mq_ref[...] = col(4).astype(BF16)
    mk_ref[...] = (col(5) * (DK_B ** -0.5)).astype(BF16)
    mv_ref[...] = col(6).astype(BF16)
    mo_ref[...] = _sigmoid(col(7)).astype(BF16)
    zif_ref[...] = _dot(hb, wif_ref[...]) + bif_ref[...]
    gb_ref[...] = (_dot(hb, wgb_ref[...]) + bgb_ref[...]).astype(BF16)


def _proj(x2d, mod, per_row, tm, weights):
    (ng, wm, wif, wgb, bm, bif, bgb, gq, gk, seg) = weights
    m_rows = x2d.shape[0]
    nt = m_rows // tm
    if per_row:
        shift_spec = pl.BlockSpec((tm, D_MODEL), lambda i: (i, 0))
        scale_spec = pl.BlockSpec((tm, D_MODEL), lambda i: (i, 1))
    else:
        tiles_per_batch = nt // mod.shape[0]
        shift_spec = pl.BlockSpec((None, 1, D_MODEL), lambda i: (i // tiles_per_batch, 0, 0))
        scale_spec = pl.BlockSpec((None, 1, D_MODEL), lambda i: (i // tiles_per_batch, 0, 1))

    def const(shape):
        return pl.BlockSpec(shape, lambda i: (0, 0))

    def rows(width):
        return pl.BlockSpec((tm, width), lambda i: (i, 0))

    out_dtypes = [BF16, F32, BF16, F32, BF16, BF16, BF16, BF16, BF16, BF16, F32, BF16]
    out_widths = [512, 512, 512, DV_A, 512, 512, 512, 512, 512, 512, GATE_PAD, 512]
    out_rows = [1, 1, 1, H_A, 1, 1, 1, 1, 1, 1, 1, 1]
    out_specs = [pl.BlockSpec((tm * r, w), lambda i: (i, 0)) for w, r in zip(out_widths, out_rows)]
    out_shape = [jax.ShapeDtypeStruct((m_rows * r, w), d)
                 for w, r, d in zip(out_widths, out_rows, out_dtypes)]
    if not per_row:
        out_specs[1] = pl.BlockSpec((None, A_QK, tm),
                                    lambda i: (i // tiles_per_batch, 0, i % tiles_per_batch))
        out_shape[1] = jax.ShapeDtypeStruct((mod.shape[0], A_QK, m_rows // mod.shape[0]), F32)
    return pl.pallas_call(
        functools.partial(_proj_kernel, k_transposed=not per_row),
        grid=(nt,),
        in_specs=[
            rows(D_MODEL), shift_spec, scale_spec, const((1, D_MODEL)),
            const((D_MODEL, N_MAIN)), const((D_MODEL, GATE_PAD)), const((D_MODEL, W_B)),
            const((1, N_MAIN)), const((1, GATE_PAD)), const((1, W_B)),
            const((1, A_QK)), const((1, A_QK)), const((A_QK, A_QK)),
        ],
        out_specs=out_specs,
        out_shape=out_shape,
        compiler_params=pltpu.CompilerParams(
            dimension_semantics=("arbitrary",), vmem_limit_bytes=VMEM_LIMIT),
        name="proj_rows" if per_row else "proj_bcast",
    )(x2d, mod, mod, ng, wm, wif, wgb, bm, bif, bgb, gq, gk, seg)


def _diff_norm_gate(o0, o1, lam, gna, ga):
    d = o0 - lam * o1
    ya = d * lax.rsqrt(jnp.mean(d * d, axis=-1, keepdims=True) + EPS) * gna
    ya = ya * (1.0 - LAM_INIT)
    return ya * _silu(ga)


def _mlstm_out_gate(hh, o, gnb, gb):
    hg = o * hh
    yb = hg * lax.rsqrt(jnp.mean(hg * hg, axis=-1, keepdims=True) + EPS) * gnb
    return yb * _silu(gb)


def _alibi_features(seq):
    pos = jnp.arange(seq, dtype=jnp.int32)
    digits = [(pos // POS_RADIX).astype(F32), (pos % POS_RADIX).astype(F32)]
    ones = jnp.ones((seq,), F32)
    qf, kf = [], []
    for slope in ALIBI_SLOPES:
        pieces = [p.astype(F32) for p in _split3(jnp.float32(slope * LOG2E))]
        qcols, kcols = [], []
        for c in pieces:
            qcols += [POS_RADIX * c * ones, c * ones]
            kcols += digits
        for c in pieces:
            qcols += digits
            kcols += [-POS_RADIX * c * ones, -c * ones]
        pad = ((0, 0), (0, 128 - len(qcols)))
        qf.append(jnp.pad(jnp.stack(qcols, axis=1), pad))
        kf.append(jnp.pad(jnp.stack(kcols, axis=1), pad))
    return jnp.stack(qf).astype(BF16), jnp.stack(kf).astype(BF16)


def _attn_kernel(q_ref, qf_ref, k_ref, kf_ref, v_ref, ga_ref, gna_ref, lam_ref, out_ref, lhs_s):
    t = ATT_T
    qi = pl.program_id(2)

    q = q_ref[...].astype(F32)
    lane = lax.broadcasted_iota(jnp.int32, (t, 2 * DK_A), 1)
    lhs_s[0:t, 0:128] = jnp.where(lane < DK_A, q, 0.0).astype(BF16)
    lhs_s[t:2 * t, 0:128] = jnp.where(lane >= DK_A, q, 0.0).astype(BF16)
    lhs_s[0:t, 128:256] = qf_ref[...]
    lhs_s[t:2 * t, 128:256] = qf_ref[...]

    krow = lax.broadcasted_iota(jnp.int32, (t, 2 * t), 0)
    qcol = lax.broadcasted_iota(jnp.int32, (t, 2 * t), 1)
    future = krow > jnp.where(qcol >= t, qcol - t, qcol)

    def block(kj, carry, masked):
        m, l, acc = carry
        off = pl.multiple_of(kj * t, t)
        kx = jnp.concatenate([k_ref[pl.ds(off, t), :], kf_ref[pl.ds(off, t), :]], axis=1)
        s = _nt_dot(kx, lhs_s[...])
        if masked:
            s = jnp.where(future, NEG, s)
        m_new = jnp.maximum(m, jnp.max(s, axis=0, keepdims=True))
        alpha = jnp.exp2(m - m_new)
        p = jnp.exp2(s - m_new)
        l = alpha * l + jnp.sum(p, axis=0, keepdims=True)
        pv = lax.dot_general(v_ref[pl.ds(off, t), :], p.astype(BF16), (((0,), (0,)), ((), ())),
                             preferred_element_type=F32)
        return m_new, l, alpha * acc + pv

    init = (jnp.full((1, 2 * t), NEG, F32), jnp.zeros((1, 2 * t), F32), jnp.zeros((DV_A, 2 * t), F32))
    carry = lax.fori_loop(0, qi, lambda kj, cr: block(kj, cr, False), init)
    m, l, acc = block(qi, carry, True)
    o = (acc / l).T
    lam = lam_ref[0:1, :]
    out = _diff_norm_gate(o[:t], o[t:], lam, gna_ref[...], ga_ref[...].astype(F32))
    out_ref[...] = out.astype(BF16)


def _attn_prompt(q, k, v, ga, gna, lam_tile, nb, seq):
    nq = seq // ATT_T
    qf, kf = _alibi_features(seq)
    return pl.pallas_call(
        _attn_kernel,
        grid=(nb, H_A, nq),
        in_specs=[
            pl.BlockSpec((ATT_T, 128), lambda b, h, i: (b * nq + i, h)),
            pl.BlockSpec((None, ATT_T, 128), lambda b, h, i: (h, i, 0)),
            pl.BlockSpec((seq, 128), lambda b, h, i: (b, h)),
            pl.BlockSpec((None, seq, 128), lambda b, h, i: (h, 0, 0)),
            pl.BlockSpec((seq, 128), lambda b, h, i: (b, h)),
            pl.BlockSpec((ATT_T, 128), lambda b, h, i: (b * nq + i, h)),
            pl.BlockSpec((1, 128), lambda b, h, i: (0, h)),
            pl.BlockSpec((8, 128), lambda b, h, i: (0, 0)),
        ],
        out_specs=pl.BlockSpec((ATT_T, 128), lambda b, h, i: (b * nq + i, h)),
        out_shape=jax.ShapeDtypeStruct((nb * seq, W_A), BF16),
        scratch_shapes=[pltpu.VMEM((2 * ATT_T, 256), BF16)],
        compiler_params=pltpu.CompilerParams(
            dimension_semantics=("arbitrary", "arbitrary", "arbitrary"),
            vmem_limit_bytes=VMEM_LIMIT),
        name="attn_prompt",
    )(q, qf, k, kf, v, ga, gna, lam_tile)


def _mlstm_kernel(q_ref, k_ref, v_ref, o_ref, gb_ref, zif_ref, gnb_ref,
                  y_ref, c_ref, n_ref, m_ref):
    L = CHUNK

    @pl.when(pl.program_id(1) == 0)
    def _():
        c_ref[...] = jnp.zeros_like(c_ref)
        n_ref[...] = jnp.zeros_like(n_ref)
        m_ref[...] = jnp.zeros_like(m_ref)

    x = zif_ref[...]
    logf = _log_sigmoid(x)
    row = lax.broadcasted_iota(jnp.int32, (L, L), 0)
    colv = lax.broadcasted_iota(jnp.int32, (L, L), 1)
    causal = row >= colv
    tri = jnp.where(causal, 1.0, 0.0).astype(BF16)
    f1, f2, f3 = _split3(logf)
    bmat = _dot(tri, f1) + (_dot(tri, f2) + _dot(tri, f3))
    xt = x.T
    bt = bmat.T

    for h in range(H_B):
        sl = slice(h * 128, (h + 1) * 128)
        qh = q_ref[:, sl]
        kh = k_ref[:, sl]
        vh = v_ref[:, sl]
        icol = x[:, h:h + 1]
        bcol = bmat[:, H_B + h:H_B + h + 1]
        irow = xt[h:h + 1, :]
        brow = bt[H_B + h:H_B + h + 1, :]
        m_prev = m_ref[h:h + 1, 0:1]
        nrow = n_ref[h:h + 1, :]
        c_old = c_ref[h]

        d = jnp.where(causal, (bcol - brow) + irow, NEG)
        inter = bcol + m_prev
        m_t = jnp.maximum(inter, jnp.max(d, axis=1, keepdims=True))
        w_intra = jnp.exp(d - m_t)
        w_inter = jnp.exp(inter - m_t)
        sw = _nt_dot(qh, kh) * w_intra
        num = _dot(sw.astype(BF16), vh) + w_inter * _nt_dot(qh, c_old.astype(BF16))
        den = jnp.sum(sw, axis=1, keepdims=True) \
            + w_inter * jnp.sum(qh.astype(F32) * nrow, axis=1, keepdims=True)
        hh = num / jnp.maximum(jnp.abs(den), jnp.exp(-m_t))
        y = _mlstm_out_gate(hh, o_ref[:, sl].astype(F32), gnb_ref[:, sl], gb_ref[:, sl].astype(F32))
        y_ref[:, sl] = y.astype(BF16)

        m_new = m_t[L - 1:L, :]
        b_last = bcol[L - 1:L, :]
        w_s = jnp.exp(((b_last - bcol) + icol) - m_new)
        decay = jnp.exp((b_last + m_prev) - m_new)
        wv = w_s * vh.astype(F32)
        c_ref[h] = decay * c_old + _dot(wv.T.astype(BF16), kh)
        n_ref[h:h + 1, :] = decay * nrow + jnp.sum(w_s * kh.astype(F32), axis=0, keepdims=True)
        m_ref[h:h + 1, :] = jnp.broadcast_to(m_new, (1, 128))


def _mlstm_prompt(mq, mk, mv, mo, gb, zif, gnb, nb, seq):
    nc = seq // CHUNK

    def rows(width):
        return pl.BlockSpec((CHUNK, width), lambda b, c: (b * nc + c, 0))

    return pl.pallas_call(
        _mlstm_kernel,
        grid=(nb, nc),
        in_specs=[rows(512), rows(512), rows(512), rows(512), rows(512), rows(GATE_PAD),
                  pl.BlockSpec((1, W_B), lambda b, c: (0, 0))],
        out_specs=[
            rows(512),
            pl.BlockSpec((None, H_B, DV_B, DK_B), lambda b, c: (b, 0, 0, 0)),
            pl.BlockSpec((None, H_B, DK_B), lambda b, c: (b, 0, 0)),
            pl.BlockSpec((None, H_B, 128), lambda b, c: (b, 0, 0)),
        ],
        out_shape=[
            jax.ShapeDtypeStruct((nb * seq, W_B), BF16),
            jax.ShapeDtypeStruct((nb, H_B, DV_B, DK_B), F32),
            jax.ShapeDtypeStruct((nb, H_B, DK_B), F32),
            jax.ShapeDtypeStruct((nb, H_B, 128), F32),
        ],
        compiler_params=pltpu.CompilerParams(
            dimension_semantics=("arbitrary", "arbitrary"), vmem_limit_bytes=VMEM_LIMIT),
        name="mlstm_prompt",
    )(mq, mk, mv, mo, gb, zif, gnb)


def _paged_kernel(pt_ref, q_ref, kn_ref, vn_ref, ga_ref, gna_ref, lam_ref, *rest):
    pp = PAGES_PER_STEP
    k_refs = rest[:pp]
    v_refs = rest[pp:2 * pp]
    out_ref = rest[2 * pp]
    kb_s, vb_s, m_s, l_s, acc_s = rest[2 * pp + 1:]
    g = pl.program_id(1)
    ng = pl.num_programs(1)
    ntok = pp * PAGE

    @pl.when(g == 0)
    def _():
        m_s[...] = jnp.full_like(m_s, NEG)
        l_s[...] = jnp.zeros_like(l_s)
        acc_s[...] = jnp.zeros_like(acc_s)

    for i in range(pp):
        kb_s[:, i * PAGE:(i + 1) * PAGE] = k_refs[i][...].astype(BF16)
        for h in range(H_A):
            vb_s[h, i * PAGE:(i + 1) * PAGE, :] = v_refs[i][pl.ds(h, PAGE, stride=H_A), :].astype(BF16)

    q = q_ref[...].astype(F32)
    sub = lax.broadcasted_iota(jnp.int32, (8, A_QK), 0)
    lane = lax.broadcasted_iota(jnp.int32, (8, A_QK), 1)
    qbd32 = jnp.where((lane >> 6) == sub, jnp.broadcast_to(q, (8, A_QK)), 0.0)
    qbd = qbd32.astype(BF16)

    j = lax.broadcasted_iota(jnp.int32, (8, 1), 0)
    slope = jnp.where(j < 2, ALIBI_SLOPES[0] * LOG2E,
                      jnp.where(j < 4, ALIBI_SLOPES[1] * LOG2E,
                                jnp.where(j < 6, ALIBI_SLOPES[2] * LOG2E,
                                          ALIBI_SLOPES[3] * LOG2E))).astype(F32)
    kpos = g * ntok + lax.broadcasted_iota(jnp.int32, (1, ntok), 1)
    dist = (ng * ntok - kpos).astype(F32)

    s = _dot(qbd, kb_s[...]) - slope * dist
    m_old = m_s[:, 0:1]
    l_old = l_s[:, 0:1]
    m_new = jnp.maximum(m_old, jnp.max(s, axis=1, keepdims=True))
    alpha = jnp.exp2(m_old - m_new)
    p = jnp.exp2(s - m_new)
    l_new = alpha * l_old + jnp.sum(p, axis=1, keepdims=True)
    pb = p.astype(BF16)
    head_of_row = lax.broadcasted_iota(jnp.int32, (8, DV_A), 0) >> 1
    pv = jnp.zeros((8, DV_A), F32)
    for h in range(H_A):
        pv = jnp.where(head_of_row == h, _dot(pb, vb_s[h]), pv)
    acc = alpha * acc_s[...] + pv
    m_s[...] = jnp.broadcast_to(m_new, m_s.shape)
    l_s[...] = jnp.broadcast_to(l_new, l_s.shape)
    acc_s[...] = acc

    @pl.when(g == ng - 1)
    def _():
        kn = kn_ref[...].astype(F32)
        s_new = jnp.sum(qbd32 * kn, axis=1, keepdims=True)
        m_fin = jnp.maximum(m_new, s_new)
        a2 = jnp.exp2(m_new - m_fin)
        p_new = jnp.exp2(s_new - m_fin)
        l_fin = a2 * l_new + p_new
        vn = jnp.zeros((8, DV_A), F32)
        for h in range(H_A):
            vrow = vn_ref[:, h * DV_A:(h + 1) * DV_A].astype(F32)
            vn = jnp.where(head_of_row == h, jnp.broadcast_to(vrow, (8, DV_A)), vn)
        o = (a2 * acc + p_new * vn) / l_fin
        lam = lam_ref[0:1, :]
        for h in range(H_A):
            sl = slice(h * DV_A, (h + 1) * DV_A)
            out = _diff_norm_gate(o[2 * h:2 * h + 1, :], o[2 * h + 1:2 * h + 2, :], lam,
                                  gna_ref[:, sl], ga_ref[:, sl].astype(F32))
            out_ref[:, sl] = out.astype(BF16)


def _paged_attn(page_table, q3, kn3, vn3, ga3, gna, lam_tile, ck, cv):
    nreq, npages = page_table.shape
    pp = PAGES_PER_STEP
    ng = npages // pp

    def req(width):
        return pl.BlockSpec((None, 1, width), lambda r, g, pt: (r, 0, 0))

    def page(i):
        return pl.BlockSpec((None, 512, PAGE), lambda r, g, pt: (pt[r, g * pp + i], 0, 0))

    return pl.pallas_call(
        _paged_kernel,
        grid_spec=pltpu.PrefetchScalarGridSpec(
            num_scalar_prefetch=1,
            grid=(nreq, ng),
            in_specs=[req(512), req(512), req(512), req(512),
                      pl.BlockSpec((1, W_A), lambda r, g, pt: (0, 0)),
                      pl.BlockSpec((8, 128), lambda r, g, pt: (0, 0))]
                     + [page(i) for i in range(pp)] + [page(i) for i in range(pp)],
            out_specs=req(512),
            scratch_shapes=[
                pltpu.VMEM((A_QK, pp * PAGE), BF16),
                pltpu.VMEM((H_A, pp * PAGE, DV_A), BF16),
                pltpu.VMEM((8, 128), F32),
                pltpu.VMEM((8, 128), F32),
                pltpu.VMEM((8, DV_A), F32),
            ],
        ),
        out_shape=jax.ShapeDtypeStruct((nreq, 1, W_A), BF16),
        compiler_params=pltpu.CompilerParams(
            dimension_semantics=("arbitrary", "arbitrary"), vmem_limit_bytes=VMEM_LIMIT),
        name="paged_attn",
    )(page_table, q3, kn3, vn3, ga3, gna, lam_tile, *([ck] * pp), *([cv] * pp))


def _mstep_kernel(q_ref, k_ref, v_ref, o_ref, gb_ref, zif_ref, c_ref, n_ref, m_ref, gnb_ref,
                  y_ref, co_ref, no_ref, mo_ref):
    rb = STEP_RB
    x = zif_ref[...]
    sub = lax.broadcasted_iota(jnp.int32, (rb, 128), 0)
    lane = lax.broadcasted_iota(jnp.int32, (rb, 128), 1)
    m_out = jnp.zeros((rb, 128), F32)
    for h in range(H_B):
        sl = slice(h * 128, (h + 1) * 128)
        qb = q_ref[:, sl]
        qh = qb.astype(F32)
        kh = k_ref[:, sl].astype(F32)
        vh = v_ref[:, sl].astype(F32)
        i_c = x[:, h:h + 1]
        b = _log_sigmoid(x[:, H_B + h:H_B + h + 1])
        m_prev = m_ref[:, h:h + 1]
        inter = b + m_prev
        m_t = jnp.maximum(inter, (b - b) + i_c)
        w_intra = jnp.exp(((b - b) + i_c) - m_t)
        w_inter = jnp.exp(inter - m_t)
        sw = jnp.sum(qh * kh, axis=1, keepdims=True) * w_intra
        cq = jnp.zeros((rb, 128), F32)
        for r in range(rb):
            res = _nt_dot(qb, c_ref[r, h].astype(BF16))
            cq = jnp.where(sub == r, res, cq)
        nh = n_ref[:, sl]
        num = sw * vh + w_inter * cq
        den = sw + w_inter * jnp.sum(nh * qh, axis=1, keepdims=True)
        hh = num / jnp.maximum(jnp.abs(den), jnp.exp(-m_t))
        y = _mlstm_out_gate(hh, o_ref[:, sl].astype(F32), gnb_ref[:, sl], gb_ref[:, sl].astype(F32))
        y_ref[:, sl] = y.astype(BF16)

        w_s = jnp.exp(((b - b) + i_c) - m_t)
        decay = jnp.exp((b + m_prev) - m_t)
        wv = w_s * vh
        for r in range(rb):
            vcol = jnp.broadcast_to(wv[r:r + 1, :], (DV_B, DK_B)).T
            co_ref[r, h] = decay[r:r + 1, :] * c_ref[r, h] + vcol * kh[r:r + 1, :]
        no_ref[:, sl] = decay * nh + w_s * kh
        m_out = jnp.where(lane == h, m_t, m_out)
    mo_ref[...] = m_out


def _mlstm_step(mq, mk, mv, mo, gb, zif, c0, n0, m0, gnb):
    nreq = mq.shape[0]
    rb = STEP_RB

    def rows(width):
        return pl.BlockSpec((rb, width), lambda i: (i, 0))

    cspec = pl.BlockSpec((rb, H_B, DV_B, DK_B), lambda i: (i, 0, 0, 0))
    return pl.pallas_call(
        _mstep_kernel,
        grid=(nreq // rb,),
        in_specs=[rows(512), rows(512), rows(512), rows(512), rows(512), rows(GATE_PAD),
                  cspec, rows(512), rows(H_B), pl.BlockSpec((1, W_B), lambda i: (0, 0))],
        out_specs=[rows(512), cspec, rows(512), rows(128)],
        out_shape=[
            jax.ShapeDtypeStruct((nreq, W_B), BF16),
            jax.ShapeDtypeStruct((nreq, H_B, DV_B, DK_B), F32),
            jax.ShapeDtypeStruct((nreq, H_B * DK_B), F32),
            jax.ShapeDtypeStruct((nreq, 128), F32),
        ],
        compiler_params=pltpu.CompilerParams(
            dimension_semantics=("arbitrary",), vmem_limit_bytes=VMEM_LIMIT),
        name="mlstm_step",
    )(mq, mk, mv, mo, gb, zif, c0, n0, m0, gnb)


def _out_kernel(ma_ref, mb_ref, w_ref, x_ref, gate_ref, y_ref):
    acc = _dot(ma_ref[...], w_ref[0:W_A, :]) + _dot(mb_ref[...], w_ref[W_A:W_A + W_B, :])
    y_ref[...] = x_ref[...] + gate_ref[...] * acc


def _out_proj(mix_a, mix_b, w_out, x2d, mod, per_row, tm):
    m_rows = x2d.shape[0]
    nt = m_rows // tm
    if per_row:
        gate_spec = pl.BlockSpec((tm, D_MODEL), lambda i: (i, 2))
    else:
        tiles_per_batch = nt // mod.shape[0]
        gate_spec = pl.BlockSpec((None, 1, D_MODEL), lambda i: (i // tiles_per_batch, 0, 2))
    return pl.pallas_call(
        _out_kernel,
        grid=(nt,),
        in_specs=[
            pl.BlockSpec((tm, W_A), lambda i: (i, 0)),
            pl.BlockSpec((tm, W_B), lambda i: (i, 0)),
            pl.BlockSpec((W_A + W_B, D_MODEL), lambda i: (0, 0)),
            pl.BlockSpec((tm, D_MODEL), lambda i: (i, 0)),
            gate_spec,
        ],
        out_specs=pl.BlockSpec((tm, D_MODEL), lambda i: (i, 0)),
        out_shape=jax.ShapeDtypeStruct((m_rows, D_MODEL), F32),
        compiler_params=pltpu.CompilerParams(
            dimension_semantics=("arbitrary",), vmem_limit_bytes=VMEM_LIMIT),
        name="out_rows" if per_row else "out_bcast",
    )(mix_a, mix_b, w_out, x2d, mod)


def kernel(x_prompt, x_sample, c_prompt, c_sample, cache_k, cache_v, state_C, state_n, state_m,
           page_table, norm_g, w_ada, b_ada, w_in, b_in, g_q, g_k, lam_q, lam_k, gn_a, gn_b, w_out):
    assert w_in.shape[0] == 1, "single-layer model"
    nb, seq, _ = x_prompt.shape
    nreq = x_sample.shape[0]
    assert x_sample.shape[1] == 1

    w = w_in[0]
    bvec = b_in[0]
    gate_lo = N_MAIN
    gb_lo = N_MAIN + 2 * H_B
    wm = w[:, :N_MAIN].astype(BF16)
    wif = jnp.pad(w[:, gate_lo:gb_lo], ((0, 0), (0, GATE_PAD - 2 * H_B))).astype(BF16)
    wgb = w[:, gb_lo:].astype(BF16)
    bm = bvec[None, :N_MAIN]
    bif = jnp.pad(bvec[gate_lo:gb_lo], (0, GATE_PAD - 2 * H_B))[None, :]
    bgb = bvec[None, gb_lo:]
    gq = jnp.tile(g_q[0], 2 * H_A)[None, :]
    gk = jnp.tile(g_k[0], 2 * H_A)[None, :]
    grp = jnp.arange(A_QK) // DK_A
    seg = (grp[:, None] == grp[None, :]).astype(BF16)
    weights = (norm_g[0][None, :], wm, wif, wgb, bm, bif, bgb, gq, gk, seg)
    gna = gn_a[0].reshape(1, W_A)
    gnb = gn_b[0].reshape(1, W_B)
    wo = w_out[0].astype(BF16)

    pad_rows = (-(nb + nreq)) % 8
    c_all = jnp.concatenate([c_prompt, c_sample, jnp.zeros((pad_rows, D_MODEL), F32)], axis=0)
    mod, lam_tile = _ada(c_all, w_ada[0], b_ada[0][None, :], lam_q[0], lam_k[0])
    mod_p = mod[:nb].reshape(nb, 1, 3 * D_MODEL)
    mod_s = mod[nb:nb + nreq]

    xp = x_prompt.reshape(nb * seq, D_MODEL)
    (q, kf, kb, vf, vb, ga, mq, mk, mv, mo, zif, gb) = _proj(xp, mod_p, False, PROJ_TM, weights)
    mix_a = _attn_prompt(q, kb, vb, ga, gna, lam_tile, nb, seq)
    mix_b, c_p, n_p, m_p = _mlstm_prompt(mq, mk, mv, mo, gb, zif, gnb, nb, seq)
    y_p = _out_proj(mix_a, mix_b, wo, xp, mod_p, False, OUT_TM)

    xs = x_sample.reshape(nreq, D_MODEL)
    (qs, kfs, kbs, vfs, vbs, gas, mqs, mks, mvs, mos, zifs, gbs) = _proj(xs, mod_s, True, nreq, weights)
    ck = jnp.transpose(cache_k[0], (0, 2, 3, 4, 1)).reshape(cache_k.shape[1], A_QK, PAGE)
    cv = cache_v[0].reshape(cache_v.shape[1], PAGE * H_A, DV_A)
    r3 = lambda a: a.reshape(nreq, 1, a.shape[-1])
    mix_as = _paged_attn(page_table, r3(qs), r3(kbs), r3(vbs), r3(gas), gna, lam_tile, ck, cv)
    mix_bs, c_s, n_s, m_s = _mlstm_step(mqs, mks, mvs, mos, gbs, zifs, state_C[0],
                                        state_n[0].reshape(nreq, H_B * DK_B), state_m[0], gnb)
    y_s = _out_proj(mix_as.reshape(nreq, W_A), mix_bs, wo, xs, mod_s, True, nreq)

    return (
        y_p.reshape(nb, seq, D_MODEL),
        y_s.reshape(nreq, 1, D_MODEL),
        jnp.transpose(kf.reshape(nb, H_A, 2, DK_A, seq), (0, 4, 1, 2, 3))[None],
        vf.reshape(1, nb, seq, H_A, DV_A),
        kfs.reshape(1, nreq, 1, H_A, 2, DK_A),
        vfs.reshape(1, nreq, 1, H_A, DV_A),
        c_p[None],
        n_p[None],
        m_p[:, :, 0][None],
        c_s[None],
        n_s.reshape(1, nreq, H_B, DK_B),
        m_s[:, :H_B][None],
    )
```

```python
import functools
import math

import jax
import jax.numpy as jnp
from jax import lax
from jax.experimental import pallas as pl
from jax.experimental.pallas import tpu as pltpu

F32 = jnp.float32
BF16 = jnp.bfloat16

D_MODEL = 1024
H_A = 4
DK_A = 64
DV_A = 128
A_QK = H_A * 2 * DK_A
W_A = H_A * DV_A
H_B = 4
DK_B = 128
DV_B = 128
W_B = H_B * DV_B
N_MAIN = 8 * 512
GATE_PAD = 128
PAGE = 128
CHUNK = 128
EPS = 1e-6
NEG = -1e30
LAM_INIT = 0.8 - 0.6 * math.exp(-0.3 * 0)
ALIBI_SLOPES = tuple(2.0 ** (-8.0 * (h + 1) / H_A) for h in range(H_A))
LOG2E = math.log2(math.e)
POS_RADIX = 64

VMEM_LIMIT = 56 * 1024 * 1024

PROJ_TM = 256
OUT_TM = 512
ATT_T = 512
PAGES_PER_STEP = 32
STEP_RB = 8


def _nt_dot(a, b):
    return lax.dot_general(a, b, (((1,), (1,)), ((), ())), preferred_element_type=F32)


def _dot(a, b):
    return jnp.dot(a, b, preferred_element_type=F32)


def _sigmoid(x):
    return 1.0 / (1.0 + jnp.exp(-x))


def _silu(x):
    return x * _sigmoid(x)


def _log_sigmoid(x):
    return jnp.minimum(x, 0.0) - jnp.log1p(jnp.exp(-jnp.abs(x)))


def _split3(a):
    a1 = a.astype(BF16)
    r1 = a - a1.astype(F32)
    a2 = r1.astype(BF16)
    a3 = (r1 - a2.astype(F32)).astype(BF16)
    return a1, a2, a3


def _ada_kernel(c_ref, w_ref, b_ref, lq_ref, lk_ref, mod_ref, lam_ref):
    a = _silu(c_ref[...])
    w = w_ref[...]
    a1, a2, _ = _split3(a)
    w1, w2, _ = _split3(w)
    mod_ref[...] = (_dot(a1, w1) + (_dot(a1, w2) + _dot(a2, w1))) + b_ref[...]

    @pl.when(pl.program_id(0) == 0)
    def _():
        s = jnp.sum(lq_ref[...] * lk_ref[...], axis=1, keepdims=True)
        e = jnp.exp(s)
        lam = e[0:1, :] - e[1:2, :] + LAM_INIT
        lam_ref[...] = jnp.broadcast_to(lam, lam_ref.shape)


def _ada(c_all, w_ada, b_ada, lam_q, lam_k):
    rows = c_all.shape[0]
    nblk = 3
    return pl.pallas_call(
        _ada_kernel,
        grid=(nblk,),
        in_specs=[
            pl.BlockSpec((rows, D_MODEL), lambda j: (0, 0)),
            pl.BlockSpec((D_MODEL, D_MODEL), lambda j: (0, j)),
            pl.BlockSpec((1, D_MODEL), lambda j: (0, j)),
            pl.BlockSpec((2, DK_A), lambda j: (0, 0)),
            pl.BlockSpec((2, DK_A), lambda j: (0, 0)),
        ],
        out_specs=[
            pl.BlockSpec((rows, D_MODEL), lambda j: (0, j)),
            pl.BlockSpec((8, 128), lambda j: (0, 0)),
        ],
        out_shape=[
            jax.ShapeDtypeStruct((rows, 3 * D_MODEL), F32),
            jax.ShapeDtypeStruct((8, 128), F32),
        ],
        compiler_params=pltpu.CompilerParams(
            dimension_semantics=("arbitrary",), vmem_limit_bytes=VMEM_LIMIT),
        name="ada",
    )(c_all, w_ada, b_ada, lam_q, lam_k)


def _proj_kernel(x_ref, shift_ref, scale_ref, ng_ref, wm_ref, wif_ref, wgb_ref,
                 bm_ref, bif_ref, bgb_ref, gq_ref, gk_ref, seg_ref,
                 q_ref, kf_ref, kb_ref, vf_ref, vb_ref, ga_ref,
                 mq_ref, mk_ref, mv_ref, mo_ref, zif_ref, gb_ref, *, k_transposed):
    x = x_ref[...]
    ms = jnp.mean(x * x, axis=-1, keepdims=True)
    h = x * lax.rsqrt(ms + EPS) * ng_ref[...]
    h = h * (1.0 + scale_ref[...]) + shift_ref[...]
    hb = h.astype(BF16)

    def col(j):
        sl = slice(j * 512, (j + 1) * 512)
        return _dot(hb, wm_ref[:, sl]) + bm_ref[:, sl]

    def headnorm(z, g):
        ss = _dot((z * z).astype(BF16), seg_ref[...])
        return z * lax.rsqrt(ss * (1.0 / DK_A) + EPS) * g

    q_ref[...] = (headnorm(col(0), gq_ref[...]) * (DK_A ** -0.5 * LOG2E)).astype(BF16)
    k = headnorm(col(1), gk_ref[...])
    kf_ref[...] = k.T if k_transposed else k
    kb_ref[...] = k.astype(BF16)
    v = col(2)
    for hh in range(H_A):
        vf_ref[pl.ds(hh, x.shape[0], stride=H_A), :] = v[:, hh * DV_A:(hh + 1) * DV_A]
    vb_ref[...] = v.astype(BF16)
    ga_ref[...] = col(3).astype(BF16)
    mq_ref[...] = col(4).astype(BF16)
    mk_ref[...] = (col(5) * (DK_B ** -0.5)).astype(BF16)
    mv_ref[...] = col(6).astype(BF16)
    mo_ref[...] = _sigmoid(col(7)).astype(BF16)
    zif_ref[...] = _dot(hb, wif_ref[...]) + bif_ref[...]
    gb_ref[...] = (_dot(hb, wgb_ref[...]) + bgb_ref[...]).astype(BF16)


def _proj(x2d, mod, per_row, tm, weights):
    (ng, wm, wif, wgb, bm, bif, bgb, gq, gk, seg) = weights
    m_rows = x2d.shape[0]
    nt = m_rows // tm
    if per_row:
        shift_spec = pl.BlockSpec((tm, D_MODEL), lambda i: (i, 0))
        scale_spec = pl.BlockSpec((tm, D_MODEL), lambda i: (i, 1))
    else:
        tiles_per_batch = nt // mod.shape[0]
        shift_spec = pl.BlockSpec((None, 1, D_MODEL), lambda i: (i // tiles_per_batch, 0, 0))
        scale_spec = pl.BlockSpec((None, 1, D_MODEL), lambda i: (i // tiles_per_batch, 0, 1))

    def const(shape):
        return pl.BlockSpec(shape, lambda i: (0, 0))

    def rows(width):
        return pl.BlockSpec((tm, width), lambda i: (i, 0))

    out_dtypes = [BF16, F32, BF16, F32, BF16, BF16, BF16, BF16, BF16, BF16, F32, BF16]
    out_widths = [512, 512, 512, DV_A, 512, 512, 512, 512, 512, 512, GATE_PAD, 512]
    out_rows = [1, 1, 1, H_A, 1, 1, 1, 1, 1, 1, 1, 1]
    out_specs = [pl.BlockSpec((tm * r, w), lambda i: (i, 0)) for w, r in zip(out_widths, out_rows)]
    out_shape = [jax.ShapeDtypeStruct((m_rows * r, w), d)
                 for w, r, d in zip(out_widths, out_rows, out_dtypes)]
    if not per_row:
        out_specs[1] = pl.BlockSpec((None, A_QK, tm),
                                    lambda i: (i // tiles_per_batch, 0, i % tiles_per_batch))
        out_shape[1] = jax.ShapeDtypeStruct((mod.shape[0], A_QK, m_rows // mod.shape[0]), F32)
    return pl.pallas_call(
        functools.partial(_proj_kernel, k_transposed=not per_row),
        grid=(nt,),
        in_specs=[
            rows(D_MODEL), shift_spec, scale_spec, const((1, D_MODEL)),
            const((D_MODEL, N_MAIN)), const((D_MODEL, GATE_PAD)), const((D_MODEL, W_B)),
            const((1, N_MAIN)), const((1, GATE_PAD)), const((1, W_B)),
            const((1, A_QK)), const((1, A_QK)), const((A_QK, A_QK)),
        ],
        out_specs=out_specs,
        out_shape=out_shape,
        compiler_params=pltpu.CompilerParams(
            dimension_semantics=("arbitrary",), vmem_limit_bytes=VMEM_LIMIT),
        name="proj_rows" if per_row else "proj_bcast",
    )(x2d, mod, mod, ng, wm, wif, wgb, bm, bif, bgb, gq, gk, seg)


def _diff_norm_gate(o0, o1, lam, gna, ga):
    d = o0 - lam * o1
    ya = d * lax.rsqrt(jnp.mean(d * d, axis=-1, keepdims=True) + EPS) * gna
    ya = ya * (1.0 - LAM_INIT)
    return ya * _silu(ga)


def _mlstm_out_gate(hh, o, gnb, gb):
    hg = o * hh
    yb = hg * lax.rsqrt(jnp.mean(hg * hg, axis=-1, keepdims=True) + EPS) * gnb
    return yb * _silu(gb)


def _alibi_features(seq):
    pos = jnp.arange(seq, dtype=jnp.int32)
    digits = [(pos // POS_RADIX).astype(F32), (pos % POS_RADIX).astype(F32)]
    ones = jnp.ones((seq,), F32)
    qf, kf = [], []
    for slope in ALIBI_SLOPES:
        pieces = [p.astype(F32) for p in _split3(jnp.float32(slope * LOG2E))]
        qcols, kcols = [], []
        for c in pieces:
            qcols += [POS_RADIX * c * ones, c * ones]
            kcols += digits
        for c in pieces:
            qcols += digits
            kcols += [-POS_RADIX * c * ones, -c * ones]
        pad = ((0, 0), (0, 128 - len(qcols)))
        qf.append(jnp.pad(jnp.stack(qcols, axis=1), pad))
        kf.append(jnp.pad(jnp.stack(kcols, axis=1), pad))
    return jnp.stack(qf).astype(BF16), jnp.stack(kf).astype(BF16)


def _attn_kernel(q_ref, qf_ref, k_ref, kf_ref, v_ref, ga_ref, gna_ref, lam_ref, out_ref, lhs_s):
    t = ATT_T
    qi = pl.program_id(2)

    q = q_ref[...].astype(F32)
    lane = lax.broadcasted_iota(jnp.int32, (t, 2 * DK_A), 1)
    lhs_s[0:t, 0:128] = jnp.where(lane < DK_A, q, 0.0).astype(BF16)
    lhs_s[t:2 * t, 0:128] = jnp.where(lane >= DK_A, q, 0.0).astype(BF16)
    lhs_s[0:t, 128:256] = qf_ref[...]
    lhs_s[t:2 * t, 128:256] = qf_ref[...]

    krow = lax.broadcasted_iota(jnp.int32, (t, 2 * t), 0)
    qcol = lax.broadcasted_iota(jnp.int32, (t, 2 * t), 1)
    future = krow > jnp.where(qcol >= t, qcol - t, qcol)

    def block(kj, carry, masked):
        m, l, acc = carry
        off = pl.multiple_of(kj * t, t)
        kx = jnp.concatenate([k_ref[pl.ds(off, t), :], kf_ref[pl.ds(off, t), :]], axis=1)
        s = _nt_dot(kx, lhs_s[...])
        if masked:
            s = jnp.where(future, NEG, s)
        m_new = jnp.maximum(m, jnp.max(s, axis=0, keepdims=True))
        alpha = jnp.exp2(m - m_new)
        p = jnp.exp2(s - m_new)
        l = alpha * l + jnp.sum(p, axis=0, keepdims=True)
        pv = lax.dot_general(v_ref[pl.ds(off, t), :], p.astype(BF16), (((0,), (0,)), ((), ())),
                             preferred_element_type=F32)
        return m_new, l, alpha * acc + pv

    init = (jnp.full((1, 2 * t), NEG, F32), jnp.zeros((1, 2 * t), F32), jnp.zeros((DV_A, 2 * t), F32))
    carry = lax.fori_loop(0, qi, lambda kj, cr: block(kj, cr, False), init)
    m, l, acc = block(qi, carry, True)
    o = (acc / l).T
    lam = lam_ref[0:1, :]
    out = _diff_norm_gate(o[:t], o[t:], lam, gna_ref[...], ga_ref[...].astype(F32))
    out_ref[...] = out.astype(BF16)


def _attn_prompt(q, k, v, ga, gna, lam_tile, nb, seq):
    nq = seq // ATT_T
    qf, kf = _alibi_features(seq)
    return pl.pallas_call(
        _attn_kernel,
        grid=(nb, H_A, nq),
        in_specs=[
            pl.BlockSpec((ATT_T, 128), lambda b, h, i: (b * nq + i, h)),
            pl.BlockSpec((None, ATT_T, 128), lambda b, h, i: (h, i, 0)),
            pl.BlockSpec((seq, 128), lambda b, h, i: (b, h)),
            pl.BlockSpec((None, seq, 128), lambda b, h, i: (h, 0, 0)),
            pl.BlockSpec((seq, 128), lambda b, h, i: (b, h)),
            pl.BlockSpec((ATT_T, 128), lambda b, h, i: (b * nq + i, h)),
            pl.BlockSpec((1, 128), lambda b, h, i: (0, h)),
            pl.BlockSpec((8, 128), lambda b, h, i: (0, 0)),
        ],
        out_specs=pl.BlockSpec((ATT_T, 128), lambda b, h, i: (b * nq + i, h)),
        out_shape=jax.ShapeDtypeStruct((nb * seq, W_A), BF16),
        scratch_shapes=[pltpu.VMEM((2 * ATT_T, 256), BF16)],
        compiler_params=pltpu.CompilerParams(
            dimension_semantics=("arbitrary", "arbitrary", "arbitrary"),
            vmem_limit_bytes=VMEM_LIMIT),
        name="attn_prompt",
    )(q, qf, k, kf, v, ga, gna, lam_tile)


def _mlstm_chunk(first, q_ref, k_ref, v_ref, o_ref, gb_ref, zif_ref, gnb_ref,
                 y_ref, c_ref, n_ref, m_ref):
    L = CHUNK

    @pl.when(first)
    def _():
        c_ref[...] = jnp.zeros_like(c_ref)
        n_ref[...] = jnp.zeros_like(n_ref)
        m_ref[...] = jnp.zeros_like(m_ref)

    x = zif_ref[...]
    logf = _log_sigmoid(x)
    row = lax.broadcasted_iota(jnp.int32, (L, L), 0)
    colv = lax.broadcasted_iota(jnp.int32, (L, L), 1)
    causal = row >= colv
    tri = jnp.where(causal, 1.0, 0.0).astype(BF16)
    f1, f2, f3 = _split3(logf)
    bmat = _dot(tri, f1) + (_dot(tri, f2) + _dot(tri, f3))
    xt = x.T
    bt = bmat.T

    for h in range(H_B):
        sl = slice(h * 128, (h + 1) * 128)
        qh = q_ref[:, sl]
        kh = k_ref[:, sl]
        vh = v_ref[:, sl]
        icol = x[:, h:h + 1]
        bcol = bmat[:, H_B + h:H_B + h + 1]
        irow = xt[h:h + 1, :]
        brow = bt[H_B + h:H_B + h + 1, :]
        m_prev = m_ref[h:h + 1, 0:1]
        nrow = n_ref[h:h + 1, :]
        c_old = c_ref[h]

        d = jnp.where(causal, (bcol - brow) + irow, NEG)
        inter = bcol + m_prev
        m_t = jnp.maximum(inter, jnp.max(d, axis=1, keepdims=True))
        w_intra = jnp.exp(d - m_t)
        w_inter = jnp.exp(inter - m_t)
        sw = _nt_dot(qh, kh) * w_intra
        num = _dot(sw.astype(BF16), vh) + w_inter * _nt_dot(qh, c_old.astype(BF16))
        den = jnp.sum(sw, axis=1, keepdims=True) \
            + w_inter * jnp.sum(qh.astype(F32) * nrow, axis=1, keepdims=True)
        hh = num / jnp.maximum(jnp.abs(den), jnp.exp(-m_t))
        y = _mlstm_out_gate(hh, o_ref[:, sl].astype(F32), gnb_ref[:, sl], gb_ref[:, sl].astype(F32))
        y_ref[:, sl] = y.astype(BF16)

        m_new = m_t[L - 1:L, :]
        b_last = bcol[L - 1:L, :]
        w_s = jnp.exp(((b_last - bcol) + icol) - m_new)
        decay = jnp.exp((b_last + m_prev) - m_new)
        wv = w_s * vh.astype(F32)
        c_ref[h] = decay * c_old + _dot(wv.T.astype(BF16), kh)
        n_ref[h:h + 1, :] = decay * nrow + jnp.sum(w_s * kh.astype(F32), axis=0, keepdims=True)
        m_ref[h:h + 1, :] = jnp.broadcast_to(m_new, (1, 128))


N_MLSTM_IN = 7
N_MLSTM_OUT = 4


def _paged_kernel(pt_ref, q_ref, kn_ref, vn_ref, ga_ref, gna_ref, lam_ref, *rest,
                  n_chunk_steps, chunks_per_seq):
    pp = PAGES_PER_STEP
    k_refs = rest[:pp]
    v_refs = rest[pp:2 * pp]
    mlstm_in = rest[2 * pp:2 * pp + N_MLSTM_IN]
    out_ref = rest[2 * pp + N_MLSTM_IN]
    mlstm_out = rest[2 * pp + N_MLSTM_IN + 1:2 * pp + N_MLSTM_IN + 1 + N_MLSTM_OUT]
    kb_s, vb_s, m_s, l_s, acc_s = rest[2 * pp + N_MLSTM_IN + 1 + N_MLSTM_OUT:]
    g = pl.program_id(1)
    ng = pl.num_programs(1)
    ntok = pp * PAGE

    step = pl.program_id(0) * ng + g

    @pl.when(step < n_chunk_steps)
    def _():
        _mlstm_chunk(step % chunks_per_seq == 0, *mlstm_in, *mlstm_out)

    @pl.when(g == 0)
    def _():
        m_s[...] = jnp.full_like(m_s, NEG)
        l_s[...] = jnp.zeros_like(l_s)
        acc_s[...] = jnp.zeros_like(acc_s)

    for i in range(pp):
        kb_s[:, i * PAGE:(i + 1) * PAGE] = k_refs[i][...].astype(BF16)
        for h in range(H_A):
            vb_s[h, i * PAGE:(i + 1) * PAGE, :] = v_refs[i][pl.ds(h, PAGE, stride=H_A), :].astype(BF16)

    q = q_ref[...].astype(F32)
    sub = lax.broadcasted_iota(jnp.int32, (8, A_QK), 0)
    lane = lax.broadcasted_iota(jnp.int32, (8, A_QK), 1)
    qbd32 = jnp.where((lane >> 6) == sub, jnp.broadcast_to(q, (8, A_QK)), 0.0)
    qbd = qbd32.astype(BF16)

    j = lax.broadcasted_iota(jnp.int32, (8, 1), 0)
    slope = jnp.where(j < 2, ALIBI_SLOPES[0] * LOG2E,
                      jnp.where(j < 4, ALIBI_SLOPES[1] * LOG2E,
                                jnp.where(j < 6, ALIBI_SLOPES[2] * LOG2E,
                                          ALIBI_SLOPES[3] * LOG2E))).astype(F32)
    kpos = g * ntok + lax.broadcasted_iota(jnp.int32, (1, ntok), 1)
    dist = (ng * ntok - kpos).astype(F32)

    s = _dot(qbd, kb_s[...]) - slope * dist
    m_old = m_s[:, 0:1]
    l_old = l_s[:, 0:1]
    m_new = jnp.maximum(m_old, jnp.max(s, axis=1, keepdims=True))
    alpha = jnp.exp2(m_old - m_new)
    p = jnp.exp2(s - m_new)
    l_new = alpha * l_old + jnp.sum(p, axis=1, keepdims=True)
    pb = p.astype(BF16)
    head_of_row = lax.broadcasted_iota(jnp.int32, (8, DV_A), 0) >> 1
    pv = jnp.zeros((8, DV_A), F32)
    for h in range(H_A):
        pv = jnp.where(head_of_row == h, _dot(pb, vb_s[h]), pv)
    acc = alpha * acc_s[...] + pv
    m_s[...] = jnp.broadcast_to(m_new, m_s.shape)
    l_s[...] = jnp.broadcast_to(l_new, l_s.shape)
    acc_s[...] = acc

    @pl.when(g == ng - 1)
    def _():
        kn = kn_ref[...].astype(F32)
        s_new = jnp.sum(qbd32 * kn, axis=1, keepdims=True)
        m_fin = jnp.maximum(m_new, s_new)
        a2 = jnp.exp2(m_new - m_fin)
        p_new = jnp.exp2(s_new - m_fin)
        l_fin = a2 * l_new + p_new
        vn = jnp.zeros((8, DV_A), F32)
        for h in range(H_A):
            vrow = vn_ref[:, h * DV_A:(h + 1) * DV_A].astype(F32)
            vn = jnp.where(head_of_row == h, jnp.broadcast_to(vrow, (8, DV_A)), vn)
        o = (a2 * acc + p_new * vn) / l_fin
        lam = lam_ref[0:1, :]
        for h in range(H_A):
            sl = slice(h * DV_A, (h + 1) * DV_A)
            out = _diff_norm_gate(o[2 * h:2 * h + 1, :], o[2 * h + 1:2 * h + 2, :], lam,
                                  gna_ref[:, sl], ga_ref[:, sl].astype(F32))
            out_ref[:, sl] = out.astype(BF16)


def _paged_attn_mlstm(page_table, q3, kn3, vn3, ga3, gna, lam_tile, ck, cv,
                      mq, mk, mv, mo, gb, zif, gnb, nb, seq):
    nreq, npages = page_table.shape
    pp = PAGES_PER_STEP
    ng = npages // pp
    nc = seq // CHUNK
    n_chunk_steps = nb * nc
    assert n_chunk_steps <= nreq * ng, "every mLSTM chunk needs a grid step"

    def req(width):
        return pl.BlockSpec((None, 1, width), lambda r, g, pt: (r, 0, 0))

    def page(i):
        return pl.BlockSpec((None, 512, PAGE), lambda r, g, pt: (pt[r, g * pp + i], 0, 0))

    def chunk(r, g):
        return jnp.minimum(r * ng + g, n_chunk_steps - 1)

    def crows(width):
        return pl.BlockSpec((CHUNK, width), lambda r, g, pt: (chunk(r, g), 0))

    def cstate(*shape):
        return pl.BlockSpec((None,) + shape, lambda r, g, pt: (chunk(r, g) // nc,) + (0,) * len(shape))

    return pl.pallas_call(
        functools.partial(_paged_kernel, n_chunk_steps=n_chunk_steps, chunks_per_seq=nc),
        grid_spec=pltpu.PrefetchScalarGridSpec(
            num_scalar_prefetch=1,
            grid=(nreq, ng),
            in_specs=[req(512), req(512), req(512), req(512),
                      pl.BlockSpec((1, W_A), lambda r, g, pt: (0, 0)),
                      pl.BlockSpec((8, 128), lambda r, g, pt: (0, 0))]
                     + [page(i) for i in range(pp)] + [page(i) for i in range(pp)]
                     + [crows(512), crows(512), crows(512), crows(512), crows(512), crows(GATE_PAD),
                        pl.BlockSpec((1, W_B), lambda r, g, pt: (0, 0))],
            out_specs=[req(512), crows(512), cstate(H_B, DV_B, DK_B), cstate(H_B, DK_B), cstate(H_B, 128)],
            scratch_shapes=[
                pltpu.VMEM((A_QK, pp * PAGE), BF16),
                pltpu.VMEM((H_A, pp * PAGE, DV_A), BF16),
                pltpu.VMEM((8, 128), F32),
                pltpu.VMEM((8, 128), F32),
                pltpu.VMEM((8, DV_A), F32),
            ],
        ),
        out_shape=[
            jax.ShapeDtypeStruct((nreq, 1, W_A), BF16),
            jax.ShapeDtypeStruct((nb * seq, W_B), BF16),
            jax.ShapeDtypeStruct((nb, H_B, DV_B, DK_B), F32),
            jax.ShapeDtypeStruct((nb, H_B, DK_B), F32),
            jax.ShapeDtypeStruct((nb, H_B, 128), F32),
        ],
        compiler_params=pltpu.CompilerParams(
            dimension_semantics=("arbitrary", "arbitrary"), vmem_limit_bytes=VMEM_LIMIT),
        name="paged_attn_mlstm",
    )(page_table, q3, kn3, vn3, ga3, gna, lam_tile, *([ck] * pp), *([cv] * pp),
      mq, mk, mv, mo, gb, zif, gnb)


def _mstep_kernel(q_ref, k_ref, v_ref, o_ref, gb_ref, zif_ref, c_ref, n_ref, m_ref, gnb_ref,
                  y_ref, co_ref, no_ref, mo_ref):
    rb = STEP_RB
    x = zif_ref[...]
    sub = lax.broadcasted_iota(jnp.int32, (rb, 128), 0)
    lane = lax.broadcasted_iota(jnp.int32, (rb, 128), 1)
    m_out = jnp.zeros((rb, 128), F32)
    for h in range(H_B):
        sl = slice(h * 128, (h + 1) * 128)
        qb = q_ref[:, sl]
        qh = qb.astype(F32)
        kh = k_ref[:, sl].astype(F32)
        vh = v_ref[:, sl].astype(F32)
        i_c = x[:, h:h + 1]
        b = _log_sigmoid(x[:, H_B + h:H_B + h + 1])
        m_prev = m_ref[:, h:h + 1]
        inter = b + m_prev
        m_t = jnp.maximum(inter, (b - b) + i_c)
        w_intra = jnp.exp(((b - b) + i_c) - m_t)
        w_inter = jnp.exp(inter - m_t)
        sw = jnp.sum(qh * kh, axis=1, keepdims=True) * w_intra
        cq = jnp.zeros((rb, 128), F32)
        for r in range(rb):
            res = _nt_dot(qb, c_ref[r, h].astype(BF16))
            cq = jnp.where(sub == r, res, cq)
        nh = n_ref[:, sl]
        num = sw * vh + w_inter * cq
        den = sw + w_inter * jnp.sum(nh * qh, axis=1, keepdims=True)
        hh = num / jnp.maximum(jnp.abs(den), jnp.exp(-m_t))
        y = _mlstm_out_gate(hh, o_ref[:, sl].astype(F32), gnb_ref[:, sl], gb_ref[:, sl].astype(F32))
        y_ref[:, sl] = y.astype(BF16)

        w_s = jnp.exp(((b - b) + i_c) - m_t)
        decay = jnp.exp((b + m_prev) - m_t)
        wv = w_s * vh
        for r in range(rb):
            vcol = jnp.broadcast_to(wv[r:r + 1, :], (DV_B, DK_B)).T
            co_ref[r, h] = decay[r:r + 1, :] * c_ref[r, h] + vcol * kh[r:r + 1, :]
        no_ref[:, sl] = decay * nh + w_s * kh
        m_out = jnp.where(lane == h, m_t, m_out)
    mo_ref[...] = m_out


def _mlstm_step(mq, mk, mv, mo, gb, zif, c0, n0, m0, gnb):
    nreq = mq.shape[0]
    rb = STEP_RB

    def rows(width):
        return pl.BlockSpec((rb, width), lambda i: (i, 0))

    cspec = pl.BlockSpec((rb, H_B, DV_B, DK_B), lambda i: (i, 0, 0, 0))
    return pl.pallas_call(
        _mstep_kernel,
        grid=(nreq // rb,),
        in_specs=[rows(512), rows(512), rows(512), rows(512), rows(512), rows(GATE_PAD),
                  cspec, rows(512), rows(H_B), pl.BlockSpec((1, W_B), lambda i: (0, 0))],
        out_specs=[rows(512), cspec, rows(512), rows(128)],
        out_shape=[
            jax.ShapeDtypeStruct((nreq, W_B), BF16),
            jax.ShapeDtypeStruct((nreq, H_B, DV_B, DK_B), F32),
            jax.ShapeDtypeStruct((nreq, H_B * DK_B), F32),
            jax.ShapeDtypeStruct((nreq, 128), F32),
        ],
        compiler_params=pltpu.CompilerParams(
            dimension_semantics=("arbitrary",), vmem_limit_bytes=VMEM_LIMIT),
        name="mlstm_step",
    )(mq, mk, mv, mo, gb, zif, c0, n0, m0, gnb)


def _out_kernel(ma_ref, mb_ref, w_ref, x_ref, gate_ref, y_ref):
    acc = _dot(ma_ref[...], w_ref[0:W_A, :]) + _dot(mb_ref[...], w_ref[W_A:W_A + W_B, :])
    y_ref[...] = x_ref[...] + gate_ref[...] * acc


def _out_proj(mix_a, mix_b, w_out, x2d, mod, per_row, tm):
    m_rows = x2d.shape[0]
    nt = m_rows // tm
    if per_row:
        gate_spec = pl.BlockSpec((tm, D_MODEL), lambda i: (i, 2))
    else:
        tiles_per_batch = nt // mod.shape[0]
        gate_spec = pl.BlockSpec((None, 1, D_MODEL), lambda i: (i // tiles_per_batch, 0, 2))
    return pl.pallas_call(
        _out_kernel,
        grid=(nt,),
        in_specs=[
            pl.BlockSpec((tm, W_A), lambda i: (i, 0)),
            pl.BlockSpec((tm, W_B), lambda i: (i, 0)),
            pl.BlockSpec((W_A + W_B, D_MODEL), lambda i: (0, 0)),
            pl.BlockSpec((tm, D_MODEL), lambda i: (i, 0)),
            gate_spec,
        ],
        out_specs=pl.BlockSpec((tm, D_MODEL), lambda i: (i, 0)),
        out_shape=jax.ShapeDtypeStruct((m_rows, D_MODEL), F32),
        compiler_params=pltpu.CompilerParams(
            dimension_semantics=("arbitrary",), vmem_limit_bytes=VMEM_LIMIT),
        name="out_rows" if per_row else "out_bcast",
    )(mix_a, mix_b, w_out, x2d, mod)


def kernel(x_prompt, x_sample, c_prompt, c_sample, cache_k, cache_v, state_C, state_n, state_m,
           page_table, norm_g, w_ada, b_ada, w_in, b_in, g_q, g_k, lam_q, lam_k, gn_a, gn_b, w_out):
    assert w_in.shape[0] == 1, "single-layer model"
    nb, seq, _ = x_prompt.shape
    nreq = x_sample.shape[0]
    assert x_sample.shape[1] == 1

    w = w_in[0]
    bvec = b_in[0]
    gate_lo = N_MAIN
    gb_lo = N_MAIN + 2 * H_B
    wm = w[:, :N_MAIN].astype(BF16)
    wif = jnp.pad(w[:, gate_lo:gb_lo], ((0, 0), (0, GATE_PAD - 2 * H_B))).astype(BF16)
    wgb = w[:, gb_lo:].astype(BF16)
    bm = bvec[None, :N_MAIN]
    bif = jnp.pad(bvec[gate_lo:gb_lo], (0, GATE_PAD - 2 * H_B))[None, :]
    bgb = bvec[None, gb_lo:]
    gq = jnp.tile(g_q[0], 2 * H_A)[None, :]
    gk = jnp.tile(g_k[0], 2 * H_A)[None, :]
    grp = jnp.arange(A_QK) // DK_A
    seg = (grp[:, None] == grp[None, :]).astype(BF16)
    weights = (norm_g[0][None, :], wm, wif, wgb, bm, bif, bgb, gq, gk, seg)
    gna = gn_a[0].reshape(1, W_A)
    gnb = gn_b[0].reshape(1, W_B)
    wo = w_out[0].astype(BF16)

    pad_rows = (-(nb + nreq)) % 8
    c_all = jnp.concatenate([c_prompt, c_sample, jnp.zeros((pad_rows, D_MODEL), F32)], axis=0)
    mod, lam_tile = _ada(c_all, w_ada[0], b_ada[0][None, :], lam_q[0], lam_k[0])
    mod_p = mod[:nb].reshape(nb, 1, 3 * D_MODEL)
    mod_s = mod[nb:nb + nreq]

    xp = x_prompt.reshape(nb * seq, D_MODEL)
    (q, kf, kb, vf, vb, ga, mq, mk, mv, mo, zif, gb) = _proj(xp, mod_p, False, PROJ_TM, weights)
    xs = x_sample.reshape(nreq, D_MODEL)
    (qs, kfs, kbs, vfs, vbs, gas, mqs, mks, mvs, mos, zifs, gbs) = _proj(xs, mod_s, True, nreq, weights)

    mix_a = _attn_prompt(q, kb, vb, ga, gna, lam_tile, nb, seq)
    ck = jnp.transpose(cache_k[0], (0, 2, 3, 4, 1)).reshape(cache_k.shape[1], A_QK, PAGE)
    cv = cache_v[0].reshape(cache_v.shape[1], PAGE * H_A, DV_A)
    r3 = lambda a: a.reshape(nreq, 1, a.shape[-1])
    mix_as, mix_b, c_p, n_p, m_p = _paged_attn_mlstm(
        page_table, r3(qs), r3(kbs), r3(vbs), r3(gas), gna, lam_tile, ck, cv,
        mq, mk, mv, mo, gb, zif, gnb, nb, seq)
    mix_bs, c_s, n_s, m_s = _mlstm_step(mqs, mks, mvs, mos, gbs, zifs, state_C[0],
                                        state_n[0].reshape(nreq, H_B * DK_B), state_m[0], gnb)

    y_p = _out_proj(mix_a, mix_b, wo, xp, mod_p, False, OUT_TM)
    y_s = _out_proj(mix_as.reshape(nreq, W_A), mix_bs, wo, xs, mod_s, True, nreq)

    return (
        y_p.reshape(nb, seq, D_MODEL),
        y_s.reshape(nreq, 1, D_MODEL),
        jnp.transpose(kf.reshape(nb, H_A, 2, DK_A, seq), (0, 4, 1, 2, 3))[None],
        vf.reshape(1, nb, seq, H_A, DV_A),
        kfs.reshape(1, nreq, 1, H_A, 2, DK_A),
        vfs.reshape(1, nreq, 1, H_A, DV_A),
        c_p[None],
        n_p[None],
        m_p[:, :, 0][None],
        c_s[None],
        n_s.reshape(1, nreq, H_B, DK_B),
        m_s[:, :H_B][None],
    )
```

```python
import functools
import math

import jax
import jax.numpy as jnp
from jax import lax
from jax.experimental import pallas as pl
from jax.experimental.pallas import tpu as pltpu

F32 = jnp.float32
BF16 = jnp.bfloat16

D_MODEL = 1024
H_A = 4
DK_A = 64
DV_A = 128
A_QK = H_A * 2 * DK_A
W_A = H_A * DV_A
H_B = 4
DK_B = 128
DV_B = 128
W_B = H_B * DV_B
N_MAIN = 8 * 512
GATE_PAD = 128
PAGE = 128
CHUNK = 128
EPS = 1e-6
NEG = -1e30
LAM_INIT = 0.8 - 0.6 * math.exp(-0.3 * 0)
ALIBI_SLOPES = tuple(2.0 ** (-8.0 * (h + 1) / H_A) for h in range(H_A))
LOG2E = math.log2(math.e)
POS_RADIX = 64

VMEM_LIMIT = 56 * 1024 * 1024

PROJ_TM = 512
OUT_TM = 512
ATT_T = 512
PAGES_PER_STEP = 32
STEP_RB = 8


def _nt_dot(a, b):
    return lax.dot_general(a, b, (((1,), (1,)), ((), ())), preferred_element_type=F32)


def _dot(a, b):
    return jnp.dot(a, b, preferred_element_type=F32)


def _sigmoid(x):
    return 1.0 / (1.0 + jnp.exp(-x))


def _silu(x):
    return x * _sigmoid(x)


def _log_sigmoid(x):
    return jnp.minimum(x, 0.0) - jnp.log1p(jnp.exp(-jnp.abs(x)))


def _split3(a):
    a1 = a.astype(BF16)
    r1 = a - a1.astype(F32)
    a2 = r1.astype(BF16)
    a3 = (r1 - a2.astype(F32)).astype(BF16)
    return a1, a2, a3


def _ada_kernel(c_ref, w_ref, b_ref, lq_ref, lk_ref, mod_ref, lam_ref):
    a = _silu(c_ref[...])
    w = w_ref[...]
    a1, a2, _ = _split3(a)
    w1, w2, _ = _split3(w)
    mod_ref[...] = (_dot(a1, w1) + (_dot(a1, w2) + _dot(a2, w1))) + b_ref[...]

    @pl.when(pl.program_id(0) == 0)
    def _():
        s = jnp.sum(lq_ref[...] * lk_ref[...], axis=1, keepdims=True)
        e = jnp.exp(s)
        lam = e[0:1, :] - e[1:2, :] + LAM_INIT
        lam_ref[...] = jnp.broadcast_to(lam, lam_ref.shape)


def _ada(c_all, w_ada, b_ada, lam_q, lam_k):
    rows = c_all.shape[0]
    nblk = 3
    return pl.pallas_call(
        _ada_kernel,
        grid=(nblk,),
        in_specs=[
            pl.BlockSpec((rows, D_MODEL), lambda j: (0, 0)),
            pl.BlockSpec((D_MODEL, D_MODEL), lambda j: (0, j)),
            pl.BlockSpec((1, D_MODEL), lambda j: (0, j)),
            pl.BlockSpec((2, DK_A), lambda j: (0, 0)),
            pl.BlockSpec((2, DK_A), lambda j: (0, 0)),
        ],
        out_specs=[
            pl.BlockSpec((rows, D_MODEL), lambda j: (0, j)),
            pl.BlockSpec((8, 128), lambda j: (0, 0)),
        ],
        out_shape=[
            jax.ShapeDtypeStruct((rows, 3 * D_MODEL), F32),
            jax.ShapeDtypeStruct((8, 128), F32),
        ],
        compiler_params=pltpu.CompilerParams(
            dimension_semantics=("arbitrary",), vmem_limit_bytes=VMEM_LIMIT),
        name="ada",
    )(c_all, w_ada, b_ada, lam_q, lam_k)


def _proj_kernel(x_ref, shift_ref, scale_ref, ng_ref, wm_ref, wif_ref, wgb_ref,
                 bm_ref, bif_ref, bgb_ref, gq_ref, gk_ref, seg_ref,
                 q_ref, kf_ref, kb_ref, vf_ref, vb_ref, ga_ref,
                 mq_ref, mk_ref, mv_ref, mo_ref, zif_ref, gb_ref, *, k_transposed):
    x = x_ref[...]
    ms = jnp.mean(x * x, axis=-1, keepdims=True)
    h = x * lax.rsqrt(ms + EPS) * ng_ref[...]
    h = h * (1.0 + scale_ref[...]) + shift_ref[...]
    hb = h.astype(BF16)

    def col(j):
        sl = slice(j * 512, (j + 1) * 512)
        return _dot(hb, wm_ref[:, sl]) + bm_ref[:, sl]

    def headnorm(z, g):
        ss = _dot((z * z).astype(BF16), seg_ref[...])
        return z * lax.rsqrt(ss * (1.0 / DK_A) + EPS) * g

    q_ref[...] = (headnorm(col(0), gq_ref[...]) * (DK_A ** -0.5 * LOG2E)).astype(BF16)
    k = headnorm(col(1), gk_ref[...])
    kf_ref[...] = k.T if k_transposed else k
    kb_ref[...] = k.astype(BF16)
    v = col(2)
    for hh in range(H_A):
        vf_ref[pl.ds(hh, x.shape[0], stride=H_A), :] = v[:, hh * DV_A:(hh + 1) * DV_A]
    vb_ref[...] = v.astype(BF16)
    ga_ref[...] = col(3).astype(BF16)
    mq_ref[...] = col(4).astype(BF16)
    mk_ref[...] = (col(5) * (DK_B ** -0.5)).astype(BF16)
    mv_ref[...] = col(6).astype(BF16)
    mo_ref[...] = _sigmoid(col(7)).astype(BF16)
    zif_ref[...] = _dot(hb, wif_ref[...]) + bif_ref[...]
    gb_ref[...] = (_dot(hb, wgb_ref[...]) + bgb_ref[...]).astype(BF16)


def _proj(x2d, mod, per_row, tm, weights):
    (ng, wm, wif, wgb, bm, bif, bgb, gq, gk, seg) = weights
    m_rows = x2d.shape[0]
    nt = m_rows // tm
    if per_row:
        shift_spec = pl.BlockSpec((tm, D_MODEL), lambda i: (i, 0))
        scale_spec = pl.BlockSpec((tm, D_MODEL), lambda i: (i, 1))
    else:
        tiles_per_batch = nt // mod.shape[0]
        shift_spec = pl.BlockSpec((None, 1, D_MODEL), lambda i: (i // tiles_per_batch, 0, 0))
        scale_spec = pl.BlockSpec((None, 1, D_MODEL), lambda i: (i // tiles_per_batch, 0, 1))

    def const(shape):
        return pl.BlockSpec(shape, lambda i: (0, 0))

    def rows(width):
        return pl.BlockSpec((tm, width), lambda i: (i, 0))

    out_dtypes = [BF16, F32, BF16, F32, BF16, BF16, BF16, BF16, BF16, BF16, F32, BF16]
    out_widths = [512, 512, 512, DV_A, 512, 512, 512, 512, 512, 512, GATE_PAD, 512]
    out_rows = [1, 1, 1, H_A, 1, 1, 1, 1, 1, 1, 1, 1]
    out_specs = [pl.BlockSpec((tm * r, w), lambda i: (i, 0)) for w, r in zip(out_widths, out_rows)]
    out_shape = [jax.ShapeDtypeStruct((m_rows * r, w), d)
                 for w, r, d in zip(out_widths, out_rows, out_dtypes)]
    if not per_row:
        out_specs[1] = pl.BlockSpec((None, A_QK, tm),
                                    lambda i: (i // tiles_per_batch, 0, i % tiles_per_batch))
        out_shape[1] = jax.ShapeDtypeStruct((mod.shape[0], A_QK, m_rows // mod.shape[0]), F32)
    return pl.pallas_call(
        functools.partial(_proj_kernel, k_transposed=not per_row),
        grid=(nt,),
        in_specs=[
            rows(D_MODEL), shift_spec, scale_spec, const((1, D_MODEL)),
            const((D_MODEL, N_MAIN)), const((D_MODEL, GATE_PAD)), const((D_MODEL, W_B)),
            const((1, N_MAIN)), const((1, GATE_PAD)), const((1, W_B)),
            const((1, A_QK)), const((1, A_QK)), const((A_QK, A_QK)),
        ],
        out_specs=out_specs,
        out_shape=out_shape,
        compiler_params=pltpu.CompilerParams(
            dimension_semantics=("arbitrary",), vmem_limit_bytes=VMEM_LIMIT),
        name="proj_rows" if per_row else "proj_bcast",
    )(x2d, mod, mod, ng, wm, wif, wgb, bm, bif, bgb, gq, gk, seg)


def _diff_norm_gate(o0, o1, lam, gna, ga):
    d = o0 - lam * o1
    ya = d * lax.rsqrt(jnp.mean(d * d, axis=-1, keepdims=True) + EPS) * gna
    ya = ya * (1.0 - LAM_INIT)
    return ya * _silu(ga)


def _mlstm_out_gate(hh, o, gnb, gb):
    hg = o * hh
    yb = hg * lax.rsqrt(jnp.mean(hg * hg, axis=-1, keepdims=True) + EPS) * gnb
    return yb * _silu(gb)


def _alibi_features(seq):
    pos = jnp.arange(seq, dtype=jnp.int32)
    digits = [(pos // POS_RADIX).astype(F32), (pos % POS_RADIX).astype(F32)]
    ones = jnp.ones((seq,), F32)
    qf, kf = [], []
    for slope in ALIBI_SLOPES:
        pieces = [p.astype(F32) for p in _split3(jnp.float32(slope * LOG2E))]
        qcols, kcols = [], []
        for c in pieces:
            qcols += [POS_RADIX * c * ones, c * ones]
            kcols += digits
        for c in pieces:
            qcols += digits
            kcols += [-POS_RADIX * c * ones, -c * ones]
        pad = ((0, 0), (0, 128 - len(qcols)))
        qf.append(jnp.pad(jnp.stack(qcols, axis=1), pad))
        kf.append(jnp.pad(jnp.stack(kcols, axis=1), pad))
    return jnp.stack(qf).astype(BF16), jnp.stack(kf).astype(BF16)


def _attn_kernel(q_ref, qf_ref, k_ref, kf_ref, v_ref, ga_ref, gna_ref, lam_ref, out_ref, lhs_s, s_s):
    t = ATT_T
    qi = pl.program_id(2)

    q = q_ref[...].astype(F32)
    lane = lax.broadcasted_iota(jnp.int32, (t, 2 * DK_A), 1)
    lhs_s[0:t, 0:128] = jnp.where(lane < DK_A, q, 0.0).astype(BF16)
    lhs_s[t:2 * t, 0:128] = jnp.where(lane >= DK_A, q, 0.0).astype(BF16)
    lhs_s[0:t, 128:256] = qf_ref[...]
    lhs_s[t:2 * t, 128:256] = qf_ref[...]

    krow = lax.broadcasted_iota(jnp.int32, (t, 2 * t), 0)
    qcol = lax.broadcasted_iota(jnp.int32, (t, 2 * t), 1)
    future = krow > jnp.where(qcol >= t, qcol - t, qcol)

    def scores(kj, slot):
        off = pl.multiple_of(kj * t, t)
        kx = jnp.concatenate([k_ref[pl.ds(off, t), :], kf_ref[pl.ds(off, t), :]], axis=1)
        s_s[slot] = _nt_dot(kx, lhs_s[...])

    def consume(kj, slot, carry, masked):
        m, l, acc = carry
        off = pl.multiple_of(kj * t, t)
        s = s_s[slot]
        if masked:
            s = jnp.where(future, NEG, s)
        m_new = jnp.maximum(m, jnp.max(s, axis=0, keepdims=True))
        alpha = jnp.exp2(m - m_new)
        p = jnp.exp2(s - m_new)
        l = alpha * l + jnp.sum(p, axis=0, keepdims=True)
        pv = lax.dot_general(v_ref[pl.ds(off, t), :], p.astype(BF16), (((0,), (0,)), ((), ())),
                             preferred_element_type=F32)
        return m_new, l, alpha * acc + pv

    def pair(i, carry):
        scores(2 * i + 1, 1)
        carry = consume(2 * i, 0, carry, False)
        scores(2 * i + 2, 0)
        return consume(2 * i + 1, 1, carry, False)

    def tail_even(carry):
        return consume(qi, 0, carry, True)

    def tail_odd(carry):
        scores(qi, 1)
        return consume(qi, 1, consume(qi - 1, 0, carry, False), True)

    init = (jnp.full((1, 2 * t), NEG, F32), jnp.zeros((1, 2 * t), F32), jnp.zeros((DV_A, 2 * t), F32))
    scores(0, 0)
    carry = lax.fori_loop(0, qi // 2, pair, init)
    m, l, acc = lax.cond(qi % 2 == 1, tail_odd, tail_even, carry)
    o = (acc / l).T
    lam = lam_ref[0:1, :]
    out = _diff_norm_gate(o[:t], o[t:], lam, gna_ref[...], ga_ref[...].astype(F32))
    out_ref[...] = out.astype(BF16)


def _attn_prompt(q, k, v, ga, gna, lam_tile, nb, seq):
    nq = seq // ATT_T
    qf, kf = _alibi_features(seq)
    return pl.pallas_call(
        _attn_kernel,
        grid=(nb, H_A, nq),
        in_specs=[
            pl.BlockSpec((ATT_T, 128), lambda b, h, i: (b * nq + i, h)),
            pl.BlockSpec((None, ATT_T, 128), lambda b, h, i: (h, i, 0)),
            pl.BlockSpec((seq, 128), lambda b, h, i: (b, h)),
            pl.BlockSpec((None, seq, 128), lambda b, h, i: (h, 0, 0)),
            pl.BlockSpec((seq, 128), lambda b, h, i: (b, h)),
            pl.BlockSpec((ATT_T, 128), lambda b, h, i: (b * nq + i, h)),
            pl.BlockSpec((1, 128), lambda b, h, i: (0, h)),
            pl.BlockSpec((8, 128), lambda b, h, i: (0, 0)),
        ],
        out_specs=pl.BlockSpec((ATT_T, 128), lambda b, h, i: (b * nq + i, h)),
        out_shape=jax.ShapeDtypeStruct((nb * seq, W_A), BF16),
        scratch_shapes=[pltpu.VMEM((2 * ATT_T, 256), BF16),
                        pltpu.VMEM((2, ATT_T, 2 * ATT_T), F32)],
        compiler_params=pltpu.CompilerParams(
            dimension_semantics=("arbitrary", "arbitrary", "arbitrary"),
            vmem_limit_bytes=VMEM_LIMIT),
        name="attn_prompt",
    )(q, qf, k, kf, v, ga, gna, lam_tile)


def _mlstm_chunk(first, q_ref, k_ref, v_ref, o_ref, gb_ref, zif_ref, gnb_ref,
                 y_ref, c_ref, n_ref, m_ref):
    L = CHUNK

    @pl.when(first)
    def _():
        c_ref[...] = jnp.zeros_like(c_ref)
        n_ref[...] = jnp.zeros_like(n_ref)
        m_ref[...] = jnp.zeros_like(m_ref)

    x = zif_ref[...]
    logf = _log_sigmoid(x)
    row = lax.broadcasted_iota(jnp.int32, (L, L), 0)
    colv = lax.broadcasted_iota(jnp.int32, (L, L), 1)
    causal = row >= colv
    tri = jnp.where(causal, 1.0, 0.0).astype(BF16)
    f1, f2, f3 = _split3(logf)
    bmat = _dot(tri, f1) + (_dot(tri, f2) + _dot(tri, f3))
    xt = x.T
    bt = bmat.T

    for h in range(H_B):
        sl = slice(h * 128, (h + 1) * 128)
        qh = q_ref[:, sl]
        kh = k_ref[:, sl]
        vh = v_ref[:, sl]
        icol = x[:, h:h + 1]
        bcol = bmat[:, H_B + h:H_B + h + 1]
        irow = xt[h:h + 1, :]
        brow = bt[H_B + h:H_B + h + 1, :]
        m_prev = m_ref[h:h + 1, 0:1]
        nrow = n_ref[h:h + 1, :]
        c_old = c_ref[h]

        d = jnp.where(causal, (bcol - brow) + irow, NEG)
        inter = bcol + m_prev
        m_t = jnp.maximum(inter, jnp.max(d, axis=1, keepdims=True))
        w_intra = jnp.exp(d - m_t)
        w_inter = jnp.exp(inter - m_t)
        sw = _nt_dot(qh, kh) * w_intra
        num = _dot(sw.astype(BF16), vh) + w_inter * _nt_dot(qh, c_old.astype(BF16))
        den = jnp.sum(sw, axis=1, keepdims=True) \
            + w_inter * jnp.sum(qh.astype(F32) * nrow, axis=1, keepdims=True)
        hh = num / jnp.maximum(jnp.abs(den), jnp.exp(-m_t))
        y = _mlstm_out_gate(hh, o_ref[:, sl].astype(F32), gnb_ref[:, sl], gb_ref[:, sl].astype(F32))
        y_ref[:, sl] = y.astype(BF16)

        m_new = m_t[L - 1:L, :]
        b_last = bcol[L - 1:L, :]
        w_s = jnp.exp(((b_last - bcol) + icol) - m_new)
        decay = jnp.exp((b_last + m_prev) - m_new)
        wv = w_s * vh.astype(F32)
        c_ref[h] = decay * c_old + _dot(wv.T.astype(BF16), kh)
        n_ref[h:h + 1, :] = decay * nrow + jnp.sum(w_s * kh.astype(F32), axis=0, keepdims=True)
        m_ref[h:h + 1, :] = jnp.broadcast_to(m_new, (1, 128))


N_MLSTM_IN = 7
N_MLSTM_OUT = 4


def _paged_kernel(pt_ref, q_ref, kn_ref, vn_ref, ga_ref, gna_ref, lam_ref, *rest,
                  n_chunk_steps, chunks_per_seq):
    pp = PAGES_PER_STEP
    k_refs = rest[:pp]
    v_refs = rest[pp:2 * pp]
    mlstm_in = rest[2 * pp:2 * pp + N_MLSTM_IN]
    out_ref = rest[2 * pp + N_MLSTM_IN]
    mlstm_out = rest[2 * pp + N_MLSTM_IN + 1:2 * pp + N_MLSTM_IN + 1 + N_MLSTM_OUT]
    kb_s, vb_s, m_s, l_s, acc_s = rest[2 * pp + N_MLSTM_IN + 1 + N_MLSTM_OUT:]
    g = pl.program_id(1)
    ng = pl.num_programs(1)
    ntok = pp * PAGE

    step = pl.program_id(0) * ng + g

    @pl.when(step < n_chunk_steps)
    def _():
        _mlstm_chunk(step % chunks_per_seq == 0, *mlstm_in, *mlstm_out)

    @pl.when(g == 0)
    def _():
        m_s[...] = jnp.full_like(m_s, NEG)
        l_s[...] = jnp.zeros_like(l_s)
        acc_s[...] = jnp.zeros_like(acc_s)

    for i in range(pp):
        kb_s[:, i * PAGE:(i + 1) * PAGE] = k_refs[i][...].astype(BF16)
        for h in range(H_A):
            vb_s[h, i * PAGE:(i + 1) * PAGE, :] = v_refs[i][pl.ds(h, PAGE, stride=H_A), :].astype(BF16)

    q = q_ref[...].astype(F32)
    sub = lax.broadcasted_iota(jnp.int32, (8, A_QK), 0)
    lane = lax.broadcasted_iota(jnp.int32, (8, A_QK), 1)
    qbd32 = jnp.where((lane >> 6) == sub, jnp.broadcast_to(q, (8, A_QK)), 0.0)
    qbd = qbd32.astype(BF16)

    j = lax.broadcasted_iota(jnp.int32, (8, 1), 0)
    slope = jnp.where(j < 2, ALIBI_SLOPES[0] * LOG2E,
                      jnp.where(j < 4, ALIBI_SLOPES[1] * LOG2E,
                                jnp.where(j < 6, ALIBI_SLOPES[2] * LOG2E,
                                          ALIBI_SLOPES[3] * LOG2E))).astype(F32)
    kpos = g * ntok + lax.broadcasted_iota(jnp.int32, (1, ntok), 1)
    dist = (ng * ntok - kpos).astype(F32)

    s = _dot(qbd, kb_s[...]) - slope * dist
    m_old = m_s[:, 0:1]
    l_old = l_s[:, 0:1]
    m_new = jnp.maximum(m_old, jnp.max(s, axis=1, keepdims=True))
    alpha = jnp.exp2(m_old - m_new)
    p = jnp.exp2(s - m_new)
    l_new = alpha * l_old + jnp.sum(p, axis=1, keepdims=True)
    pb = p.astype(BF16)
    head_of_row = lax.broadcasted_iota(jnp.int32, (8, DV_A), 0) >> 1
    pv = jnp.zeros((8, DV_A), F32)
    for h in range(H_A):
        pv = jnp.where(head_of_row == h, _dot(pb, vb_s[h]), pv)
    acc = alpha * acc_s[...] + pv
    m_s[...] = jnp.broadcast_to(m_new, m_s.shape)
    l_s[...] = jnp.broadcast_to(l_new, l_s.shape)
    acc_s[...] = acc

    @pl.when(g == ng - 1)
    def _():
        kn = kn_ref[...].astype(F32)
        s_new = jnp.sum(qbd32 * kn, axis=1, keepdims=True)
        m_fin = jnp.maximum(m_new, s_new)
        a2 = jnp.exp2(m_new - m_fin)
        p_new = jnp.exp2(s_new - m_fin)
        l_fin = a2 * l_new + p_new
        vn = jnp.zeros((8, DV_A), F32)
        for h in range(H_A):
            vrow = vn_ref[:, h * DV_A:(h + 1) * DV_A].astype(F32)
            vn = jnp.where(head_of_row == h, jnp.broadcast_to(vrow, (8, DV_A)), vn)
        o = (a2 * acc + p_new * vn) / l_fin
        lam = lam_ref[0:1, :]
        for h in range(H_A):
            sl = slice(h * DV_A, (h + 1) * DV_A)
            out = _diff_norm_gate(o[2 * h:2 * h + 1, :], o[2 * h + 1:2 * h + 2, :], lam,
                                  gna_ref[:, sl], ga_ref[:, sl].astype(F32))
            out_ref[:, sl] = out.astype(BF16)


def _paged_attn_mlstm(page_table, q3, kn3, vn3, ga3, gna, lam_tile, ck, cv,
                      mq, mk, mv, mo, gb, zif, gnb, nb, seq):
    nreq, npages = page_table.shape
    pp = PAGES_PER_STEP
    ng = npages // pp
    nc = seq // CHUNK
    n_chunk_steps = nb * nc
    assert n_chunk_steps <= nreq * ng, "every mLSTM chunk needs a grid step"

    def req(width):
        return pl.BlockSpec((None, 1, width), lambda r, g, pt: (r, 0, 0))

    def page(i):
        return pl.BlockSpec((None, 512, PAGE), lambda r, g, pt: (pt[r, g * pp + i], 0, 0))

    def chunk(r, g):
        return jnp.minimum(r * ng + g, n_chunk_steps - 1)

    def crows(width):
        return pl.BlockSpec((CHUNK, width), lambda r, g, pt: (chunk(r, g), 0))

    def cstate(*shape):
        return pl.BlockSpec((None,) + shape, lambda r, g, pt: (chunk(r, g) // nc,) + (0,) * len(shape))

    return pl.pallas_call(
        functools.partial(_paged_kernel, n_chunk_steps=n_chunk_steps, chunks_per_seq=nc),
        grid_spec=pltpu.PrefetchScalarGridSpec(
            num_scalar_prefetch=1,
            grid=(nreq, ng),
            in_specs=[req(512), req(512), req(512), req(512),
                      pl.BlockSpec((1, W_A), lambda r, g, pt: (0, 0)),
                      pl.BlockSpec((8, 128), lambda r, g, pt: (0, 0))]
                     + [page(i) for i in range(pp)] + [page(i) for i in range(pp)]
                     + [crows(512), crows(512), crows(512), crows(512), crows(512), crows(GATE_PAD),
                        pl.BlockSpec((1, W_B), lambda r, g, pt: (0, 0))],
            out_specs=[req(512), crows(512), cstate(H_B, DV_B, DK_B), cstate(H_B, DK_B), cstate(H_B, 128)],
            scratch_shapes=[
                pltpu.VMEM((A_QK, pp * PAGE), BF16),
                pltpu.VMEM((H_A, pp * PAGE, DV_A), BF16),
                pltpu.VMEM((8, 128), F32),
                pltpu.VMEM((8, 128), F32),
                pltpu.VMEM((8, DV_A), F32),
            ],
        ),
        out_shape=[
            jax.ShapeDtypeStruct((nreq, 1, W_A), BF16),
            jax.ShapeDtypeStruct((nb * seq, W_B), BF16),
            jax.ShapeDtypeStruct((nb, H_B, DV_B, DK_B), F32),
            jax.ShapeDtypeStruct((nb, H_B, DK_B), F32),
            jax.ShapeDtypeStruct((nb, H_B, 128), F32),
        ],
        compiler_params=pltpu.CompilerParams(
            dimension_semantics=("arbitrary", "arbitrary"), vmem_limit_bytes=VMEM_LIMIT),
        name="paged_attn_mlstm",
    )(page_table, q3, kn3, vn3, ga3, gna, lam_tile, *([ck] * pp), *([cv] * pp),
      mq, mk, mv, mo, gb, zif, gnb)


def _mstep_kernel(q_ref, k_ref, v_ref, o_ref, gb_ref, zif_ref, c_ref, n_ref, m_ref, gnb_ref,
                  y_ref, co_ref, no_ref, mo_ref):
    rb = STEP_RB
    x = zif_ref[...]
    sub = lax.broadcasted_iota(jnp.int32, (rb, 128), 0)
    lane = lax.broadcasted_iota(jnp.int32, (rb, 128), 1)
    m_out = jnp.zeros((rb, 128), F32)
    for h in range(H_B):
        sl = slice(h * 128, (h + 1) * 128)
        qb = q_ref[:, sl]
        qh = qb.astype(F32)
        kh = k_ref[:, sl].astype(F32)
        vh = v_ref[:, sl].astype(F32)
        i_c = x[:, h:h + 1]
        b = _log_sigmoid(x[:, H_B + h:H_B + h + 1])
        m_prev = m_ref[:, h:h + 1]
        inter = b + m_prev
        m_t = jnp.maximum(inter, (b - b) + i_c)
        w_intra = jnp.exp(((b - b) + i_c) - m_t)
        w_inter = jnp.exp(inter - m_t)
        sw = jnp.sum(qh * kh, axis=1, keepdims=True) * w_intra
        cq = jnp.zeros((rb, 128), F32)
        for r in range(rb):
            res = _nt_dot(qb, c_ref[r, h].astype(BF16))
            cq = jnp.where(sub == r, res, cq)
        nh = n_ref[:, sl]
        num = sw * vh + w_inter * cq
        den = sw + w_inter * jnp.sum(nh * qh, axis=1, keepdims=True)
        hh = num / jnp.maximum(jnp.abs(den), jnp.exp(-m_t))
        y = _mlstm_out_gate(hh, o_ref[:, sl].astype(F32), gnb_ref[:, sl], gb_ref[:, sl].astype(F32))
        y_ref[:, sl] = y.astype(BF16)

        w_s = jnp.exp(((b - b) + i_c) - m_t)
        decay = jnp.exp((b + m_prev) - m_t)
        wv = w_s * vh
        for r in range(rb):
            vcol = jnp.broadcast_to(wv[r:r + 1, :], (DV_B, DK_B)).T
            co_ref[r, h] = decay[r:r + 1, :] * c_ref[r, h] + vcol * kh[r:r + 1, :]
        no_ref[:, sl] = decay * nh + w_s * kh
        m_out = jnp.where(lane == h, m_t, m_out)
    mo_ref[...] = m_out


def _mlstm_step(mq, mk, mv, mo, gb, zif, c0, n0, m0, gnb):
    nreq = mq.shape[0]
    rb = STEP_RB

    def rows(width):
        return pl.BlockSpec((rb, width), lambda i: (i, 0))

    cspec = pl.BlockSpec((rb, H_B, DV_B, DK_B), lambda i: (i, 0, 0, 0))
    return pl.pallas_call(
        _mstep_kernel,
        grid=(nreq // rb,),
        in_specs=[rows(512), rows(512), rows(512), rows(512), rows(512), rows(GATE_PAD),
                  cspec, rows(512), rows(H_B), pl.BlockSpec((1, W_B), lambda i: (0, 0))],
        out_specs=[rows(512), cspec, rows(512), rows(128)],
        out_shape=[
            jax.ShapeDtypeStruct((nreq, W_B), BF16),
            jax.ShapeDtypeStruct((nreq, H_B, DV_B, DK_B), F32),
            jax.ShapeDtypeStruct((nreq, H_B * DK_B), F32),
            jax.ShapeDtypeStruct((nreq, 128), F32),
        ],
        compiler_params=pltpu.CompilerParams(
            dimension_semantics=("arbitrary",), vmem_limit_bytes=VMEM_LIMIT),
        name="mlstm_step",
    )(mq, mk, mv, mo, gb, zif, c0, n0, m0, gnb)


def _out_kernel(ma_ref, mb_ref, w_ref, x_ref, gate_ref, y_ref):
    acc = _dot(ma_ref[...], w_ref[0:W_A, :]) + _dot(mb_ref[...], w_ref[W_A:W_A + W_B, :])
    y_ref[...] = x_ref[...] + gate_ref[...] * acc


def _out_proj(mix_a, mix_b, w_out, x2d, mod, per_row, tm):
    m_rows = x2d.shape[0]
    nt = m_rows // tm
    if per_row:
        gate_spec = pl.BlockSpec((tm, D_MODEL), lambda i: (i, 2))
    else:
        tiles_per_batch = nt // mod.shape[0]
        gate_spec = pl.BlockSpec((None, 1, D_MODEL), lambda i: (i // tiles_per_batch, 0, 2))
    return pl.pallas_call(
        _out_kernel,
        grid=(nt,),
        in_specs=[
            pl.BlockSpec((tm, W_A), lambda i: (i, 0)),
            pl.BlockSpec((tm, W_B), lambda i: (i, 0)),
            pl.BlockSpec((W_A + W_B, D_MODEL), lambda i: (0, 0)),
            pl.BlockSpec((tm, D_MODEL), lambda i: (i, 0)),
            gate_spec,
        ],
        out_specs=pl.BlockSpec((tm, D_MODEL), lambda i: (i, 0)),
        out_shape=jax.ShapeDtypeStruct((m_rows, D_MODEL), F32),
        compiler_params=pltpu.CompilerParams(
            dimension_semantics=("arbitrary",), vmem_limit_bytes=VMEM_LIMIT),
        name="out_rows" if per_row else "out_bcast",
    )(mix_a, mix_b, w_out, x2d, mod)


def kernel(x_prompt, x_sample, c_prompt, c_sample, cache_k, cache_v, state_C, state_n, state_m,
           page_table, norm_g, w_ada, b_ada, w_in, b_in, g_q, g_k, lam_q, lam_k, gn_a, gn_b, w_out):
    assert w_in.shape[0] == 1, "single-layer model"
    nb, seq, _ = x_prompt.shape
    nreq = x_sample.shape[0]
    assert x_sample.shape[1] == 1

    w = w_in[0]
    bvec = b_in[0]
    gate_lo = N_MAIN
    gb_lo = N_MAIN + 2 * H_B
    wm = w[:, :N_MAIN].astype(BF16)
    wif = jnp.pad(w[:, gate_lo:gb_lo], ((0, 0), (0, GATE_PAD - 2 * H_B))).astype(BF16)
    wgb = w[:, gb_lo:].astype(BF16)
    bm = bvec[None, :N_MAIN]
    bif = jnp.pad(bvec[gate_lo:gb_lo], (0, GATE_PAD - 2 * H_B))[None, :]
    bgb = bvec[None, gb_lo:]
    gq = jnp.tile(g_q[0], 2 * H_A)[None, :]
    gk = jnp.tile(g_k[0], 2 * H_A)[None, :]
    grp = jnp.arange(A_QK) // DK_A
    seg = (grp[:, None] == grp[None, :]).astype(BF16)
    weights = (norm_g[0][None, :], wm, wif, wgb, bm, bif, bgb, gq, gk, seg)
    gna = gn_a[0].reshape(1, W_A)
    gnb = gn_b[0].reshape(1, W_B)
    wo = w_out[0].astype(BF16)

    pad_rows = (-(nb + nreq)) % 8
    c_all = jnp.concatenate([c_prompt, c_sample, jnp.zeros((pad_rows, D_MODEL), F32)], axis=0)
    mod, lam_tile = _ada(c_all, w_ada[0], b_ada[0][None, :], lam_q[0], lam_k[0])
    mod_p = mod[:nb].reshape(nb, 1, 3 * D_MODEL)
    mod_s = mod[nb:nb + nreq]

    xp = x_prompt.reshape(nb * seq, D_MODEL)
    (q, kf, kb, vf, vb, ga, mq, mk, mv, mo, zif, gb) = _proj(xp, mod_p, False, PROJ_TM, weights)
    xs = x_sample.reshape(nreq, D_MODEL)
    (qs, kfs, kbs, vfs, vbs, gas, mqs, mks, mvs, mos, zifs, gbs) = _proj(xs, mod_s, True, nreq, weights)

    mix_a = _attn_prompt(q, kb, vb, ga, gna, lam_tile, nb, seq)
    ck = jnp.transpose(cache_k[0], (0, 2, 3, 4, 1)).reshape(cache_k.shape[1], A_QK, PAGE)
    cv = cache_v[0].reshape(cache_v.shape[1], PAGE * H_A, DV_A)
    r3 = lambda a: a.reshape(nreq, 1, a.shape[-1])
    mix_as, mix_b, c_p, n_p, m_p = _paged_attn_mlstm(
        page_table, r3(qs), r3(kbs), r3(vbs), r3(gas), gna, lam_tile, ck, cv,
        mq, mk, mv, mo, gb, zif, gnb, nb, seq)
    mix_bs, c_s, n_s, m_s = _mlstm_step(mqs, mks, mvs, mos, gbs, zifs, state_C[0],
                                        state_n[0].reshape(nreq, H_B * DK_B), state_m[0], gnb)

    y_p = _out_proj(mix_a, mix_b, wo, xp, mod_p, False, OUT_TM)
    y_s = _out_proj(mix_as.reshape(nreq, W_A), mix_bs, wo, xs, mod_s, True, nreq)

    return (
        y_p.reshape(nb, seq, D_MODEL),
        y_s.reshape(nreq, 1, D_MODEL),
        jnp.transpose(kf.reshape(nb, H_A, 2, DK_A, seq), (0, 4, 1, 2, 3))[None],
        vf.reshape(1, nb, seq, H_A, DV_A),
        kfs.reshape(1, nreq, 1, H_A, 2, DK_A),
        vfs.reshape(1, nreq, 1, H_A, DV_A),
        c_p[None],
        n_p[None],
        m_p[:, :, 0][None],
        c_s[None],
        n_s.reshape(1, nreq, H_B, DK_B),
        m_s[:, :H_B][None],
    )
```

```python
import functools
import math

import jax
import jax.numpy as jnp
from jax import lax
from jax.experimental import pallas as pl
from jax.experimental.pallas import tpu as pltpu

F32 = jnp.float32
BF16 = jnp.bfloat16

D_MODEL = 1024
H_A = 4
DK_A = 64
DV_A = 128
A_QK = H_A * 2 * DK_A
W_A = H_A * DV_A
H_B = 4
DK_B = 128
DV_B = 128
W_B = H_B * DV_B
N_MAIN = 8 * 512
GATE_PAD = 128
PAGE = 128
CHUNK = 128
EPS = 1e-6
NEG = -1e30
LAM_INIT = 0.8 - 0.6 * math.exp(-0.3 * 0)
ALIBI_SLOPES = tuple(2.0 ** (-8.0 * (h + 1) / H_A) for h in range(H_A))
LOG2E = math.log2(math.e)
POS_RADIX = 64

VMEM_LIMIT = 56 * 1024 * 1024

PROJ_TM = 512
OUT_TM = 512
ATT_T = 512
PAGES_PER_STEP = 32
STEP_RB = 8


def _nt_dot(a, b):
    return lax.dot_general(a, b, (((1,), (1,)), ((), ())), preferred_element_type=F32)


def _dot(a, b):
    return jnp.dot(a, b, preferred_element_type=F32)


def _sigmoid(x):
    return 1.0 / (1.0 + jnp.exp(-x))


def _silu(x):
    return x * _sigmoid(x)


def _log_sigmoid(x):
    return jnp.minimum(x, 0.0) - jnp.log1p(jnp.exp(-jnp.abs(x)))


def _split3(a):
    a1 = a.astype(BF16)
    r1 = a - a1.astype(F32)
    a2 = r1.astype(BF16)
    a3 = (r1 - a2.astype(F32)).astype(BF16)
    return a1, a2, a3


def _ada_kernel(c_ref, w_ref, b_ref, lq_ref, lk_ref, mod_ref, lam_ref):
    a = _silu(c_ref[...])
    w = w_ref[...]
    a1, a2, _ = _split3(a)
    w1, w2, _ = _split3(w)
    mod_ref[...] = (_dot(a1, w1) + (_dot(a1, w2) + _dot(a2, w1))) + b_ref[...]

    @pl.when(pl.program_id(0) == 0)
    def _():
        s = jnp.sum(lq_ref[...] * lk_ref[...], axis=1, keepdims=True)
        e = jnp.exp(s)
        lam = e[0:1, :] - e[1:2, :] + LAM_INIT
        lam_ref[...] = jnp.broadcast_to(lam, lam_ref.shape)


def _ada(c_all, w_ada, b_ada, lam_q, lam_k):
    rows = c_all.shape[0]
    nblk = 3
    return pl.pallas_call(
        _ada_kernel,
        grid=(nblk,),
        in_specs=[
            pl.BlockSpec((rows, D_MODEL), lambda j: (0, 0)),
            pl.BlockSpec((D_MODEL, D_MODEL), lambda j: (0, j)),
            pl.BlockSpec((1, D_MODEL), lambda j: (0, j)),
            pl.BlockSpec((2, DK_A), lambda j: (0, 0)),
            pl.BlockSpec((2, DK_A), lambda j: (0, 0)),
        ],
        out_specs=[
            pl.BlockSpec((rows, D_MODEL), lambda j: (0, j)),
            pl.BlockSpec((8, 128), lambda j: (0, 0)),
        ],
        out_shape=[
            jax.ShapeDtypeStruct((rows, 3 * D_MODEL), F32),
            jax.ShapeDtypeStruct((8, 128), F32),
        ],
        compiler_params=pltpu.CompilerParams(
            dimension_semantics=("arbitrary",), vmem_limit_bytes=VMEM_LIMIT),
        name="ada",
    )(c_all, w_ada, b_ada, lam_q, lam_k)


def _proj_kernel(x_ref, shift_ref, scale_ref, ng_ref, wm_ref, wif_ref, wgb_ref,
                 bm_ref, bif_ref, bgb_ref, gq_ref, gk_ref, seg_ref,
                 q_ref, kf_ref, kb_ref, vf_ref, vb_ref, ga_ref,
                 mq_ref, mk_ref, mv_ref, mo_ref, zif_ref, gb_ref, *, k_transposed):
    x = x_ref[...]
    ms = jnp.mean(x * x, axis=-1, keepdims=True)
    h = x * lax.rsqrt(ms + EPS) * ng_ref[...]
    h = h * (1.0 + scale_ref[...]) + shift_ref[...]
    hb = h.astype(BF16)

    def col(j):
        sl = slice(j * 512, (j + 1) * 512)
        return _dot(hb, wm_ref[:, sl]) + bm_ref[:, sl]

    def headnorm(z, g):
        ss = _dot((z * z).astype(BF16), seg_ref[...])
        return z * lax.rsqrt(ss * (1.0 / DK_A) + EPS) * g

    q_ref[...] = (headnorm(col(0), gq_ref[...]) * (DK_A ** -0.5 * LOG2E)).astype(BF16)
    k = headnorm(col(1), gk_ref[...])
    kf_ref[...] = k.T if k_transposed else k
    kb_ref[...] = k.astype(BF16)
    v = col(2)
    for hh in range(H_A):
        vf_ref[pl.ds(hh, x.shape[0], stride=H_A), :] = v[:, hh * DV_A:(hh + 1) * DV_A]
    vb_ref[...] = v.astype(BF16)
    ga_ref[...] = col(3).astype(BF16)
    mq_ref[...] = col(4).astype(BF16)
    mk_ref[...] = (col(5) * (DK_B ** -0.5)).astype(BF16)
    mv_ref[...] = col(6).astype(BF16)
    mo_ref[...] = _sigmoid(col(7)).astype(BF16)
    zif_ref[...] = _dot(hb, wif_ref[...]) + bif_ref[...]
    gb_ref[...] = (_dot(hb, wgb_ref[...]) + bgb_ref[...]).astype(BF16)


def _proj(x2d, mod, per_row, tm, weights):
    (ng, wm, wif, wgb, bm, bif, bgb, gq, gk, seg) = weights
    m_rows = x2d.shape[0]
    nt = m_rows // tm
    if per_row:
        shift_spec = pl.BlockSpec((tm, D_MODEL), lambda i: (i, 0))
        scale_spec = pl.BlockSpec((tm, D_MODEL), lambda i: (i, 1))
    else:
        tiles_per_batch = nt // mod.shape[0]
        shift_spec = pl.BlockSpec((None, 1, D_MODEL), lambda i: (i // tiles_per_batch, 0, 0))
        scale_spec = pl.BlockSpec((None, 1, D_MODEL), lambda i: (i // tiles_per_batch, 0, 1))

    def const(shape):
        return pl.BlockSpec(shape, lambda i: (0, 0))

    def rows(width):
        return pl.BlockSpec((tm, width), lambda i: (i, 0))

    out_dtypes = [BF16, F32, BF16, F32, BF16, BF16, BF16, BF16, BF16, BF16, F32, BF16]
    out_widths = [512, 512, 512, DV_A, 512, 512, 512, 512, 512, 512, GATE_PAD, 512]
    out_rows = [1, 1, 1, H_A, 1, 1, 1, 1, 1, 1, 1, 1]
    out_specs = [pl.BlockSpec((tm * r, w), lambda i: (i, 0)) for w, r in zip(out_widths, out_rows)]
    out_shape = [jax.ShapeDtypeStruct((m_rows * r, w), d)
                 for w, r, d in zip(out_widths, out_rows, out_dtypes)]
    if not per_row:
        out_specs[1] = pl.BlockSpec((None, A_QK, tm),
                                    lambda i: (i // tiles_per_batch, 0, i % tiles_per_batch))
        out_shape[1] = jax.ShapeDtypeStruct((mod.shape[0], A_QK, m_rows // mod.shape[0]), F32)
    return pl.pallas_call(
        functools.partial(_proj_kernel, k_transposed=not per_row),
        grid=(nt,),
        in_specs=[
            rows(D_MODEL), shift_spec, scale_spec, const((1, D_MODEL)),
            const((D_MODEL, N_MAIN)), const((D_MODEL, GATE_PAD)), const((D_MODEL, W_B)),
            const((1, N_MAIN)), const((1, GATE_PAD)), const((1, W_B)),
            const((1, A_QK)), const((1, A_QK)), const((A_QK, A_QK)),
        ],
        out_specs=out_specs,
        out_shape=out_shape,
        compiler_params=pltpu.CompilerParams(
            dimension_semantics=("arbitrary",), vmem_limit_bytes=VMEM_LIMIT),
        name="proj_rows" if per_row else "proj_bcast",
    )(x2d, mod, mod, ng, wm, wif, wgb, bm, bif, bgb, gq, gk, seg)


def _diff_norm_gate(o0, o1, lam, gna, ga):
    d = o0 - lam * o1
    ya = d * lax.rsqrt(jnp.mean(d * d, axis=-1, keepdims=True) + EPS) * gna
    ya = ya * (1.0 - LAM_INIT)
    return ya * _silu(ga)


def _mlstm_out_gate(hh, o, gnb, gb):
    hg = o * hh
    yb = hg * lax.rsqrt(jnp.mean(hg * hg, axis=-1, keepdims=True) + EPS) * gnb
    return yb * _silu(gb)


def _alibi_features(seq):
    pos = jnp.arange(seq, dtype=jnp.int32)
    digits = [(pos // POS_RADIX).astype(F32), (pos % POS_RADIX).astype(F32)]
    ones = jnp.ones((seq,), F32)
    qf, kf = [], []
    for slope in ALIBI_SLOPES:
        pieces = [p.astype(F32) for p in _split3(jnp.float32(slope * LOG2E))]
        qcols, kcols = [], []
        for c in pieces:
            qcols += [POS_RADIX * c * ones, c * ones]
            kcols += digits
        for c in pieces:
            qcols += digits
            kcols += [-POS_RADIX * c * ones, -c * ones]
        pad = ((0, 0), (0, 128 - len(qcols)))
        qf.append(jnp.pad(jnp.stack(qcols, axis=1), pad))
        kf.append(jnp.pad(jnp.stack(kcols, axis=1), pad))
    return jnp.stack(qf).astype(BF16), jnp.stack(kf).astype(BF16)


def _attn_kernel(q_ref, qf_ref, k_ref, kf_ref, v_ref, ga_ref, gna_ref, lam_ref, out_ref, lhs_s, s_s):
    t = ATT_T
    qi = pl.program_id(2)

    q = q_ref[...].astype(F32)
    lane = lax.broadcasted_iota(jnp.int32, (t, 2 * DK_A), 1)
    lhs_s[0:t, 0:128] = jnp.where(lane < DK_A, q, 0.0).astype(BF16)
    lhs_s[t:2 * t, 0:128] = jnp.where(lane >= DK_A, q, 0.0).astype(BF16)
    lhs_s[0:t, 128:256] = qf_ref[...]
    lhs_s[t:2 * t, 128:256] = qf_ref[...]

    krow = lax.broadcasted_iota(jnp.int32, (t, 2 * t), 0)
    qcol = lax.broadcasted_iota(jnp.int32, (t, 2 * t), 1)
    future = krow > jnp.where(qcol >= t, qcol - t, qcol)

    def scores(kj, slot):
        off = pl.multiple_of(kj * t, t)
        kx = jnp.concatenate([k_ref[pl.ds(off, t), :], kf_ref[pl.ds(off, t), :]], axis=1)
        s_s[slot] = _nt_dot(kx, lhs_s[...])

    def consume(kj, slot, carry, masked):
        m, l, acc = carry
        off = pl.multiple_of(kj * t, t)
        s = s_s[slot]
        if masked:
            s = jnp.where(future, NEG, s)
        m_new = jnp.maximum(m, jnp.max(s, axis=0, keepdims=True))
        alpha = jnp.exp2(m - m_new)
        p = jnp.exp2(s - m_new)
        l = alpha * l + jnp.sum(p, axis=0, keepdims=True)
        pv = lax.dot_general(v_ref[pl.ds(off, t), :], p.astype(BF16), (((0,), (0,)), ((), ())),
                             preferred_element_type=F32)
        return m_new, l, alpha * acc + pv

    def pair(i, carry):
        scores(2 * i + 1, 1)
        carry = consume(2 * i, 0, carry, False)
        scores(2 * i + 2, 0)
        return consume(2 * i + 1, 1, carry, False)

    def tail_even(carry):
        return consume(qi, 0, carry, True)

    def tail_odd(carry):
        scores(qi, 1)
        return consume(qi, 1, consume(qi - 1, 0, carry, False), True)

    init = (jnp.full((1, 2 * t), NEG, F32), jnp.zeros((1, 2 * t), F32), jnp.zeros((DV_A, 2 * t), F32))
    scores(0, 0)
    carry = lax.fori_loop(0, qi // 2, pair, init)
    m, l, acc = lax.cond(qi % 2 == 1, tail_odd, tail_even, carry)
    o = (acc / l).T
    lam = lam_ref[0:1, :]
    out = _diff_norm_gate(o[:t], o[t:], lam, gna_ref[...], ga_ref[...].astype(F32))
    out_ref[...] = out.astype(BF16)


def _attn_prompt(q, k, v, ga, gna, lam_tile, nb, seq):
    nq = seq // ATT_T
    qf, kf = _alibi_features(seq)
    return pl.pallas_call(
        _attn_kernel,
        grid=(nb, H_A, nq),
        in_specs=[
            pl.BlockSpec((ATT_T, 128), lambda b, h, i: (b * nq + i, h)),
            pl.BlockSpec((None, ATT_T, 128), lambda b, h, i: (h, i, 0)),
            pl.BlockSpec((seq, 128), lambda b, h, i: (b, h)),
            pl.BlockSpec((None, seq, 128), lambda b, h, i: (h, 0, 0)),
            pl.BlockSpec((seq, 128), lambda b, h, i: (b, h)),
            pl.BlockSpec((ATT_T, 128), lambda b, h, i: (b * nq + i, h)),
            pl.BlockSpec((1, 128), lambda b, h, i: (0, h)),
            pl.BlockSpec((8, 128), lambda b, h, i: (0, 0)),
        ],
        out_specs=pl.BlockSpec((ATT_T, 128), lambda b, h, i: (b * nq + i, h)),
        out_shape=jax.ShapeDtypeStruct((nb * seq, W_A), BF16),
        scratch_shapes=[pltpu.VMEM((2 * ATT_T, 256), BF16),
                        pltpu.VMEM((2, ATT_T, 2 * ATT_T), F32)],
        compiler_params=pltpu.CompilerParams(
            dimension_semantics=("arbitrary", "arbitrary", "arbitrary"),
            vmem_limit_bytes=VMEM_LIMIT),
        name="attn_prompt",
    )(q, qf, k, kf, v, ga, gna, lam_tile)


def _mlstm_chunk(first, heads, gates_cached, q_ref, k_ref, v_ref, o_ref, gb_ref, zif_ref, gnb_ref,
                 y_ref, c_ref, n_ref, m_ref, bcum_s, rows_s):
    L = CHUNK

    x = zif_ref[...]
    row = lax.broadcasted_iota(jnp.int32, (L, L), 0)
    colv = lax.broadcasted_iota(jnp.int32, (L, L), 1)
    causal = row >= colv
    if not gates_cached:
        logf = _log_sigmoid(x)
        tri = jnp.where(causal, 1.0, 0.0).astype(BF16)
        f1, f2, f3 = _split3(logf)
        bcum_s[...] = _dot(tri, f1) + (_dot(tri, f2) + _dot(tri, f3))
        rows_s[0:8, :] = x.T[0:8, :]
        rows_s[8:16, :] = bcum_s[...].T[0:8, :]
    bmat = bcum_s[...]
    xt = rows_s[0:8, :]
    bt = rows_s[8:16, :]

    for h in heads:
        @pl.when(first)
        def _():
            c_ref[h] = jnp.zeros((DV_B, DK_B), F32)
            n_ref[h:h + 1, :] = jnp.zeros((1, DK_B), F32)
            m_ref[h:h + 1, :] = jnp.zeros((1, 128), F32)

        sl = slice(h * 128, (h + 1) * 128)
        qh = q_ref[:, sl]
        kh = k_ref[:, sl]
        vh = v_ref[:, sl]
        icol = x[:, h:h + 1]
        bcol = bmat[:, H_B + h:H_B + h + 1]
        irow = xt[h:h + 1, :]
        brow = bt[H_B + h:H_B + h + 1, :]
        m_prev = m_ref[h:h + 1, 0:1]
        nrow = n_ref[h:h + 1, :]
        c_old = c_ref[h]

        d = jnp.where(causal, (bcol - brow) + irow, NEG)
        inter = bcol + m_prev
        m_t = jnp.maximum(inter, jnp.max(d, axis=1, keepdims=True))
        w_intra = jnp.exp(d - m_t)
        w_inter = jnp.exp(inter - m_t)
        sw = _nt_dot(qh, kh) * w_intra
        num = _dot(sw.astype(BF16), vh) + w_inter * _nt_dot(qh, c_old.astype(BF16))
        den = jnp.sum(sw, axis=1, keepdims=True) \
            + w_inter * jnp.sum(qh.astype(F32) * nrow, axis=1, keepdims=True)
        hh = num / jnp.maximum(jnp.abs(den), jnp.exp(-m_t))
        y = _mlstm_out_gate(hh, o_ref[:, sl].astype(F32), gnb_ref[:, sl], gb_ref[:, sl].astype(F32))
        y_ref[:, sl] = y.astype(BF16)

        m_new = m_t[L - 1:L, :]
        b_last = bcol[L - 1:L, :]
        w_s = jnp.exp(((b_last - bcol) + icol) - m_new)
        decay = jnp.exp((b_last + m_prev) - m_new)
        wv = w_s * vh.astype(F32)
        c_ref[h] = decay * c_old + _dot(wv.T.astype(BF16), kh)
        n_ref[h:h + 1, :] = decay * nrow + jnp.sum(w_s * kh.astype(F32), axis=0, keepdims=True)
        m_ref[h:h + 1, :] = jnp.broadcast_to(m_new, (1, 128))


N_MLSTM_IN = 7
N_MLSTM_OUT = 4
MLSTM_HEAD_GROUPS = ((0, 1), (2, 3))


def _paged_kernel(pt_ref, q_ref, kn_ref, vn_ref, ga_ref, gna_ref, lam_ref, *rest,
                  n_chunk_steps, chunks_per_seq):
    pp = PAGES_PER_STEP
    k_refs = rest[:pp]
    v_refs = rest[pp:2 * pp]
    mlstm_in = rest[2 * pp:2 * pp + N_MLSTM_IN]
    out_ref = rest[2 * pp + N_MLSTM_IN]
    mlstm_out = rest[2 * pp + N_MLSTM_IN + 1:2 * pp + N_MLSTM_IN + 1 + N_MLSTM_OUT]
    kb_s, vb_s, m_s, l_s, acc_s, bcum_s, rows_s = rest[2 * pp + N_MLSTM_IN + 1 + N_MLSTM_OUT:]
    g = pl.program_id(1)
    ng = pl.num_programs(1)
    ntok = pp * PAGE

    step = pl.program_id(0) * ng + g

    ngroups = len(MLSTM_HEAD_GROUPS)
    first = (step // ngroups) % chunks_per_seq == 0
    for gi, heads in enumerate(MLSTM_HEAD_GROUPS):
        @pl.when(jnp.logical_and(step < n_chunk_steps, step % ngroups == gi))
        def _():
            _mlstm_chunk(first, heads, gi > 0, *mlstm_in, *mlstm_out, bcum_s, rows_s)

    @pl.when(g == 0)
    def _():
        m_s[...] = jnp.full_like(m_s, NEG)
        l_s[...] = jnp.zeros_like(l_s)
        acc_s[...] = jnp.zeros_like(acc_s)

    for i in range(pp):
        kb_s[:, i * PAGE:(i + 1) * PAGE] = k_refs[i][...].astype(BF16)
        for h in range(H_A):
            vb_s[h, i * PAGE:(i + 1) * PAGE, :] = v_refs[i][pl.ds(h, PAGE, stride=H_A), :].astype(BF16)

    q = q_ref[...].astype(F32)
    sub = lax.broadcasted_iota(jnp.int32, (8, A_QK), 0)
    lane = lax.broadcasted_iota(jnp.int32, (8, A_QK), 1)
    qbd32 = jnp.where((lane >> 6) == sub, jnp.broadcast_to(q, (8, A_QK)), 0.0)
    qbd = qbd32.astype(BF16)

    j = lax.broadcasted_iota(jnp.int32, (8, 1), 0)
    slope = jnp.where(j < 2, ALIBI_SLOPES[0] * LOG2E,
                      jnp.where(j < 4, ALIBI_SLOPES[1] * LOG2E,
                                jnp.where(j < 6, ALIBI_SLOPES[2] * LOG2E,
                                          ALIBI_SLOPES[3] * LOG2E))).astype(F32)
    kpos = g * ntok + lax.broadcasted_iota(jnp.int32, (1, ntok), 1)
    dist = (ng * ntok - kpos).astype(F32)

    s = _dot(qbd, kb_s[...]) - slope * dist
    m_old = m_s[:, 0:1]
    l_old = l_s[:, 0:1]
    m_new = jnp.maximum(m_old, jnp.max(s, axis=1, keepdims=True))
    alpha = jnp.exp2(m_old - m_new)
    p = jnp.exp2(s - m_new)
    l_new = alpha * l_old + jnp.sum(p, axis=1, keepdims=True)
    pb = p.astype(BF16)
    head_of_row = lax.broadcasted_iota(jnp.int32, (8, DV_A), 0) >> 1
    pv = jnp.zeros((8, DV_A), F32)
    for h in range(H_A):
        pv = jnp.where(head_of_row == h, _dot(pb, vb_s[h]), pv)
    acc = alpha * acc_s[...] + pv
    m_s[...] = jnp.broadcast_to(m_new, m_s.shape)
    l_s[...] = jnp.broadcast_to(l_new, l_s.shape)
    acc_s[...] = acc

    @pl.when(g == ng - 1)
    def _():
        kn = kn_ref[...].astype(F32)
        s_new = jnp.sum(qbd32 * kn, axis=1, keepdims=True)
        m_fin = jnp.maximum(m_new, s_new)
        a2 = jnp.exp2(m_new - m_fin)
        p_new = jnp.exp2(s_new - m_fin)
        l_fin = a2 * l_new + p_new
        vn = jnp.zeros((8, DV_A), F32)
        for h in range(H_A):
            vrow = vn_ref[:, h * DV_A:(h + 1) * DV_A].astype(F32)
            vn = jnp.where(head_of_row == h, jnp.broadcast_to(vrow, (8, DV_A)), vn)
        o = (a2 * acc + p_new * vn) / l_fin
        lam = lam_ref[0:1, :]
        for h in range(H_A):
            sl = slice(h * DV_A, (h + 1) * DV_A)
            out = _diff_norm_gate(o[2 * h:2 * h + 1, :], o[2 * h + 1:2 * h + 2, :], lam,
                                  gna_ref[:, sl], ga_ref[:, sl].astype(F32))
            out_ref[:, sl] = out.astype(BF16)


def _paged_attn_mlstm(page_table, q3, kn3, vn3, ga3, gna, lam_tile, ck, cv,
                      mq, mk, mv, mo, gb, zif, gnb, nb, seq):
    nreq, npages = page_table.shape
    pp = PAGES_PER_STEP
    ng = npages // pp
    nc = seq // CHUNK
    ngroups = len(MLSTM_HEAD_GROUPS)
    n_chunk_steps = nb * nc * ngroups
    assert n_chunk_steps <= nreq * ng, "every mLSTM (chunk, head group) needs a grid step"

    def req(width):
        return pl.BlockSpec((None, 1, width), lambda r, g, pt: (r, 0, 0))

    def page(i):
        return pl.BlockSpec((None, 512, PAGE), lambda r, g, pt: (pt[r, g * pp + i], 0, 0))

    def chunk(r, g):
        return jnp.minimum(r * ng + g, n_chunk_steps - 1) // ngroups

    def crows(width):
        return pl.BlockSpec((CHUNK, width), lambda r, g, pt: (chunk(r, g), 0))

    def cstate(*shape):
        return pl.BlockSpec((None,) + shape, lambda r, g, pt: (chunk(r, g) // nc,) + (0,) * len(shape))

    return pl.pallas_call(
        functools.partial(_paged_kernel, n_chunk_steps=n_chunk_steps, chunks_per_seq=nc),
        grid_spec=pltpu.PrefetchScalarGridSpec(
            num_scalar_prefetch=1,
            grid=(nreq, ng),
            in_specs=[req(512), req(512), req(512), req(512),
                      pl.BlockSpec((1, W_A), lambda r, g, pt: (0, 0)),
                      pl.BlockSpec((8, 128), lambda r, g, pt: (0, 0))]
                     + [page(i) for i in range(pp)] + [page(i) for i in range(pp)]
                     + [crows(512), crows(512), crows(512), crows(512), crows(512), crows(GATE_PAD),
                        pl.BlockSpec((1, W_B), lambda r, g, pt: (0, 0))],
            out_specs=[req(512), crows(512), cstate(H_B, DV_B, DK_B), cstate(H_B, DK_B), cstate(H_B, 128)],
            scratch_shapes=[
                pltpu.VMEM((A_QK, pp * PAGE), BF16),
                pltpu.VMEM((H_A, pp * PAGE, DV_A), BF16),
                pltpu.VMEM((8, 128), F32),
                pltpu.VMEM((8, 128), F32),
                pltpu.VMEM((8, DV_A), F32),
                pltpu.VMEM((CHUNK, 128), F32),
                pltpu.VMEM((16, CHUNK), F32),
            ],
        ),
        out_shape=[
            jax.ShapeDtypeStruct((nreq, 1, W_A), BF16),
            jax.ShapeDtypeStruct((nb * seq, W_B), BF16),
            jax.ShapeDtypeStruct((nb, H_B, DV_B, DK_B), F32),
            jax.ShapeDtypeStruct((nb, H_B, DK_B), F32),
            jax.ShapeDtypeStruct((nb, H_B, 128), F32),
        ],
        compiler_params=pltpu.CompilerParams(
            dimension_semantics=("arbitrary", "arbitrary"), vmem_limit_bytes=VMEM_LIMIT),
        name="paged_attn_mlstm",
    )(page_table, q3, kn3, vn3, ga3, gna, lam_tile, *([ck] * pp), *([cv] * pp),
      mq, mk, mv, mo, gb, zif, gnb)


def _mstep_kernel(q_ref, k_ref, v_ref, o_ref, gb_ref, zif_ref, c_ref, n_ref, m_ref, gnb_ref,
                  y_ref, co_ref, no_ref, mo_ref):
    rb = STEP_RB
    x = zif_ref[...]
    sub = lax.broadcasted_iota(jnp.int32, (rb, 128), 0)
    lane = lax.broadcasted_iota(jnp.int32, (rb, 128), 1)
    m_out = jnp.zeros((rb, 128), F32)
    for h in range(H_B):
        sl = slice(h * 128, (h + 1) * 128)
        qb = q_ref[:, sl]
        qh = qb.astype(F32)
        kh = k_ref[:, sl].astype(F32)
        vh = v_ref[:, sl].astype(F32)
        i_c = x[:, h:h + 1]
        b = _log_sigmoid(x[:, H_B + h:H_B + h + 1])
        m_prev = m_ref[:, h:h + 1]
        inter = b + m_prev
        m_t = jnp.maximum(inter, (b - b) + i_c)
        w_intra = jnp.exp(((b - b) + i_c) - m_t)
        w_inter = jnp.exp(inter - m_t)
        sw = jnp.sum(qh * kh, axis=1, keepdims=True) * w_intra
        cq = jnp.zeros((rb, 128), F32)
        for r in range(rb):
            res = _nt_dot(qb, c_ref[r, h].astype(BF16))
            cq = jnp.where(sub == r, res, cq)
        nh = n_ref[:, sl]
        num = sw * vh + w_inter * cq
        den = sw + w_inter * jnp.sum(nh * qh, axis=1, keepdims=True)
        hh = num / jnp.maximum(jnp.abs(den), jnp.exp(-m_t))
        y = _mlstm_out_gate(hh, o_ref[:, sl].astype(F32), gnb_ref[:, sl], gb_ref[:, sl].astype(F32))
        y_ref[:, sl] = y.astype(BF16)

        w_s = jnp.exp(((b - b) + i_c) - m_t)
        decay = jnp.exp((b + m_prev) - m_t)
        wv = w_s * vh
        for r in range(rb):
            vcol = jnp.broadcast_to(wv[r:r + 1, :], (DV_B, DK_B)).T
            co_ref[r, h] = decay[r:r + 1, :] * c_ref[r, h] + vcol * kh[r:r + 1, :]
        no_ref[:, sl] = decay * nh + w_s * kh
        m_out = jnp.where(lane == h, m_t, m_out)
    mo_ref[...] = m_out


def _mlstm_step(mq, mk, mv, mo, gb, zif, c0, n0, m0, gnb):
    nreq = mq.shape[0]
    rb = STEP_RB

    def rows(width):
        return pl.BlockSpec((rb, width), lambda i: (i, 0))

    cspec = pl.BlockSpec((rb, H_B, DV_B, DK_B), lambda i: (i, 0, 0, 0))
    return pl.pallas_call(
        _mstep_kernel,
        grid=(nreq // rb,),
        in_specs=[rows(512), rows(512), rows(512), rows(512), rows(512), rows(GATE_PAD),
                  cspec, rows(512), rows(H_B), pl.BlockSpec((1, W_B), lambda i: (0, 0))],
        out_specs=[rows(512), cspec, rows(512), rows(128)],
        out_shape=[
            jax.ShapeDtypeStruct((nreq, W_B), BF16),
            jax.ShapeDtypeStruct((nreq, H_B, DV_B, DK_B), F32),
            jax.ShapeDtypeStruct((nreq, H_B * DK_B), F32),
            jax.ShapeDtypeStruct((nreq, 128), F32),
        ],
        compiler_params=pltpu.CompilerParams(
            dimension_semantics=("arbitrary",), vmem_limit_bytes=VMEM_LIMIT),
        name="mlstm_step",
    )(mq, mk, mv, mo, gb, zif, c0, n0, m0, gnb)


def _out_kernel(ma_ref, mb_ref, w_ref, x_ref, gate_ref, y_ref):
    acc = _dot(ma_ref[...], w_ref[0:W_A, :]) + _dot(mb_ref[...], w_ref[W_A:W_A + W_B, :])
    y_ref[...] = x_ref[...] + gate_ref[...] * acc


def _out_proj(mix_a, mix_b, w_out, x2d, mod, per_row, tm):
    m_rows = x2d.shape[0]
    nt = m_rows // tm
    if per_row:
        gate_spec = pl.BlockSpec((tm, D_MODEL), lambda i: (i, 2))
    else:
        tiles_per_batch = nt // mod.shape[0]
        gate_spec = pl.BlockSpec((None, 1, D_MODEL), lambda i: (i // tiles_per_batch, 0, 2))
    return pl.pallas_call(
        _out_kernel,
        grid=(nt,),
        in_specs=[
            pl.BlockSpec((tm, W_A), lambda i: (i, 0)),
            pl.BlockSpec((tm, W_B), lambda i: (i, 0)),
            pl.BlockSpec((W_A + W_B, D_MODEL), lambda i: (0, 0)),
            pl.BlockSpec((tm, D_MODEL), lambda i: (i, 0)),
            gate_spec,
        ],
        out_specs=pl.BlockSpec((tm, D_MODEL), lambda i: (i, 0)),
        out_shape=jax.ShapeDtypeStruct((m_rows, D_MODEL), F32),
        compiler_params=pltpu.CompilerParams(
            dimension_semantics=("arbitrary",), vmem_limit_bytes=VMEM_LIMIT),
        name="out_rows" if per_row else "out_bcast",
    )(mix_a, mix_b, w_out, x2d, mod)


def kernel(x_prompt, x_sample, c_prompt, c_sample, cache_k, cache_v, state_C, state_n, state_m,
           page_table, norm_g, w_ada, b_ada, w_in, b_in, g_q, g_k, lam_q, lam_k, gn_a, gn_b, w_out):
    assert w_in.shape[0] == 1, "single-layer model"
    nb, seq, _ = x_prompt.shape
    nreq = x_sample.shape[0]
    assert x_sample.shape[1] == 1

    w = w_in[0]
    bvec = b_in[0]
    gate_lo = N_MAIN
    gb_lo = N_MAIN + 2 * H_B
    wm = w[:, :N_MAIN].astype(BF16)
    wif = jnp.pad(w[:, gate_lo:gb_lo], ((0, 0), (0, GATE_PAD - 2 * H_B))).astype(BF16)
    wgb = w[:, gb_lo:].astype(BF16)
    bm = bvec[None, :N_MAIN]
    bif = jnp.pad(bvec[gate_lo:gb_lo], (0, GATE_PAD - 2 * H_B))[None, :]
    bgb = bvec[None, gb_lo:]
    gq = jnp.tile(g_q[0], 2 * H_A)[None, :]
    gk = jnp.tile(g_k[0], 2 * H_A)[None, :]
    grp = jnp.arange(A_QK) // DK_A
    seg = (grp[:, None] == grp[None, :]).astype(BF16)
    weights = (norm_g[0][None, :], wm, wif, wgb, bm, bif, bgb, gq, gk, seg)
    gna = gn_a[0].reshape(1, W_A)
    gnb = gn_b[0].reshape(1, W_B)
    wo = w_out[0].astype(BF16)

    pad_rows = (-(nb + nreq)) % 8
    c_all = jnp.concatenate([c_prompt, c_sample, jnp.zeros((pad_rows, D_MODEL), F32)], axis=0)
    mod, lam_tile = _ada(c_all, w_ada[0], b_ada[0][None, :], lam_q[0], lam_k[0])
    mod_p = mod[:nb].reshape(nb, 1, 3 * D_MODEL)
    mod_s = mod[nb:nb + nreq]

    xp = x_prompt.reshape(nb * seq, D_MODEL)
    (q, kf, kb, vf, vb, ga, mq, mk, mv, mo, zif, gb) = _proj(xp, mod_p, False, PROJ_TM, weights)
    xs = x_sample.reshape(nreq, D_MODEL)
    (qs, kfs, kbs, vfs, vbs, gas, mqs, mks, mvs, mos, zifs, gbs) = _proj(xs, mod_s, True, nreq, weights)

    mix_a = _attn_prompt(q, kb, vb, ga, gna, lam_tile, nb, seq)
    ck = jnp.transpose(cache_k[0], (0, 2, 3, 4, 1)).reshape(cache_k.shape[1], A_QK, PAGE)
    cv = cache_v[0].reshape(cache_v.shape[1], PAGE * H_A, DV_A)
    r3 = lambda a: a.reshape(nreq, 1, a.shape[-1])
    mix_as, mix_b, c_p, n_p, m_p = _paged_attn_mlstm(
        page_table, r3(qs), r3(kbs), r3(vbs), r3(gas), gna, lam_tile, ck, cv,
        mq, mk, mv, mo, gb, zif, gnb, nb, seq)
    mix_bs, c_s, n_s, m_s = _mlstm_step(mqs, mks, mvs, mos, gbs, zifs, state_C[0],
                                        state_n[0].reshape(nreq, H_B * DK_B), state_m[0], gnb)

    y_p = _out_proj(mix_a, mix_b, wo, xp, mod_p, False, OUT_TM)
    y_s = _out_proj(mix_as.reshape(nreq, W_A), mix_bs, wo, xs, mod_s, True, nreq)

    return (
        y_p.reshape(nb, seq, D_MODEL),
        y_s.reshape(nreq, 1, D_MODEL),
        jnp.transpose(kf.reshape(nb, H_A, 2, DK_A, seq), (0, 4, 1, 2, 3))[None],
        vf.reshape(1, nb, seq, H_A, DV_A),
        kfs.reshape(1, nreq, 1, H_A, 2, DK_A),
        vfs.reshape(1, nreq, 1, H_A, DV_A),
        c_p[None],
        n_p[None],
        m_p[:, :, 0][None],
        c_s[None],
        n_s.reshape(1, nreq, H_B, DK_B),
        m_s[:, :H_B][None],
    )
```

```python
import functools
import math

import jax
import jax.numpy as jnp
from jax import lax
from jax.experimental import pallas as pl
from jax.experimental.pallas import tpu as pltpu

F32 = jnp.float32
BF16 = jnp.bfloat16

D_MODEL = 1024
H_A = 4
assert H_A & (H_A - 1) == 0
DK_A = 64
DV_A = 128
A_QK = H_A * 2 * DK_A
W_A = H_A * DV_A
H_B = 4
DK_B = 128
DV_B = 128
W_B = H_B * DV_B
N_MAIN = 8 * 512
GATE_PAD = 128
PAGE = 128
CHUNK = 128
EPS = 1e-6
NEG = -1e30
LAM_INIT = 0.8 - 0.6 * math.exp(-0.3 * 0)
ALIBI_SLOPES = tuple(2.0 ** (-8.0 * (h + 1) / H_A) for h in range(H_A))
LOG2E = math.log2(math.e)
POS_RADIX = 64

VMEM_LIMIT = 56 * 1024 * 1024

PROJ_TM = 512
OUT_TM = 512
ATT_T = 512
PAGES_PER_STEP = 32
STEP_RB = 8


def _nt_dot(a, b):
    return lax.dot_general(a, b, (((1,), (1,)), ((), ())), preferred_element_type=F32)


def _dot(a, b):
    return jnp.dot(a, b, preferred_element_type=F32)


def _sigmoid(x):
    return 1.0 / (1.0 + jnp.exp(-x))


def _silu(x):
    return x * _sigmoid(x)


def _log_sigmoid(x):
    return jnp.minimum(x, 0.0) - jnp.log1p(jnp.exp(-jnp.abs(x)))


def _split3(a):
    a1 = a.astype(BF16)
    r1 = a - a1.astype(F32)
    a2 = r1.astype(BF16)
    a3 = (r1 - a2.astype(F32)).astype(BF16)
    return a1, a2, a3


def _ada_kernel(c_ref, w_ref, b_ref, lq_ref, lk_ref, mod_ref, lam_ref):
    a = _silu(c_ref[...])
    w = w_ref[...]
    a1, a2, _ = _split3(a)
    w1, w2, _ = _split3(w)
    mod_ref[...] = (_dot(a1, w1) + (_dot(a1, w2) + _dot(a2, w1))) + b_ref[...]

    @pl.when(pl.program_id(0) == 0)
    def _():
        s = jnp.sum(lq_ref[...] * lk_ref[...], axis=1, keepdims=True)
        e = jnp.exp(s)
        lam = e[0:1, :] - e[1:2, :] + LAM_INIT
        lam_ref[...] = jnp.broadcast_to(lam, lam_ref.shape)


def _ada(c_all, w_ada, b_ada, lam_q, lam_k):
    rows = c_all.shape[0]
    nblk = 3
    return pl.pallas_call(
        _ada_kernel,
        grid=(nblk,),
        in_specs=[
            pl.BlockSpec((rows, D_MODEL), lambda j: (0, 0)),
            pl.BlockSpec((D_MODEL, D_MODEL), lambda j: (0, j)),
            pl.BlockSpec((1, D_MODEL), lambda j: (0, j)),
            pl.BlockSpec((2, DK_A), lambda j: (0, 0)),
            pl.BlockSpec((2, DK_A), lambda j: (0, 0)),
        ],
        out_specs=[
            pl.BlockSpec((rows, D_MODEL), lambda j: (0, j)),
            pl.BlockSpec((8, 128), lambda j: (0, 0)),
        ],
        out_shape=[
            jax.ShapeDtypeStruct((rows, 3 * D_MODEL), F32),
            jax.ShapeDtypeStruct((8, 128), F32),
        ],
        compiler_params=pltpu.CompilerParams(
            dimension_semantics=("arbitrary",), vmem_limit_bytes=VMEM_LIMIT),
        name="ada",
    )(c_all, w_ada, b_ada, lam_q, lam_k)


def _proj_kernel(x_ref, shift_ref, scale_ref, ng_ref, wm_ref, wif_ref, wgb_ref,
                 bm_ref, bif_ref, bgb_ref, gq_ref, gk_ref, seg_ref,
                 q_ref, kf_ref, kb_ref, vf_ref, vb_ref, ga_ref,
                 mq_ref, mk_ref, mv_ref, mo_ref, zif_ref, gb_ref, *, k_transposed):
    x = x_ref[...]
    ms = jnp.mean(x * x, axis=-1, keepdims=True)
    h = x * lax.rsqrt(ms + EPS) * ng_ref[...]
    h = h * (1.0 + scale_ref[...]) + shift_ref[...]
    hb = h.astype(BF16)

    def col(j):
        sl = slice(j * 512, (j + 1) * 512)
        return _dot(hb, wm_ref[:, sl]) + bm_ref[:, sl]

    def headnorm(z, g):
        ss = _dot((z * z).astype(BF16), seg_ref[...])
        return z * lax.rsqrt(ss * (1.0 / DK_A) + EPS) * g

    q_ref[...] = (headnorm(col(0), gq_ref[...]) * (DK_A ** -0.5 * LOG2E)).astype(BF16)
    k = headnorm(col(1), gk_ref[...])
    kf_ref[...] = k.T if k_transposed else k
    kb_ref[...] = k.astype(BF16)
    v = col(2)
    for hh in range(H_A):
        vf_ref[pl.ds(hh, x.shape[0], stride=H_A), :] = v[:, hh * DV_A:(hh + 1) * DV_A]
    vb_ref[...] = v.astype(BF16)
    ga_ref[...] = col(3).astype(BF16)
    mq_ref[...] = col(4).astype(BF16)
    mk_ref[...] = (col(5) * (DK_B ** -0.5)).astype(BF16)
    mv_ref[...] = col(6).astype(BF16)
    mo_ref[...] = _sigmoid(col(7)).astype(BF16)
    zif_ref[...] = _dot(hb, wif_ref[...]) + bif_ref[...]
    gb_ref[...] = (_dot(hb, wgb_ref[...]) + bgb_ref[...]).astype(BF16)


def _proj(x2d, mod, per_row, tm, weights):
    (ng, wm, wif, wgb, bm, bif, bgb, gq, gk, seg) = weights
    m_rows = x2d.shape[0]
    nt = m_rows // tm
    if per_row:
        shift_spec = pl.BlockSpec((tm, D_MODEL), lambda i: (i, 0))
        scale_spec = pl.BlockSpec((tm, D_MODEL), lambda i: (i, 1))
    else:
        tiles_per_batch = nt // mod.shape[0]
        shift_spec = pl.BlockSpec((None, 1, D_MODEL), lambda i: (i // tiles_per_batch, 0, 0))
        scale_spec = pl.BlockSpec((None, 1, D_MODEL), lambda i: (i // tiles_per_batch, 0, 1))

    def const(shape):
        return pl.BlockSpec(shape, lambda i: (0, 0))

    def rows(width):
        return pl.BlockSpec((tm, width), lambda i: (i, 0))

    out_dtypes = [BF16, F32, BF16, F32, BF16, BF16, BF16, BF16, BF16, BF16, F32, BF16]
    out_widths = [512, 512, 512, DV_A, 512, 512, 512, 512, 512, 512, GATE_PAD, 512]
    out_rows = [1, 1, 1, H_A, 1, 1, 1, 1, 1, 1, 1, 1]
    out_specs = [pl.BlockSpec((tm * r, w), lambda i: (i, 0)) for w, r in zip(out_widths, out_rows)]
    out_shape = [jax.ShapeDtypeStruct((m_rows * r, w), d)
                 for w, r, d in zip(out_widths, out_rows, out_dtypes)]
    if not per_row:
        out_specs[1] = pl.BlockSpec((None, A_QK, tm),
                                    lambda i: (i // tiles_per_batch, 0, i % tiles_per_batch))
        out_shape[1] = jax.ShapeDtypeStruct((mod.shape[0], A_QK, m_rows // mod.shape[0]), F32)
    return pl.pallas_call(
        functools.partial(_proj_kernel, k_transposed=not per_row),
        grid=(nt,),
        in_specs=[
            rows(D_MODEL), shift_spec, scale_spec, const((1, D_MODEL)),
            const((D_MODEL, N_MAIN)), const((D_MODEL, GATE_PAD)), const((D_MODEL, W_B)),
            const((1, N_MAIN)), const((1, GATE_PAD)), const((1, W_B)),
            const((1, A_QK)), const((1, A_QK)), const((A_QK, A_QK)),
        ],
        out_specs=out_specs,
        out_shape=out_shape,
        compiler_params=pltpu.CompilerParams(
            dimension_semantics=("arbitrary",), vmem_limit_bytes=VMEM_LIMIT),
        name="proj_rows" if per_row else "proj_bcast",
    )(x2d, mod, mod, ng, wm, wif, wgb, bm, bif, bgb, gq, gk, seg)


def _diff_norm_gate(o0, o1, lam, gna, ga):
    d = o0 - lam * o1
    ya = d * lax.rsqrt(jnp.mean(d * d, axis=-1, keepdims=True) + EPS) * gna
    ya = ya * (1.0 - LAM_INIT)
    return ya * _silu(ga)


def _mlstm_out_gate(hh, o, gnb, gb):
    hg = o * hh
    yb = hg * lax.rsqrt(jnp.mean(hg * hg, axis=-1, keepdims=True) + EPS) * gnb
    return yb * _silu(gb)


def _alibi_features(seq):
    pos = jnp.arange(seq, dtype=jnp.int32)
    digits = [(pos // POS_RADIX).astype(F32), (pos % POS_RADIX).astype(F32)]
    ones = jnp.ones((seq,), F32)
    qf, kf = [], []
    for slope in ALIBI_SLOPES:
        pieces = [p.astype(F32) for p in _split3(jnp.float32(slope * LOG2E))]
        qcols, kcols = [], []
        for c in pieces:
            qcols += [POS_RADIX * c * ones, c * ones]
            kcols += digits
        for c in pieces:
            qcols += digits
            kcols += [-POS_RADIX * c * ones, -c * ones]
        pad = ((0, 0), (0, 128 - len(qcols)))
        qf.append(jnp.pad(jnp.stack(qcols, axis=1), pad))
        kf.append(jnp.pad(jnp.stack(kcols, axis=1), pad))
    return jnp.stack(qf).astype(BF16), jnp.stack(kf).astype(BF16)


def _attn_kernel(q_ref, qf_ref, k_ref, kf_ref, v_ref, ga_ref, gna_ref, lam_ref, out_ref, lhs_s, s_s):
    t = ATT_T
    qi = pl.program_id(2)

    q = q_ref[...].astype(F32)
    lane = lax.broadcasted_iota(jnp.int32, (t, 2 * DK_A), 1)
    lhs_s[0:t, 0:128] = jnp.where(lane < DK_A, q, 0.0).astype(BF16)
    lhs_s[t:2 * t, 0:128] = jnp.where(lane >= DK_A, q, 0.0).astype(BF16)
    lhs_s[0:t, 128:256] = qf_ref[...]
    lhs_s[t:2 * t, 128:256] = qf_ref[...]

    krow = lax.broadcasted_iota(jnp.int32, (t, 2 * t), 0)
    qcol = lax.broadcasted_iota(jnp.int32, (t, 2 * t), 1)
    future = krow > jnp.where(qcol >= t, qcol - t, qcol)

    def scores(kj, slot):
        off = pl.multiple_of(kj * t, t)
        kx = jnp.concatenate([k_ref[pl.ds(off, t), :], kf_ref[pl.ds(off, t), :]], axis=1)
        s_s[slot] = _nt_dot(kx, lhs_s[...])

    def consume(kj, slot, carry, masked):
        m, l, acc = carry
        off = pl.multiple_of(kj * t, t)
        s = s_s[slot]
        if masked:
            s = jnp.where(future, NEG, s)
        m_new = jnp.maximum(m, jnp.max(s, axis=0, keepdims=True))
        alpha = jnp.exp2(m - m_new)
        p = jnp.exp2(s - m_new)
        l = alpha * l + jnp.sum(p, axis=0, keepdims=True)
        pv = lax.dot_general(v_ref[pl.ds(off, t), :], p.astype(BF16), (((0,), (0,)), ((), ())),
                             preferred_element_type=F32)
        return m_new, l, alpha * acc + pv

    def pair(i, carry):
        scores(2 * i + 1, 1)
        carry = consume(2 * i, 0, carry, False)
        scores(2 * i + 2, 0)
        return consume(2 * i + 1, 1, carry, False)

    def tail_even(carry):
        return consume(qi, 0, carry, True)

    def tail_odd(carry):
        scores(qi, 1)
        return consume(qi, 1, consume(qi - 1, 0, carry, False), True)

    init = (jnp.full((1, 2 * t), NEG, F32), jnp.zeros((1, 2 * t), F32), jnp.zeros((DV_A, 2 * t), F32))
    scores(0, 0)
    carry = lax.fori_loop(0, qi // 2, pair, init)
    m, l, acc = lax.cond(qi % 2 == 1, tail_odd, tail_even, carry)
    o = (acc / l).T
    lam = lam_ref[0:1, :]
    out = _diff_norm_gate(o[:t], o[t:], lam, gna_ref[...], ga_ref[...].astype(F32))
    out_ref[...] = out.astype(BF16)


def _attn_prompt(q, k, v, ga, gna, lam_tile, nb, seq):
    nq = seq // ATT_T
    qf, kf = _alibi_features(seq)
    return pl.pallas_call(
        _attn_kernel,
        grid=(nb, H_A, nq),
        in_specs=[
            pl.BlockSpec((ATT_T, 128), lambda b, h, i: (b * nq + i, h)),
            pl.BlockSpec((None, ATT_T, 128), lambda b, h, i: (h, i, 0)),
            pl.BlockSpec((seq, 128), lambda b, h, i: (b, h)),
            pl.BlockSpec((None, seq, 128), lambda b, h, i: (h, 0, 0)),
            pl.BlockSpec((seq, 128), lambda b, h, i: (b, h)),
            pl.BlockSpec((ATT_T, 128), lambda b, h, i: (b * nq + i, h)),
            pl.BlockSpec((1, 128), lambda b, h, i: (0, h)),
            pl.BlockSpec((8, 128), lambda b, h, i: (0, 0)),
        ],
        out_specs=pl.BlockSpec((ATT_T, 128), lambda b, h, i: (b * nq + i, h)),
        out_shape=jax.ShapeDtypeStruct((nb * seq, W_A), BF16),
        scratch_shapes=[pltpu.VMEM((2 * ATT_T, 256), BF16),
                        pltpu.VMEM((2, ATT_T, 2 * ATT_T), F32)],
        compiler_params=pltpu.CompilerParams(
            dimension_semantics=("arbitrary", "arbitrary", "arbitrary"),
            vmem_limit_bytes=VMEM_LIMIT),
        name="attn_prompt",
    )(q, qf, k, kf, v, ga, gna, lam_tile)


def _mlstm_chunk(first, heads, gates_cached, q_ref, k_ref, v_ref, o_ref, gb_ref, zif_ref, gnb_ref,
                 y_ref, c_ref, n_ref, m_ref, bcum_s, rows_s):
    L = CHUNK

    x = zif_ref[...]
    row = lax.broadcasted_iota(jnp.int32, (L, L), 0)
    colv = lax.broadcasted_iota(jnp.int32, (L, L), 1)
    causal = row >= colv
    if not gates_cached:
        logf = _log_sigmoid(x)
        tri = jnp.where(causal, 1.0, 0.0).astype(BF16)
        f1, f2, f3 = _split3(logf)
        bcum_s[...] = _dot(tri, f1) + (_dot(tri, f2) + _dot(tri, f3))
        rows_s[0:8, :] = x.T[0:8, :]
        rows_s[8:16, :] = bcum_s[...].T[0:8, :]
    bmat = bcum_s[...]
    xt = rows_s[0:8, :]
    bt = rows_s[8:16, :]

    for h in heads:
        @pl.when(first)
        def _():
            c_ref[h] = jnp.zeros((DV_B, DK_B), F32)
            n_ref[h:h + 1, :] = jnp.zeros((1, DK_B), F32)
            m_ref[h:h + 1, :] = jnp.zeros((1, 128), F32)

        sl = slice(h * 128, (h + 1) * 128)
        qh = q_ref[:, sl]
        kh = k_ref[:, sl]
        vh = v_ref[:, sl]
        icol = x[:, h:h + 1]
        bcol = bmat[:, H_B + h:H_B + h + 1]
        irow = xt[h:h + 1, :]
        brow = bt[H_B + h:H_B + h + 1, :]
        m_prev = m_ref[h:h + 1, 0:1]
        nrow = n_ref[h:h + 1, :]
        c_old = c_ref[h]

        d = jnp.where(causal, (bcol - brow) + irow, NEG)
        inter = bcol + m_prev
        m_t = jnp.maximum(inter, jnp.max(d, axis=1, keepdims=True))
        w_intra = jnp.exp(d - m_t)
        w_inter = jnp.exp(inter - m_t)
        sw = _nt_dot(qh, kh) * w_intra
        num = _dot(sw.astype(BF16), vh) + w_inter * _nt_dot(qh, c_old.astype(BF16))
        den = jnp.sum(sw, axis=1, keepdims=True) \
            + w_inter * jnp.sum(qh.astype(F32) * nrow, axis=1, keepdims=True)
        hh = num / jnp.maximum(jnp.abs(den), jnp.exp(-m_t))
        y = _mlstm_out_gate(hh, o_ref[:, sl].astype(F32), gnb_ref[:, sl], gb_ref[:, sl].astype(F32))
        y_ref[:, sl] = y.astype(BF16)

        m_new = m_t[L - 1:L, :]
        b_last = bcol[L - 1:L, :]
        w_s = jnp.exp(((b_last - bcol) + icol) - m_new)
        decay = jnp.exp((b_last + m_prev) - m_new)
        wv = w_s * vh.astype(F32)
        c_ref[h] = decay * c_old + _dot(wv.T.astype(BF16), kh)
        n_ref[h:h + 1, :] = decay * nrow + jnp.sum(w_s * kh.astype(F32), axis=0, keepdims=True)
        m_ref[h:h + 1, :] = jnp.broadcast_to(m_new, (1, 128))


N_MLSTM_IN = 7
N_MLSTM_OUT = 4
MLSTM_HEAD_GROUPS = ((0, 1), (2, 3))


def _paged_kernel(pt_ref, q_ref, kn_ref, vn_ref, ga_ref, gna_ref, lam_ref, exp_ref, *rest,
                  n_chunk_steps, chunks_per_seq):
    pp = PAGES_PER_STEP
    k_refs = rest[:pp]
    v_refs = rest[pp:2 * pp]
    mlstm_in = rest[2 * pp:2 * pp + N_MLSTM_IN]
    out_ref = rest[2 * pp + N_MLSTM_IN]
    mlstm_out = rest[2 * pp + N_MLSTM_IN + 1:2 * pp + N_MLSTM_IN + 1 + N_MLSTM_OUT]
    kb_s, vb_s, m_s, l_s, acc_s, bcum_s, rows_s = rest[2 * pp + N_MLSTM_IN + 1 + N_MLSTM_OUT:]
    g = pl.program_id(1)
    ng = pl.num_programs(1)
    ntok = pp * PAGE

    step = pl.program_id(0) * ng + g

    ngroups = len(MLSTM_HEAD_GROUPS)
    first = (step // ngroups) % chunks_per_seq == 0
    for gi, heads in enumerate(MLSTM_HEAD_GROUPS):
        @pl.when(jnp.logical_and(step < n_chunk_steps, step % ngroups == gi))
        def _():
            _mlstm_chunk(first, heads, gi > 0, *mlstm_in, *mlstm_out, bcum_s, rows_s)

    @pl.when(g == 0)
    def _():
        m_s[...] = jnp.full_like(m_s, NEG)
        l_s[...] = jnp.zeros_like(l_s)
        acc_s[...] = jnp.zeros_like(acc_s)

    vrows = PAGE * H_A
    for i in range(pp):
        kb_s[:, i * PAGE:(i + 1) * PAGE] = k_refs[i][...].astype(BF16)
        vb_s[i * vrows:(i + 1) * vrows, :] = v_refs[i][...].astype(BF16)

    q = q_ref[...].astype(F32)
    sub = lax.broadcasted_iota(jnp.int32, (8, A_QK), 0)
    lane = lax.broadcasted_iota(jnp.int32, (8, A_QK), 1)
    qbd32 = jnp.where((lane >> 6) == sub, jnp.broadcast_to(q, (8, A_QK)), 0.0)
    qbd = qbd32.astype(BF16)

    j = lax.broadcasted_iota(jnp.int32, (8, 1), 0)
    slope = jnp.where(j < 2, ALIBI_SLOPES[0] * LOG2E,
                      jnp.where(j < 4, ALIBI_SLOPES[1] * LOG2E,
                                jnp.where(j < 6, ALIBI_SLOPES[2] * LOG2E,
                                          ALIBI_SLOPES[3] * LOG2E))).astype(F32)
    kpos = g * ntok + lax.broadcasted_iota(jnp.int32, (1, ntok), 1)
    dist = (ng * ntok - kpos).astype(F32)

    s = _dot(qbd, kb_s[...]) - slope * dist
    m_old = m_s[:, 0:1]
    l_old = l_s[:, 0:1]
    m_new = jnp.maximum(m_old, jnp.max(s, axis=1, keepdims=True))
    alpha = jnp.exp2(m_old - m_new)
    p = jnp.exp2(s - m_new)
    l_new = alpha * l_old + jnp.sum(p, axis=1, keepdims=True)
    pst = jnp.concatenate([p[:, i * PAGE:(i + 1) * PAGE] for i in range(pp)], axis=0)
    spread = _dot(pst.astype(BF16), exp_ref[...])
    own_head = (lax.broadcasted_iota(jnp.int32, (8, vrows), 1) & (H_A - 1)) \
        == (lax.broadcasted_iota(jnp.int32, (8, vrows), 0) >> 1)
    p4 = jnp.concatenate([jnp.where(own_head, spread[8 * i:8 * (i + 1), :], 0.0) for i in range(pp)],
                         axis=1)
    acc = alpha * acc_s[...] + _dot(p4.astype(BF16), vb_s[...])
    head_of_row = lax.broadcasted_iota(jnp.int32, (8, DV_A), 0) >> 1
    m_s[...] = jnp.broadcast_to(m_new, m_s.shape)
    l_s[...] = jnp.broadcast_to(l_new, l_s.shape)
    acc_s[...] = acc

    @pl.when(g == ng - 1)
    def _():
        kn = kn_ref[...].astype(F32)
        s_new = jnp.sum(qbd32 * kn, axis=1, keepdims=True)
        m_fin = jnp.maximum(m_new, s_new)
        a2 = jnp.exp2(m_new - m_fin)
        p_new = jnp.exp2(s_new - m_fin)
        l_fin = a2 * l_new + p_new
        vn = jnp.zeros((8, DV_A), F32)
        for h in range(H_A):
            vrow = vn_ref[:, h * DV_A:(h + 1) * DV_A].astype(F32)
            vn = jnp.where(head_of_row == h, jnp.broadcast_to(vrow, (8, DV_A)), vn)
        o = (a2 * acc + p_new * vn) / l_fin
        lam = lam_ref[0:1, :]
        for h in range(H_A):
            sl = slice(h * DV_A, (h + 1) * DV_A)
            out = _diff_norm_gate(o[2 * h:2 * h + 1, :], o[2 * h + 1:2 * h + 2, :], lam,
                                  gna_ref[:, sl], ga_ref[:, sl].astype(F32))
            out_ref[:, sl] = out.astype(BF16)


def _paged_attn_mlstm(page_table, q3, kn3, vn3, ga3, gna, lam_tile, ck, cv,
                      mq, mk, mv, mo, gb, zif, gnb, nb, seq):
    nreq, npages = page_table.shape
    pp = PAGES_PER_STEP
    ng = npages // pp
    nc = seq // CHUNK
    ngroups = len(MLSTM_HEAD_GROUPS)
    n_chunk_steps = nb * nc * ngroups
    assert n_chunk_steps <= nreq * ng, "every mLSTM (chunk, head group) needs a grid step"

    expand = (jnp.arange(PAGE * H_A)[None, :] // H_A == jnp.arange(PAGE)[:, None]).astype(BF16)

    def req(width):
        return pl.BlockSpec((None, 1, width), lambda r, g, pt: (r, 0, 0))

    def page(i):
        return pl.BlockSpec((None, 512, PAGE), lambda r, g, pt: (pt[r, g * pp + i], 0, 0))

    def chunk(r, g):
        return jnp.minimum(r * ng + g, n_chunk_steps - 1) // ngroups

    def crows(width):
        return pl.BlockSpec((CHUNK, width), lambda r, g, pt: (chunk(r, g), 0))

    def cstate(*shape):
        return pl.BlockSpec((None,) + shape, lambda r, g, pt: (chunk(r, g) // nc,) + (0,) * len(shape))

    return pl.pallas_call(
        functools.partial(_paged_kernel, n_chunk_steps=n_chunk_steps, chunks_per_seq=nc),
        grid_spec=pltpu.PrefetchScalarGridSpec(
            num_scalar_prefetch=1,
            grid=(nreq, ng),
            in_specs=[req(512), req(512), req(512), req(512),
                      pl.BlockSpec((1, W_A), lambda r, g, pt: (0, 0)),
                      pl.BlockSpec((8, 128), lambda r, g, pt: (0, 0)),
                      pl.BlockSpec((PAGE, PAGE * H_A), lambda r, g, pt: (0, 0))]
                     + [page(i) for i in range(pp)] + [page(i) for i in range(pp)]
                     + [crows(512), crows(512), crows(512), crows(512), crows(512), crows(GATE_PAD),
                        pl.BlockSpec((1, W_B), lambda r, g, pt: (0, 0))],
            out_specs=[req(512), crows(512), cstate(H_B, DV_B, DK_B), cstate(H_B, DK_B), cstate(H_B, 128)],
            scratch_shapes=[
                pltpu.VMEM((A_QK, pp * PAGE), BF16),
                pltpu.VMEM((pp * PAGE * H_A, DV_A), BF16),
                pltpu.VMEM((8, 128), F32),
                pltpu.VMEM((8, 128), F32),
                pltpu.VMEM((8, DV_A), F32),
                pltpu.VMEM((CHUNK, 128), F32),
                pltpu.VMEM((16, CHUNK), F32),
            ],
        ),
        out_shape=[
            jax.ShapeDtypeStruct((nreq, 1, W_A), BF16),
            jax.ShapeDtypeStruct((nb * seq, W_B), BF16),
            jax.ShapeDtypeStruct((nb, H_B, DV_B, DK_B), F32),
            jax.ShapeDtypeStruct((nb, H_B, DK_B), F32),
            jax.ShapeDtypeStruct((nb, H_B, 128), F32),
        ],
        compiler_params=pltpu.CompilerParams(
            dimension_semantics=("arbitrary", "arbitrary"), vmem_limit_bytes=VMEM_LIMIT),
        name="paged_attn_mlstm",
    )(page_table, q3, kn3, vn3, ga3, gna, lam_tile, expand, *([ck] * pp), *([cv] * pp),
      mq, mk, mv, mo, gb, zif, gnb)


def _mstep_kernel(q_ref, k_ref, v_ref, o_ref, gb_ref, zif_ref, c_ref, n_ref, m_ref, gnb_ref,
                  y_ref, co_ref, no_ref, mo_ref):
    rb = STEP_RB
    x = zif_ref[...]
    sub = lax.broadcasted_iota(jnp.int32, (rb, 128), 0)
    lane = lax.broadcasted_iota(jnp.int32, (rb, 128), 1)
    m_out = jnp.zeros((rb, 128), F32)
    for h in range(H_B):
        sl = slice(h * 128, (h + 1) * 128)
        qb = q_ref[:, sl]
        qh = qb.astype(F32)
        kh = k_ref[:, sl].astype(F32)
        vh = v_ref[:, sl].astype(F32)
        i_c = x[:, h:h + 1]
        b = _log_sigmoid(x[:, H_B + h:H_B + h + 1])
        m_prev = m_ref[:, h:h + 1]
        inter = b + m_prev
        m_t = jnp.maximum(inter, (b - b) + i_c)
        w_intra = jnp.exp(((b - b) + i_c) - m_t)
        w_inter = jnp.exp(inter - m_t)
        sw = jnp.sum(qh * kh, axis=1, keepdims=True) * w_intra
        cq = jnp.zeros((rb, 128), F32)
        for r in range(rb):
            res = _nt_dot(qb, c_ref[r, h].astype(BF16))
            cq = jnp.where(sub == r, res, cq)
        nh = n_ref[:, sl]
        num = sw * vh + w_inter * cq
        den = sw + w_inter * jnp.sum(nh * qh, axis=1, keepdims=True)
        hh = num / jnp.maximum(jnp.abs(den), jnp.exp(-m_t))
        y = _mlstm_out_gate(hh, o_ref[:, sl].astype(F32), gnb_ref[:, sl], gb_ref[:, sl].astype(F32))
        y_ref[:, sl] = y.astype(BF16)

        w_s = jnp.exp(((b - b) + i_c) - m_t)
        decay = jnp.exp((b + m_prev) - m_t)
        wv = w_s * vh
        for r in range(rb):
            vcol = jnp.broadcast_to(wv[r:r + 1, :], (DV_B, DK_B)).T
            co_ref[r, h] = decay[r:r + 1, :] * c_ref[r, h] + vcol * kh[r:r + 1, :]
        no_ref[:, sl] = decay * nh + w_s * kh
        m_out = jnp.where(lane == h, m_t, m_out)
    mo_ref[...] = m_out


def _mlstm_step(mq, mk, mv, mo, gb, zif, c0, n0, m0, gnb):
    nreq = mq.shape[0]
    rb = STEP_RB

    def rows(width):
        return pl.BlockSpec((rb, width), lambda i: (i, 0))

    cspec = pl.BlockSpec((rb, H_B, DV_B, DK_B), lambda i: (i, 0, 0, 0))
    return pl.pallas_call(
        _mstep_kernel,
        grid=(nreq // rb,),
        in_specs=[rows(512), rows(512), rows(512), rows(512), rows(512), rows(GATE_PAD),
                  cspec, rows(512), rows(H_B), pl.BlockSpec((1, W_B), lambda i: (0, 0))],
        out_specs=[rows(512), cspec, rows(512), rows(128)],
        out_shape=[
            jax.ShapeDtypeStruct((nreq, W_B), BF16),
            jax.ShapeDtypeStruct((nreq, H_B, DV_B, DK_B), F32),
            jax.ShapeDtypeStruct((nreq, H_B * DK_B), F32),
            jax.ShapeDtypeStruct((nreq, 128), F32),
        ],
        compiler_params=pltpu.CompilerParams(
            dimension_semantics=("arbitrary",), vmem_limit_bytes=VMEM_LIMIT),
        name="mlstm_step",
    )(mq, mk, mv, mo, gb, zif, c0, n0, m0, gnb)


def _out_kernel(ma_ref, mb_ref, w_ref, x_ref, gate_ref, y_ref):
    acc = _dot(ma_ref[...], w_ref[0:W_A, :]) + _dot(mb_ref[...], w_ref[W_A:W_A + W_B, :])
    y_ref[...] = x_ref[...] + gate_ref[...] * acc


def _out_proj(mix_a, mix_b, w_out, x2d, mod, per_row, tm):
    m_rows = x2d.shape[0]
    nt = m_rows // tm
    if per_row:
        gate_spec = pl.BlockSpec((tm, D_MODEL), lambda i: (i, 2))
    else:
        tiles_per_batch = nt // mod.shape[0]
        gate_spec = pl.BlockSpec((None, 1, D_MODEL), lambda i: (i // tiles_per_batch, 0, 2))
    return pl.pallas_call(
        _out_kernel,
        grid=(nt,),
        in_specs=[
            pl.BlockSpec((tm, W_A), lambda i: (i, 0)),
            pl.BlockSpec((tm, W_B), lambda i: (i, 0)),
            pl.BlockSpec((W_A + W_B, D_MODEL), lambda i: (0, 0)),
            pl.BlockSpec((tm, D_MODEL), lambda i: (i, 0)),
            gate_spec,
        ],
        out_specs=pl.BlockSpec((tm, D_MODEL), lambda i: (i, 0)),
        out_shape=jax.ShapeDtypeStruct((m_rows, D_MODEL), F32),
        compiler_params=pltpu.CompilerParams(
            dimension_semantics=("arbitrary",), vmem_limit_bytes=VMEM_LIMIT),
        name="out_rows" if per_row else "out_bcast",
    )(mix_a, mix_b, w_out, x2d, mod)


def kernel(x_prompt, x_sample, c_prompt, c_sample, cache_k, cache_v, state_C, state_n, state_m,
           page_table, norm_g, w_ada, b_ada, w_in, b_in, g_q, g_k, lam_q, lam_k, gn_a, gn_b, w_out):
    assert w_in.shape[0] == 1, "single-layer model"
    nb, seq, _ = x_prompt.shape
    nreq = x_sample.shape[0]
    assert x_sample.shape[1] == 1

    w = w_in[0]
    bvec = b_in[0]
    gate_lo = N_MAIN
    gb_lo = N_MAIN + 2 * H_B
    wm = w[:, :N_MAIN].astype(BF16)
    wif = jnp.pad(w[:, gate_lo:gb_lo], ((0, 0), (0, GATE_PAD - 2 * H_B))).astype(BF16)
    wgb = w[:, gb_lo:].astype(BF16)
    bm = bvec[None, :N_MAIN]
    bif = jnp.pad(bvec[gate_lo:gb_lo], (0, GATE_PAD - 2 * H_B))[None, :]
    bgb = bvec[None, gb_lo:]
    gq = jnp.tile(g_q[0], 2 * H_A)[None, :]
    gk = jnp.tile(g_k[0], 2 * H_A)[None, :]
    grp = jnp.arange(A_QK) // DK_A
    seg = (grp[:, None] == grp[None, :]).astype(BF16)
    weights = (norm_g[0][None, :], wm, wif, wgb, bm, bif, bgb, gq, gk, seg)
    gna = gn_a[0].reshape(1, W_A)
    gnb = gn_b[0].reshape(1, W_B)
    wo = w_out[0].astype(BF16)

    pad_rows = (-(nb + nreq)) % 8
    c_all = jnp.concatenate([c_prompt, c_sample, jnp.zeros((pad_rows, D_MODEL), F32)], axis=0)
    mod, lam_tile = _ada(c_all, w_ada[0], b_ada[0][None, :], lam_q[0], lam_k[0])
    mod_p = mod[:nb].reshape(nb, 1, 3 * D_MODEL)
    mod_s = mod[nb:nb + nreq]

    xp = x_prompt.reshape(nb * seq, D_MODEL)
    (q, kf, kb, vf, vb, ga, mq, mk, mv, mo, zif, gb) = _proj(xp, mod_p, False, PROJ_TM, weights)
    xs = x_sample.reshape(nreq, D_MODEL)
    (qs, kfs, kbs, vfs, vbs, gas, mqs, mks, mvs, mos, zifs, gbs) = _proj(xs, mod_s, True, nreq, weights)

    mix_a = _attn_prompt(q, kb, vb, ga, gna, lam_tile, nb, seq)
    ck = jnp.transpose(cache_k[0], (0, 2, 3, 4, 1)).reshape(cache_k.shape[1], A_QK, PAGE)
    cv = cache_v[0].reshape(cache_v.shape[1], PAGE * H_A, DV_A)
    r3 = lambda a: a.reshape(nreq, 1, a.shape[-1])
    mix_as, mix_b, c_p, n_p, m_p = _paged_attn_mlstm(
        page_table, r3(qs), r3(kbs), r3(vbs), r3(gas), gna, lam_tile, ck, cv,
        mq, mk, mv, mo, gb, zif, gnb, nb, seq)
    mix_bs, c_s, n_s, m_s = _mlstm_step(mqs, mks, mvs, mos, gbs, zifs, state_C[0],
                                        state_n[0].reshape(nreq, H_B * DK_B), state_m[0], gnb)

    y_p = _out_proj(mix_a, mix_b, wo, xp, mod_p, False, OUT_TM)
    y_s = _out_proj(mix_as.reshape(nreq, W_A), mix_bs, wo, xs, mod_s, True, nreq)

    return (
        y_p.reshape(nb, seq, D_MODEL),
        y_s.reshape(nreq, 1, D_MODEL),
        jnp.transpose(kf.reshape(nb, H_A, 2, DK_A, seq), (0, 4, 1, 2, 3))[None],
        vf.reshape(1, nb, seq, H_A, DV_A),
        kfs.reshape(1, nreq, 1, H_A, 2, DK_A),
        vfs.reshape(1, nreq, 1, H_A, DV_A),
        c_p[None],
        n_p[None],
        m_p[:, :, 0][None],
        c_s[None],
        n_s.reshape(1, nreq, H_B, DK_B),
        m_s[:, :H_B][None],
    )
```

```python
import functools
import math

import jax
import jax.numpy as jnp
from jax import lax
from jax.experimental import pallas as pl
from jax.experimental.pallas import tpu as pltpu

F32 = jnp.float32
BF16 = jnp.bfloat16

D_MODEL = 1024
H_A = 4
assert H_A & (H_A - 1) == 0
DK_A = 64
DV_A = 128
A_QK = H_A * 2 * DK_A
W_A = H_A * DV_A
H_B = 4
DK_B = 128
DV_B = 128
W_B = H_B * DV_B
N_MAIN = 8 * 512
GATE_PAD = 128
PAGE = 128
CHUNK = 128
EPS = 1e-6
NEG = -1e30
LAM_INIT = 0.8 - 0.6 * math.exp(-0.3 * 0)
ALIBI_SLOPES = tuple(2.0 ** (-8.0 * (h + 1) / H_A) for h in range(H_A))
LOG2E = math.log2(math.e)
POS_RADIX = 64

VMEM_LIMIT = 56 * 1024 * 1024

PROJ_TM = 512
OUT_TM = 1024
ATT_T = 512
PAGES_PER_STEP = 32
STEP_RB = 8


def _nt_dot(a, b):
    return lax.dot_general(a, b, (((1,), (1,)), ((), ())), preferred_element_type=F32)


def _dot(a, b):
    return jnp.dot(a, b, preferred_element_type=F32)


def _sigmoid(x):
    return 1.0 / (1.0 + jnp.exp(-x))


def _silu(x):
    return x * _sigmoid(x)


def _log_sigmoid(x):
    return jnp.minimum(x, 0.0) - jnp.log1p(jnp.exp(-jnp.abs(x)))


def _split3(a):
    a1 = a.astype(BF16)
    r1 = a - a1.astype(F32)
    a2 = r1.astype(BF16)
    a3 = (r1 - a2.astype(F32)).astype(BF16)
    return a1, a2, a3


def _ada_kernel(c_ref, w_ref, b_ref, lq_ref, lk_ref, mod_ref, lam_ref):
    a = _silu(c_ref[...])
    w = w_ref[...]
    a1, a2, _ = _split3(a)
    w1, w2, _ = _split3(w)
    mod_ref[...] = (_dot(a1, w1) + (_dot(a1, w2) + _dot(a2, w1))) + b_ref[...]

    @pl.when(pl.program_id(0) == 0)
    def _():
        s = jnp.sum(lq_ref[...] * lk_ref[...], axis=1, keepdims=True)
        e = jnp.exp(s)
        lam = e[0:1, :] - e[1:2, :] + LAM_INIT
        lam_ref[...] = jnp.broadcast_to(lam, lam_ref.shape)


def _ada(c_all, w_ada, b_ada, lam_q, lam_k):
    rows = c_all.shape[0]
    nblk = 3
    return pl.pallas_call(
        _ada_kernel,
        grid=(nblk,),
        in_specs=[
            pl.BlockSpec((rows, D_MODEL), lambda j: (0, 0)),
            pl.BlockSpec((D_MODEL, D_MODEL), lambda j: (0, j)),
            pl.BlockSpec((1, D_MODEL), lambda j: (0, j)),
            pl.BlockSpec((2, DK_A), lambda j: (0, 0)),
            pl.BlockSpec((2, DK_A), lambda j: (0, 0)),
        ],
        out_specs=[
            pl.BlockSpec((rows, D_MODEL), lambda j: (0, j)),
            pl.BlockSpec((8, 128), lambda j: (0, 0)),
        ],
        out_shape=[
            jax.ShapeDtypeStruct((rows, 3 * D_MODEL), F32),
            jax.ShapeDtypeStruct((8, 128), F32),
        ],
        compiler_params=pltpu.CompilerParams(
            dimension_semantics=("arbitrary",), vmem_limit_bytes=VMEM_LIMIT),
        name="ada",
    )(c_all, w_ada, b_ada, lam_q, lam_k)


def _proj_kernel(x_ref, shift_ref, scale_ref, ng_ref, wm_ref, wif_ref, wgb_ref,
                 bm_ref, bif_ref, bgb_ref, gq_ref, gk_ref, seg_ref,
                 q_ref, kf_ref, kb_ref, vf_ref, vb_ref, ga_ref,
                 mq_ref, mk_ref, mv_ref, mo_ref, zif_ref, gb_ref, *, k_transposed):
    x = x_ref[...]
    ms = jnp.mean(x * x, axis=-1, keepdims=True)
    h = x * lax.rsqrt(ms + EPS) * ng_ref[...]
    h = h * (1.0 + scale_ref[...]) + shift_ref[...]
    hb = h.astype(BF16)

    def col(j):
        sl = slice(j * 512, (j + 1) * 512)
        return _dot(hb, wm_ref[:, sl]) + bm_ref[:, sl]

    def headnorm(z, g):
        ss = _dot((z * z).astype(BF16), seg_ref[...])
        return z * lax.rsqrt(ss * (1.0 / DK_A) + EPS) * g

    q_ref[...] = (headnorm(col(0), gq_ref[...]) * (DK_A ** -0.5 * LOG2E)).astype(BF16)
    k = headnorm(col(1), gk_ref[...])
    kf_ref[...] = k.T if k_transposed else k
    kb_ref[...] = k.astype(BF16)
    v = col(2)
    for hh in range(H_A):
        vf_ref[pl.ds(hh, x.shape[0], stride=H_A), :] = v[:, hh * DV_A:(hh + 1) * DV_A]
    vb_ref[...] = v.astype(BF16)
    ga_ref[...] = col(3).astype(BF16)
    mq_ref[...] = col(4).astype(BF16)
    mk_ref[...] = (col(5) * (DK_B ** -0.5)).astype(BF16)
    mv_ref[...] = col(6).astype(BF16)
    mo_ref[...] = _sigmoid(col(7)).astype(BF16)
    zif_ref[...] = _dot(hb, wif_ref[...]) + bif_ref[...]
    gb_ref[...] = (_dot(hb, wgb_ref[...]) + bgb_ref[...]).astype(BF16)


def _proj(x2d, mod, per_row, tm, weights):
    (ng, wm, wif, wgb, bm, bif, bgb, gq, gk, seg) = weights
    m_rows = x2d.shape[0]
    nt = m_rows // tm
    if per_row:
        shift_spec = pl.BlockSpec((tm, D_MODEL), lambda i: (i, 0))
        scale_spec = pl.BlockSpec((tm, D_MODEL), lambda i: (i, 1))
    else:
        tiles_per_batch = nt // mod.shape[0]
        shift_spec = pl.BlockSpec((None, 1, D_MODEL), lambda i: (i // tiles_per_batch, 0, 0))
        scale_spec = pl.BlockSpec((None, 1, D_MODEL), lambda i: (i // tiles_per_batch, 0, 1))

    def const(shape):
        return pl.BlockSpec(shape, lambda i: (0, 0))

    def rows(width):
        return pl.BlockSpec((tm, width), lambda i: (i, 0))

    out_dtypes = [BF16, F32, BF16, F32, BF16, BF16, BF16, BF16, BF16, BF16, F32, BF16]
    out_widths = [512, 512, 512, DV_A, 512, 512, 512, 512, 512, 512, GATE_PAD, 512]
    out_rows = [1, 1, 1, H_A, 1, 1, 1, 1, 1, 1, 1, 1]
    out_specs = [pl.BlockSpec((tm * r, w), lambda i: (i, 0)) for w, r in zip(out_widths, out_rows)]
    out_shape = [jax.ShapeDtypeStruct((m_rows * r, w), d)
                 for w, r, d in zip(out_widths, out_rows, out_dtypes)]
    if not per_row:
        out_specs[1] = pl.BlockSpec((None, A_QK, tm),
                                    lambda i: (i // tiles_per_batch, 0, i % tiles_per_batch))
        out_shape[1] = jax.ShapeDtypeStruct((mod.shape[0], A_QK, m_rows // mod.shape[0]), F32)
    return pl.pallas_call(
        functools.partial(_proj_kernel, k_transposed=not per_row),
        grid=(nt,),
        in_specs=[
            rows(D_MODEL), shift_spec, scale_spec, const((1, D_MODEL)),
            const((D_MODEL, N_MAIN)), const((D_MODEL, GATE_PAD)), const((D_MODEL, W_B)),
            const((1, N_MAIN)), const((1, GATE_PAD)), const((1, W_B)),
            const((1, A_QK)), const((1, A_QK)), const((A_QK, A_QK)),
        ],
        out_specs=out_specs,
        out_shape=out_shape,
        compiler_params=pltpu.CompilerParams(
            dimension_semantics=("arbitrary",), vmem_limit_bytes=VMEM_LIMIT),
        name="proj_rows" if per_row else "proj_bcast",
    )(x2d, mod, mod, ng, wm, wif, wgb, bm, bif, bgb, gq, gk, seg)


def _diff_norm_gate(o0, o1, lam, gna, ga):
    d = o0 - lam * o1
    ya = d * lax.rsqrt(jnp.mean(d * d, axis=-1, keepdims=True) + EPS) * gna
    ya = ya * (1.0 - LAM_INIT)
    return ya * _silu(ga)


def _mlstm_out_gate(hh, o, gnb, gb):
    hg = o * hh
    yb = hg * lax.rsqrt(jnp.mean(hg * hg, axis=-1, keepdims=True) + EPS) * gnb
    return yb * _silu(gb)


def _alibi_features(seq):
    pos = jnp.arange(seq, dtype=jnp.int32)
    digits = [(pos // POS_RADIX).astype(F32), (pos % POS_RADIX).astype(F32)]
    ones = jnp.ones((seq,), F32)
    qf, kf = [], []
    for slope in ALIBI_SLOPES:
        pieces = [p.astype(F32) for p in _split3(jnp.float32(slope * LOG2E))]
        qcols, kcols = [], []
        for c in pieces:
            qcols += [POS_RADIX * c * ones, c * ones]
            kcols += digits
        for c in pieces:
            qcols += digits
            kcols += [-POS_RADIX * c * ones, -c * ones]
        pad = ((0, 0), (0, 128 - len(qcols)))
        qf.append(jnp.pad(jnp.stack(qcols, axis=1), pad))
        kf.append(jnp.pad(jnp.stack(kcols, axis=1), pad))
    return jnp.stack(qf).astype(BF16), jnp.stack(kf).astype(BF16)


def _attn_kernel(q_ref, qf_ref, k_ref, kf_ref, v_ref, ga_ref, gna_ref, lam_ref, out_ref,
                 lhs_s, s_s, mask_s):
    t = ATT_T
    qi = pl.program_id(2)

    q = q_ref[...].astype(F32)
    lane = lax.broadcasted_iota(jnp.int32, (t, 2 * DK_A), 1)
    lhs_s[0:t, 0:128] = jnp.where(lane < DK_A, q, 0.0).astype(BF16)
    lhs_s[t:2 * t, 0:128] = jnp.where(lane >= DK_A, q, 0.0).astype(BF16)
    lhs_s[0:t, 128:256] = qf_ref[...]
    lhs_s[t:2 * t, 128:256] = qf_ref[...]

    @pl.when((pl.program_id(0) == 0) & (pl.program_id(1) == 0) & (qi == 0))
    def _():
        krow = lax.broadcasted_iota(jnp.int32, (t, 2 * t), 0)
        qcol = lax.broadcasted_iota(jnp.int32, (t, 2 * t), 1)
        mask_s[...] = jnp.where(krow > jnp.where(qcol >= t, qcol - t, qcol), NEG, 0.0)

    def scores(kj, slot):
        off = pl.multiple_of(kj * t, t)
        kx = jnp.concatenate([k_ref[pl.ds(off, t), :], kf_ref[pl.ds(off, t), :]], axis=1)
        s_s[slot] = _nt_dot(kx, lhs_s[...])

    def consume(kj, slot, carry, masked):
        m, l, acc = carry
        off = pl.multiple_of(kj * t, t)
        s = s_s[slot]
        if masked:
            s = s + mask_s[...]
        m_new = jnp.maximum(m, jnp.max(s, axis=0, keepdims=True))
        alpha = jnp.exp2(m - m_new)
        p = jnp.exp2(s - m_new)
        l = alpha * l + jnp.sum(p, axis=0, keepdims=True)
        pv = lax.dot_general(v_ref[pl.ds(off, t), :], p.astype(BF16), (((0,), (0,)), ((), ())),
                             preferred_element_type=F32)
        return m_new, l, alpha * acc + pv

    def pair(i, carry):
        scores(2 * i + 1, 1)
        carry = consume(2 * i, 0, carry, False)
        scores(2 * i + 2, 0)
        return consume(2 * i + 1, 1, carry, False)

    def tail_even(carry):
        return consume(qi, 0, carry, True)

    def tail_odd(carry):
        scores(qi, 1)
        return consume(qi, 1, consume(qi - 1, 0, carry, False), True)

    init = (jnp.full((1, 2 * t), NEG, F32), jnp.zeros((1, 2 * t), F32), jnp.zeros((DV_A, 2 * t), F32))
    scores(0, 0)
    carry = lax.fori_loop(0, qi // 2, pair, init)
    m, l, acc = lax.cond(qi % 2 == 1, tail_odd, tail_even, carry)
    o = (acc / l).T
    lam = lam_ref[0:1, :]
    out = _diff_norm_gate(o[:t], o[t:], lam, gna_ref[...], ga_ref[...].astype(F32))
    out_ref[...] = out.astype(BF16)


def _attn_prompt(q, k, v, ga, gna, lam_tile, nb, seq):
    nq = seq // ATT_T
    qf, kf = _alibi_features(seq)
    return pl.pallas_call(
        _attn_kernel,
        grid=(nb, H_A, nq),
        in_specs=[
            pl.BlockSpec((ATT_T, 128), lambda b, h, i: (b * nq + i, h)),
            pl.BlockSpec((None, ATT_T, 128), lambda b, h, i: (h, i, 0)),
            pl.BlockSpec((seq, 128), lambda b, h, i: (b, h)),
            pl.BlockSpec((None, seq, 128), lambda b, h, i: (h, 0, 0)),
            pl.BlockSpec((seq, 128), lambda b, h, i: (b, h)),
            pl.BlockSpec((ATT_T, 128), lambda b, h, i: (b * nq + i, h)),
            pl.BlockSpec((1, 128), lambda b, h, i: (0, h)),
            pl.BlockSpec((8, 128), lambda b, h, i: (0, 0)),
        ],
        out_specs=pl.BlockSpec((ATT_T, 128), lambda b, h, i: (b * nq + i, h)),
        out_shape=jax.ShapeDtypeStruct((nb * seq, W_A), BF16),
        scratch_shapes=[pltpu.VMEM((2 * ATT_T, 256), BF16),
                        pltpu.VMEM((2, ATT_T, 2 * ATT_T), F32),
                        pltpu.VMEM((ATT_T, 2 * ATT_T), F32)],
        compiler_params=pltpu.CompilerParams(
            dimension_semantics=("arbitrary", "arbitrary", "arbitrary"),
            vmem_limit_bytes=VMEM_LIMIT),
        name="attn_prompt",
    )(q, qf, k, kf, v, ga, gna, lam_tile)


def _mlstm_chunk(first, heads, gates_cached, q_ref, k_ref, v_ref, o_ref, gb_ref, zif_ref, gnb_ref,
                 y_ref, c_ref, n_ref, m_ref, bcum_s, rows_s):
    L = CHUNK

    x = zif_ref[...]
    row = lax.broadcasted_iota(jnp.int32, (L, L), 0)
    colv = lax.broadcasted_iota(jnp.int32, (L, L), 1)
    causal = row >= colv
    if not gates_cached:
        logf = _log_sigmoid(x)
        tri = jnp.where(causal, 1.0, 0.0).astype(BF16)
        f1, f2, f3 = _split3(logf)
        bcum_s[...] = _dot(tri, f1) + (_dot(tri, f2) + _dot(tri, f3))
        rows_s[0:8, :] = x.T[0:8, :]
        rows_s[8:16, :] = bcum_s[...].T[0:8, :]
    bmat = bcum_s[...]
    xt = rows_s[0:8, :]
    bt = rows_s[8:16, :]

    for h in heads:
        @pl.when(first)
        def _():
            c_ref[h] = jnp.zeros((DV_B, DK_B), F32)
            n_ref[h:h + 1, :] = jnp.zeros((1, DK_B), F32)
            m_ref[h:h + 1, :] = jnp.zeros((1, 128), F32)

        sl = slice(h * 128, (h + 1) * 128)
        qh = q_ref[:, sl]
        kh = k_ref[:, sl]
        vh = v_ref[:, sl]
        icol = x[:, h:h + 1]
        bcol = bmat[:, H_B + h:H_B + h + 1]
        irow = xt[h:h + 1, :]
        brow = bt[H_B + h:H_B + h + 1, :]
        m_prev = m_ref[h:h + 1, 0:1]
        nrow = n_ref[h:h + 1, :]
        c_old = c_ref[h]

        d = jnp.where(causal, (bcol - brow) + irow, NEG)
        inter = bcol + m_prev
        m_t = jnp.maximum(inter, jnp.max(d, axis=1, keepdims=True))
        w_intra = jnp.exp(d - m_t)
        w_inter = jnp.exp(inter - m_t)
        sw = _nt_dot(qh, kh) * w_intra
        num = _dot(sw.astype(BF16), vh) + w_inter * _nt_dot(qh, c_old.astype(BF16))
        den = jnp.sum(sw, axis=1, keepdims=True) \
            + w_inter * jnp.sum(qh.astype(F32) * nrow, axis=1, keepdims=True)
        hh = num / jnp.maximum(jnp.abs(den), jnp.exp(-m_t))
        y = _mlstm_out_gate(hh, o_ref[:, sl].astype(F32), gnb_ref[:, sl], gb_ref[:, sl].astype(F32))
        y_ref[:, sl] = y.astype(BF16)

        m_new = m_t[L - 1:L, :]
        b_last = bcol[L - 1:L, :]
        w_s = jnp.exp(((b_last - bcol) + icol) - m_new)
        decay = jnp.exp((b_last + m_prev) - m_new)
        wv = w_s * vh.astype(F32)
        c_ref[h] = decay * c_old + _dot(wv.T.astype(BF16), kh)
        n_ref[h:h + 1, :] = decay * nrow + jnp.sum(w_s * kh.astype(F32), axis=0, keepdims=True)
        m_ref[h:h + 1, :] = jnp.broadcast_to(m_new, (1, 128))


N_MLSTM_IN = 7
N_MLSTM_OUT = 4
MLSTM_HEAD_GROUPS = ((0, 1), (2, 3))


def _paged_kernel(pt_ref, q_ref, kn_ref, vn_ref, ga_ref, gna_ref, lam_ref, exp_ref, *rest,
                  n_chunk_steps, chunks_per_seq):
    pp = PAGES_PER_STEP
    k_refs = rest[:pp]
    v_refs = rest[pp:2 * pp]
    mlstm_in = rest[2 * pp:2 * pp + N_MLSTM_IN]
    out_ref = rest[2 * pp + N_MLSTM_IN]
    mlstm_out = rest[2 * pp + N_MLSTM_IN + 1:2 * pp + N_MLSTM_IN + 1 + N_MLSTM_OUT]
    kb_s, vb_s, m_s, l_s, acc_s, bcum_s, rows_s = rest[2 * pp + N_MLSTM_IN + 1 + N_MLSTM_OUT:]
    g = pl.program_id(1)
    ng = pl.num_programs(1)
    ntok = pp * PAGE

    step = pl.program_id(0) * ng + g

    ngroups = len(MLSTM_HEAD_GROUPS)
    first = (step // ngroups) % chunks_per_seq == 0
    for gi, heads in enumerate(MLSTM_HEAD_GROUPS):
        @pl.when(jnp.logical_and(step < n_chunk_steps, step % ngroups == gi))
        def _():
            _mlstm_chunk(first, heads, gi > 0, *mlstm_in, *mlstm_out, bcum_s, rows_s)

    @pl.when(g == 0)
    def _():
        m_s[...] = jnp.full_like(m_s, NEG)
        l_s[...] = jnp.zeros_like(l_s)
        acc_s[...] = jnp.zeros_like(acc_s)

    vrows = PAGE * H_A
    for i in range(pp):
        kb_s[:, i * PAGE:(i + 1) * PAGE] = k_refs[i][...].astype(BF16)
        vb_s[i * vrows:(i + 1) * vrows, :] = v_refs[i][...].astype(BF16)

    q = q_ref[...].astype(F32)
    sub = lax.broadcasted_iota(jnp.int32, (8, A_QK), 0)
    lane = lax.broadcasted_iota(jnp.int32, (8, A_QK), 1)
    qbd32 = jnp.where((lane >> 6) == sub, jnp.broadcast_to(q, (8, A_QK)), 0.0)
    qbd = qbd32.astype(BF16)

    j = lax.broadcasted_iota(jnp.int32, (8, 1), 0)
    slope = jnp.where(j < 2, ALIBI_SLOPES[0] * LOG2E,
                      jnp.where(j < 4, ALIBI_SLOPES[1] * LOG2E,
                                jnp.where(j < 6, ALIBI_SLOPES[2] * LOG2E,
                                          ALIBI_SLOPES[3] * LOG2E))).astype(F32)
    kpos = g * ntok + lax.broadcasted_iota(jnp.int32, (1, ntok), 1)
    dist = (ng * ntok - kpos).astype(F32)

    s = _dot(qbd, kb_s[...]) - slope * dist
    m_old = m_s[:, 0:1]
    l_old = l_s[:, 0:1]
    m_new = jnp.maximum(m_old, jnp.max(s, axis=1, keepdims=True))
    alpha = jnp.exp2(m_old - m_new)
    p = jnp.exp2(s - m_new)
    l_new = alpha * l_old + jnp.sum(p, axis=1, keepdims=True)
    pst = jnp.concatenate([p[:, i * PAGE:(i + 1) * PAGE] for i in range(pp)], axis=0)
    spread = _dot(pst.astype(BF16), exp_ref[...])
    own_head = (lax.broadcasted_iota(jnp.int32, (8, vrows), 1) & (H_A - 1)) \
        == (lax.broadcasted_iota(jnp.int32, (8, vrows), 0) >> 1)
    p4 = jnp.concatenate([jnp.where(own_head, spread[8 * i:8 * (i + 1), :], 0.0) for i in range(pp)],
                         axis=1)
    acc = alpha * acc_s[...] + _dot(p4.astype(BF16), vb_s[...])
    head_of_row = lax.broadcasted_iota(jnp.int32, (8, DV_A), 0) >> 1
    m_s[...] = jnp.broadcast_to(m_new, m_s.shape)
    l_s[...] = jnp.broadcast_to(l_new, l_s.shape)
    acc_s[...] = acc

    @pl.when(g == ng - 1)
    def _():
        kn = kn_ref[...].astype(F32)
        s_new = jnp.sum(qbd32 * kn, axis=1, keepdims=True)
        m_fin = jnp.maximum(m_new, s_new)
        a2 = jnp.exp2(m_new - m_fin)
        p_new = jnp.exp2(s_new - m_fin)
        l_fin = a2 * l_new + p_new
        vn = jnp.zeros((8, DV_A), F32)
        for h in range(H_A):
            vrow = vn_ref[:, h * DV_A:(h + 1) * DV_A].astype(F32)
            vn = jnp.where(head_of_row == h, jnp.broadcast_to(vrow, (8, DV_A)), vn)
        o = (a2 * acc + p_new * vn) / l_fin
        lam = lam_ref[0:1, :]
        for h in range(H_A):
            sl = slice(h * DV_A, (h + 1) * DV_A)
            out = _diff_norm_gate(o[2 * h:2 * h + 1, :], o[2 * h + 1:2 * h + 2, :], lam,
                                  gna_ref[:, sl], ga_ref[:, sl].astype(F32))
            out_ref[:, sl] = out.astype(BF16)


def _paged_attn_mlstm(page_table, q3, kn3, vn3, ga3, gna, lam_tile, ck, cv,
                      mq, mk, mv, mo, gb, zif, gnb, nb, seq):
    nreq, npages = page_table.shape
    pp = PAGES_PER_STEP
    ng = npages // pp
    nc = seq // CHUNK
    ngroups = len(MLSTM_HEAD_GROUPS)
    n_chunk_steps = nb * nc * ngroups
    assert n_chunk_steps <= nreq * ng, "every mLSTM (chunk, head group) needs a grid step"

    expand = (jnp.arange(PAGE * H_A)[None, :] // H_A == jnp.arange(PAGE)[:, None]).astype(BF16)

    def req(width):
        return pl.BlockSpec((None, 1, width), lambda r, g, pt: (r, 0, 0))

    def page(i):
        return pl.BlockSpec((None, 512, PAGE), lambda r, g, pt: (pt[r, g * pp + i], 0, 0))

    def chunk(r, g):
        return jnp.minimum(r * ng + g, n_chunk_steps - 1) // ngroups

    def crows(width):
        return pl.BlockSpec((CHUNK, width), lambda r, g, pt: (chunk(r, g), 0))

    def cstate(*shape):
        return pl.BlockSpec((None,) + shape, lambda r, g, pt: (chunk(r, g) // nc,) + (0,) * len(shape))

    return pl.pallas_call(
        functools.partial(_paged_kernel, n_chunk_steps=n_chunk_steps, chunks_per_seq=nc),
        grid_spec=pltpu.PrefetchScalarGridSpec(
            num_scalar_prefetch=1,
            grid=(nreq, ng),
            in_specs=[req(512), req(512), req(512), req(512),
                      pl.BlockSpec((1, W_A), lambda r, g, pt: (0, 0)),
                      pl.BlockSpec((8, 128), lambda r, g, pt: (0, 0)),
                      pl.BlockSpec((PAGE, PAGE * H_A), lambda r, g, pt: (0, 0))]
                     + [page(i) for i in range(pp)] + [page(i) for i in range(pp)]
                     + [crows(512), crows(512), crows(512), crows(512), crows(512), crows(GATE_PAD),
                        pl.BlockSpec((1, W_B), lambda r, g, pt: (0, 0))],
            out_specs=[req(512), crows(512), cstate(H_B, DV_B, DK_B), cstate(H_B, DK_B), cstate(H_B, 128)],
            scratch_shapes=[
                pltpu.VMEM((A_QK, pp * PAGE), BF16),
                pltpu.VMEM((pp * PAGE * H_A, DV_A), BF16),
                pltpu.VMEM((8, 128), F32),
                pltpu.VMEM((8, 128), F32),
                pltpu.VMEM((8, DV_A), F32),
                pltpu.VMEM((CHUNK, 128), F32),
                pltpu.VMEM((16, CHUNK), F32),
            ],
        ),
        out_shape=[
            jax.ShapeDtypeStruct((nreq, 1, W_A), BF16),
            jax.ShapeDtypeStruct((nb * seq, W_B), BF16),
            jax.ShapeDtypeStruct((nb, H_B, DV_B, DK_B), F32),
            jax.ShapeDtypeStruct((nb, H_B, DK_B), F32),
            jax.ShapeDtypeStruct((nb, H_B, 128), F32),
        ],
        compiler_params=pltpu.CompilerParams(
            dimension_semantics=("arbitrary", "arbitrary"), vmem_limit_bytes=VMEM_LIMIT),
        name="paged_attn_mlstm",
    )(page_table, q3, kn3, vn3, ga3, gna, lam_tile, expand, *([ck] * pp), *([cv] * pp),
      mq, mk, mv, mo, gb, zif, gnb)


def _mstep_kernel(q_ref, k_ref, v_ref, o_ref, gb_ref, zif_ref, c_ref, n_ref, m_ref, gnb_ref,
                  y_ref, co_ref, no_ref, mo_ref):
    rb = STEP_RB
    x = zif_ref[...]
    sub = lax.broadcasted_iota(jnp.int32, (rb, 128), 0)
    lane = lax.broadcasted_iota(jnp.int32, (rb, 128), 1)
    m_out = jnp.zeros((rb, 128), F32)
    for h in range(H_B):
        sl = slice(h * 128, (h + 1) * 128)
        qb = q_ref[:, sl]
        qh = qb.astype(F32)
        kh = k_ref[:, sl].astype(F32)
        vh = v_ref[:, sl].astype(F32)
        i_c = x[:, h:h + 1]
        b = _log_sigmoid(x[:, H_B + h:H_B + h + 1])
        m_prev = m_ref[:, h:h + 1]
        inter = b + m_prev
        m_t = jnp.maximum(inter, (b - b) + i_c)
        w_intra = jnp.exp(((b - b) + i_c) - m_t)
        w_inter = jnp.exp(inter - m_t)
        sw = jnp.sum(qh * kh, axis=1, keepdims=True) * w_intra
        cq = jnp.zeros((rb, 128), F32)
        for r in range(rb):
            res = _nt_dot(qb, c_ref[r, h].astype(BF16))
            cq = jnp.where(sub == r, res, cq)
        nh = n_ref[:, sl]
        num = sw * vh + w_inter * cq
        den = sw + w_inter * jnp.sum(nh * qh, axis=1, keepdims=True)
        hh = num / jnp.maximum(jnp.abs(den), jnp.exp(-m_t))
        y = _mlstm_out_gate(hh, o_ref[:, sl].astype(F32), gnb_ref[:, sl], gb_ref[:, sl].astype(F32))
        y_ref[:, sl] = y.astype(BF16)

        w_s = jnp.exp(((b - b) + i_c) - m_t)
        decay = jnp.exp((b + m_prev) - m_t)
        wv = w_s * vh
        for r in range(rb):
            vcol = jnp.broadcast_to(wv[r:r + 1, :], (DV_B, DK_B)).T
            co_ref[r, h] = decay[r:r + 1, :] * c_ref[r, h] + vcol * kh[r:r + 1, :]
        no_ref[:, sl] = decay * nh + w_s * kh
        m_out = jnp.where(lane == h, m_t, m_out)
    mo_ref[...] = m_out


def _mlstm_step(mq, mk, mv, mo, gb, zif, c0, n0, m0, gnb):
    nreq = mq.shape[0]
    rb = STEP_RB

    def rows(width):
        return pl.BlockSpec((rb, width), lambda i: (i, 0))

    cspec = pl.BlockSpec((rb, H_B, DV_B, DK_B), lambda i: (i, 0, 0, 0))
    return pl.pallas_call(
        _mstep_kernel,
        grid=(nreq // rb,),
        in_specs=[rows(512), rows(512), rows(512), rows(512), rows(512), rows(GATE_PAD),
                  cspec, rows(512), rows(H_B), pl.BlockSpec((1, W_B), lambda i: (0, 0))],
        out_specs=[rows(512), cspec, rows(512), rows(128)],
        out_shape=[
            jax.ShapeDtypeStruct((nreq, W_B), BF16),
            jax.ShapeDtypeStruct((nreq, H_B, DV_B, DK_B), F32),
            jax.ShapeDtypeStruct((nreq, H_B * DK_B), F32),
            jax.ShapeDtypeStruct((nreq, 128), F32),
        ],
        compiler_params=pltpu.CompilerParams(
            dimension_semantics=("arbitrary",), vmem_limit_bytes=VMEM_LIMIT),
        name="mlstm_step",
    )(mq, mk, mv, mo, gb, zif, c0, n0, m0, gnb)


def _out_kernel(ma_ref, mb_ref, w_ref, x_ref, gate_ref, y_ref):
    acc = _dot(ma_ref[...], w_ref[0:W_A, :]) + _dot(mb_ref[...], w_ref[W_A:W_A + W_B, :])
    y_ref[...] = x_ref[...] + gate_ref[...] * acc


def _out_proj(mix_a, mix_b, w_out, x2d, mod, per_row, tm):
    m_rows = x2d.shape[0]
    nt = m_rows // tm
    if per_row:
        gate_spec = pl.BlockSpec((tm, D_MODEL), lambda i: (i, 2))
    else:
        tiles_per_batch = nt // mod.shape[0]
        gate_spec = pl.BlockSpec((None, 1, D_MODEL), lambda i: (i // tiles_per_batch, 0, 2))
    return pl.pallas_call(
        _out_kernel,
        grid=(nt,),
        in_specs=[
            pl.BlockSpec((tm, W_A), lambda i: (i, 0)),
            pl.BlockSpec((tm, W_B), lambda i: (i, 0)),
            pl.BlockSpec((W_A + W_B, D_MODEL), lambda i: (0, 0)),
            pl.BlockSpec((tm, D_MODEL), lambda i: (i, 0)),
            gate_spec,
        ],
        out_specs=pl.BlockSpec((tm, D_MODEL), lambda i: (i, 0)),
        out_shape=jax.ShapeDtypeStruct((m_rows, D_MODEL), F32),
        compiler_params=pltpu.CompilerParams(
            dimension_semantics=("arbitrary",), vmem_limit_bytes=VMEM_LIMIT),
        name="out_rows" if per_row else "out_bcast",
    )(mix_a, mix_b, w_out, x2d, mod)


def kernel(x_prompt, x_sample, c_prompt, c_sample, cache_k, cache_v, state_C, state_n, state_m,
           page_table, norm_g, w_ada, b_ada, w_in, b_in, g_q, g_k, lam_q, lam_k, gn_a, gn_b, w_out):
    assert w_in.shape[0] == 1, "single-layer model"
    nb, seq, _ = x_prompt.shape
    nreq = x_sample.shape[0]
    assert x_sample.shape[1] == 1

    w = w_in[0]
    bvec = b_in[0]
    gate_lo = N_MAIN
    gb_lo = N_MAIN + 2 * H_B
    wm = w[:, :N_MAIN].astype(BF16)
    wif = jnp.pad(w[:, gate_lo:gb_lo], ((0, 0), (0, GATE_PAD - 2 * H_B))).astype(BF16)
    wgb = w[:, gb_lo:].astype(BF16)
    bm = bvec[None, :N_MAIN]
    bif = jnp.pad(bvec[gate_lo:gb_lo], (0, GATE_PAD - 2 * H_B))[None, :]
    bgb = bvec[None, gb_lo:]
    gq = jnp.tile(g_q[0], 2 * H_A)[None, :]
    gk = jnp.tile(g_k[0], 2 * H_A)[None, :]
    grp = jnp.arange(A_QK) // DK_A
    seg = (grp[:, None] == grp[None, :]).astype(BF16)
    weights = (norm_g[0][None, :], wm, wif, wgb, bm, bif, bgb, gq, gk, seg)
    gna = gn_a[0].reshape(1, W_A)
    gnb = gn_b[0].reshape(1, W_B)
    wo = w_out[0].astype(BF16)

    pad_rows = (-(nb + nreq)) % 8
    c_all = jnp.concatenate([c_prompt, c_sample, jnp.zeros((pad_rows, D_MODEL), F32)], axis=0)
    mod, lam_tile = _ada(c_all, w_ada[0], b_ada[0][None, :], lam_q[0], lam_k[0])
    mod_p = mod[:nb].reshape(nb, 1, 3 * D_MODEL)
    mod_s = mod[nb:nb + nreq]

    xp = x_prompt.reshape(nb * seq, D_MODEL)
    (q, kf, kb, vf, vb, ga, mq, mk, mv, mo, zif, gb) = _proj(xp, mod_p, False, PROJ_TM, weights)
    xs = x_sample.reshape(nreq, D_MODEL)
    (qs, kfs, kbs, vfs, vbs, gas, mqs, mks, mvs, mos, zifs, gbs) = _proj(xs, mod_s, True, nreq, weights)

    mix_a = _attn_prompt(q, kb, vb, ga, gna, lam_tile, nb, seq)
    ck = jnp.transpose(cache_k[0], (0, 2, 3, 4, 1)).reshape(cache_k.shape[1], A_QK, PAGE)
    cv = cache_v[0].reshape(cache_v.shape[1], PAGE * H_A, DV_A)
    r3 = lambda a: a.reshape(nreq, 1, a.shape[-1])
    mix_as, mix_b, c_p, n_p, m_p = _paged_attn_mlstm(
        page_table, r3(qs), r3(kbs), r3(vbs), r3(gas), gna, lam_tile, ck, cv,
        mq, mk, mv, mo, gb, zif, gnb, nb, seq)
    mix_bs, c_s, n_s, m_s = _mlstm_step(mqs, mks, mvs, mos, gbs, zifs, state_C[0],
                                        state_n[0].reshape(nreq, H_B * DK_B), state_m[0], gnb)

    y_p = _out_proj(mix_a, mix_b, wo, xp, mod_p, False, OUT_TM)
    y_s = _out_proj(mix_as.reshape(nreq, W_A), mix_bs, wo, xs, mod_s, True, nreq)

    return (
        y_p.reshape(nb, seq, D_MODEL),
        y_s.reshape(nreq, 1, D_MODEL),
        jnp.transpose(kf.reshape(nb, H_A, 2, DK_A, seq), (0, 4, 1, 2, 3))[None],
        vf.reshape(1, nb, seq, H_A, DV_A),
        kfs.reshape(1, nreq, 1, H_A, 2, DK_A),
        vfs.reshape(1, nreq, 1, H_A, DV_A),
        c_p[None],
        n_p[None],
        m_p[:, :, 0][None],
        c_s[None],
        n_s.reshape(1, nreq, H_B, DK_B),
        m_s[:, :H_B][None],
    )
```

```python
import functools
import math

import jax
import jax.numpy as jnp
from jax import lax
from jax.experimental import pallas as pl
from jax.experimental.pallas import tpu as pltpu

F32 = jnp.float32
BF16 = jnp.bfloat16

D_MODEL = 1024
H_A = 4
assert H_A & (H_A - 1) == 0
DK_A = 64
DV_A = 128
A_QK = H_A * 2 * DK_A
W_A = H_A * DV_A
H_B = 4
DK_B = 128
DV_B = 128
W_B = H_B * DV_B
N_MAIN = 8 * 512
GATE_PAD = 128
PAGE = 128
CHUNK = 128
EPS = 1e-6
NEG = -1e30
LAM_INIT = 0.8 - 0.6 * math.exp(-0.3 * 0)
ALIBI_SLOPES = tuple(2.0 ** (-8.0 * (h + 1) / H_A) for h in range(H_A))
LOG2E = math.log2(math.e)
POS_RADIX = 64

VMEM_LIMIT = 56 * 1024 * 1024

PROJ_TM = 512
OUT_TM = 1024
ATT_T = 512
PAGES_PER_STEP = 32
STEP_RB = 8


def _nt_dot(a, b):
    return lax.dot_general(a, b, (((1,), (1,)), ((), ())), preferred_element_type=F32)


def _dot(a, b):
    return jnp.dot(a, b, preferred_element_type=F32)


def _sigmoid(x):
    return 1.0 / (1.0 + jnp.exp(-x))


def _silu(x):
    return x * _sigmoid(x)


def _log_sigmoid(x):
    return jnp.minimum(x, 0.0) - jnp.log1p(jnp.exp(-jnp.abs(x)))


def _split3(a):
    a1 = a.astype(BF16)
    r1 = a - a1.astype(F32)
    a2 = r1.astype(BF16)
    a3 = (r1 - a2.astype(F32)).astype(BF16)
    return a1, a2, a3


def _ada_kernel(c_ref, w_ref, b_ref, lq_ref, lk_ref, mod_ref, lam_ref):
    a = _silu(c_ref[...])
    w = w_ref[...]
    a1, a2, _ = _split3(a)
    w1, w2, _ = _split3(w)
    mod_ref[...] = (_dot(a1, w1) + (_dot(a1, w2) + _dot(a2, w1))) + b_ref[...]

    @pl.when(pl.program_id(0) == 0)
    def _():
        s = jnp.sum(lq_ref[...] * lk_ref[...], axis=1, keepdims=True)
        e = jnp.exp(s)
        lam = e[0:1, :] - e[1:2, :] + LAM_INIT
        lam_ref[...] = jnp.broadcast_to(lam, lam_ref.shape)


def _ada(c_all, w_ada, b_ada, lam_q, lam_k):
    rows = c_all.shape[0]
    nblk = 3
    return pl.pallas_call(
        _ada_kernel,
        grid=(nblk,),
        in_specs=[
            pl.BlockSpec((rows, D_MODEL), lambda j: (0, 0)),
            pl.BlockSpec((D_MODEL, D_MODEL), lambda j: (0, j)),
            pl.BlockSpec((1, D_MODEL), lambda j: (0, j)),
            pl.BlockSpec((2, DK_A), lambda j: (0, 0)),
            pl.BlockSpec((2, DK_A), lambda j: (0, 0)),
        ],
        out_specs=[
            pl.BlockSpec((rows, D_MODEL), lambda j: (0, j)),
            pl.BlockSpec((8, 128), lambda j: (0, 0)),
        ],
        out_shape=[
            jax.ShapeDtypeStruct((rows, 3 * D_MODEL), F32),
            jax.ShapeDtypeStruct((8, 128), F32),
        ],
        compiler_params=pltpu.CompilerParams(
            dimension_semantics=("arbitrary",), vmem_limit_bytes=VMEM_LIMIT),
        name="ada",
    )(c_all, w_ada, b_ada, lam_q, lam_k)


def _proj_kernel(x_ref, shift_ref, scale_ref, ng_ref, wm_ref, wif_ref, wgb_ref,
                 bm_ref, bif_ref, bgb_ref, gq_ref, gk_ref, seg_ref,
                 q_ref, kf_ref, kb_ref, vf_ref, vb_ref, ga_ref,
                 mq_ref, mk_ref, mv_ref, mo_ref, zif_ref, gb_ref, *, k_transposed):
    x = x_ref[...]
    ms = jnp.mean(x * x, axis=-1, keepdims=True)
    h = x * lax.rsqrt(ms + EPS) * ng_ref[...]
    h = h * (1.0 + scale_ref[...]) + shift_ref[...]
    hb = h.astype(BF16)

    def col(j):
        sl = slice(j * 512, (j + 1) * 512)
        return _dot(hb, wm_ref[:, sl]) + bm_ref[:, sl]

    def headnorm(z, g):
        ss = _dot((z * z).astype(BF16), seg_ref[...])
        return z * lax.rsqrt(ss * (1.0 / DK_A) + EPS) * g

    q_ref[...] = (headnorm(col(0), gq_ref[...]) * (DK_A ** -0.5 * LOG2E)).astype(BF16)
    k = headnorm(col(1), gk_ref[...])
    kf_ref[...] = k.T if k_transposed else k
    kb_ref[...] = k.astype(BF16)
    v = col(2)
    for hh in range(H_A):
        vf_ref[pl.ds(hh, x.shape[0], stride=H_A), :] = v[:, hh * DV_A:(hh + 1) * DV_A]
    vb_ref[...] = v.astype(BF16)
    ga_ref[...] = col(3).astype(BF16)
    mq_ref[...] = col(4).astype(BF16)
    mk_ref[...] = (col(5) * (DK_B ** -0.5)).astype(BF16)
    mv_ref[...] = col(6).astype(BF16)
    mo_ref[...] = _sigmoid(col(7)).astype(BF16)
    zif_ref[...] = _dot(hb, wif_ref[...]) + bif_ref[...]
    gb_ref[...] = (_dot(hb, wgb_ref[...]) + bgb_ref[...]).astype(BF16)


def _proj(x2d, mod, per_row, tm, weights):
    (ng, wm, wif, wgb, bm, bif, bgb, gq, gk, seg) = weights
    m_rows = x2d.shape[0]
    nt = m_rows // tm
    if per_row:
        shift_spec = pl.BlockSpec((tm, D_MODEL), lambda i: (i, 0))
        scale_spec = pl.BlockSpec((tm, D_MODEL), lambda i: (i, 1))
    else:
        tiles_per_batch = nt // mod.shape[0]
        shift_spec = pl.BlockSpec((None, 1, D_MODEL), lambda i: (i // tiles_per_batch, 0, 0))
        scale_spec = pl.BlockSpec((None, 1, D_MODEL), lambda i: (i // tiles_per_batch, 0, 1))

    def const(shape):
        return pl.BlockSpec(shape, lambda i: (0, 0))

    def rows(width):
        return pl.BlockSpec((tm, width), lambda i: (i, 0))

    out_dtypes = [BF16, F32, BF16, F32, BF16, BF16, BF16, BF16, BF16, BF16, F32, BF16]
    out_widths = [512, 512, 512, DV_A, 512, 512, 512, 512, 512, 512, GATE_PAD, 512]
    out_rows = [1, 1, 1, H_A, 1, 1, 1, 1, 1, 1, 1, 1]
    out_specs = [pl.BlockSpec((tm * r, w), lambda i: (i, 0)) for w, r in zip(out_widths, out_rows)]
    out_shape = [jax.ShapeDtypeStruct((m_rows * r, w), d)
                 for w, r, d in zip(out_widths, out_rows, out_dtypes)]
    if not per_row:
        out_specs[1] = pl.BlockSpec((None, A_QK, tm),
                                    lambda i: (i // tiles_per_batch, 0, i % tiles_per_batch))
        out_shape[1] = jax.ShapeDtypeStruct((mod.shape[0], A_QK, m_rows // mod.shape[0]), F32)
    return pl.pallas_call(
        functools.partial(_proj_kernel, k_transposed=not per_row),
        grid=(nt,),
        in_specs=[
            rows(D_MODEL), shift_spec, scale_spec, const((1, D_MODEL)),
            const((D_MODEL, N_MAIN)), const((D_MODEL, GATE_PAD)), const((D_MODEL, W_B)),
            const((1, N_MAIN)), const((1, GATE_PAD)), const((1, W_B)),
            const((1, A_QK)), const((1, A_QK)), const((A_QK, A_QK)),
        ],
        out_specs=out_specs,
        out_shape=out_shape,
        compiler_params=pltpu.CompilerParams(
            dimension_semantics=("arbitrary",), vmem_limit_bytes=VMEM_LIMIT),
        name="proj_rows" if per_row else "proj_bcast",
    )(x2d, mod, mod, ng, wm, wif, wgb, bm, bif, bgb, gq, gk, seg)


def _diff_norm_gate(o0, o1, lam, gna, ga):
    d = o0 - lam * o1
    ya = d * lax.rsqrt(jnp.mean(d * d, axis=-1, keepdims=True) + EPS) * gna
    ya = ya * (1.0 - LAM_INIT)
    return ya * _silu(ga)


def _mlstm_out_gate(hh, o, gnb, gb):
    hg = o * hh
    yb = hg * lax.rsqrt(jnp.mean(hg * hg, axis=-1, keepdims=True) + EPS) * gnb
    return yb * _silu(gb)


def _alibi_features(seq):
    pos = jnp.arange(seq, dtype=jnp.int32)
    digits = [(pos // POS_RADIX).astype(F32), (pos % POS_RADIX).astype(F32)]
    ones = jnp.ones((seq,), F32)
    qf, kf = [], []
    for slope in ALIBI_SLOPES:
        pieces = [p.astype(F32) for p in _split3(jnp.float32(slope * LOG2E))]
        qcols, kcols = [], []
        for c in pieces:
            qcols += [POS_RADIX * c * ones, c * ones]
            kcols += digits
        for c in pieces:
            qcols += digits
            kcols += [-POS_RADIX * c * ones, -c * ones]
        pad = ((0, 0), (0, 128 - len(qcols)))
        qf.append(jnp.pad(jnp.stack(qcols, axis=1), pad))
        kf.append(jnp.pad(jnp.stack(kcols, axis=1), pad))
    return jnp.stack(qf).astype(BF16), jnp.stack(kf).astype(BF16)


def _attn_kernel(q_ref, qf_ref, k_ref, kf_ref, v_ref, ga_ref, gna_ref, lam_ref, out_ref, lhs_s, s_s):
    t = ATT_T
    qi = pl.program_id(2)

    q = q_ref[...].astype(F32)
    lane = lax.broadcasted_iota(jnp.int32, (t, 2 * DK_A), 1)
    lhs_s[0:t, 0:128] = jnp.where(lane < DK_A, q, 0.0).astype(BF16)
    lhs_s[t:2 * t, 0:128] = jnp.where(lane >= DK_A, q, 0.0).astype(BF16)
    lhs_s[0:t, 128:256] = qf_ref[...]
    lhs_s[t:2 * t, 128:256] = qf_ref[...]

    krow = lax.broadcasted_iota(jnp.int32, (t, 2 * t), 0)
    qcol = lax.broadcasted_iota(jnp.int32, (t, 2 * t), 1)
    future = krow > jnp.where(qcol >= t, qcol - t, qcol)

    def scores(kj, slot):
        off = pl.multiple_of(kj * t, t)
        kx = jnp.concatenate([k_ref[pl.ds(off, t), :], kf_ref[pl.ds(off, t), :]], axis=1)
        s_s[slot] = _nt_dot(kx, lhs_s[...])

    def consume(kj, slot, carry, masked):
        m, l, acc = carry
        off = pl.multiple_of(kj * t, t)
        s = s_s[slot]
        if masked:
            s = jnp.where(future, NEG, s)
        m_new = jnp.maximum(m, jnp.max(s, axis=0, keepdims=True))
        alpha = jnp.exp2(m - m_new)
        p = jnp.exp2(s - m_new)
        l = alpha * l + jnp.sum(p, axis=0, keepdims=True)
        pv = lax.dot_general(v_ref[pl.ds(off, t), :], p.astype(BF16), (((0,), (0,)), ((), ())),
                             preferred_element_type=F32)
        return m_new, l, alpha * acc + pv

    def pair(i, carry):
        scores(2 * i + 1, 1)
        carry = consume(2 * i, 0, carry, False)
        scores(2 * i + 2, 0)
        return consume(2 * i + 1, 1, carry, False)

    def tail_even(carry):
        return consume(qi, 0, carry, True)

    def tail_odd(carry):
        scores(qi, 1)
        return consume(qi, 1, consume(qi - 1, 0, carry, False), True)

    init = (jnp.full((1, 2 * t), NEG, F32), jnp.zeros((1, 2 * t), F32), jnp.zeros((DV_A, 2 * t), F32))
    scores(0, 0)
    carry = lax.fori_loop(0, qi // 2, pair, init)
    m, l, acc = lax.cond(qi % 2 == 1, tail_odd, tail_even, carry)
    o = (acc / l).T
    lam = lam_ref[0:1, :]
    out = _diff_norm_gate(o[:t], o[t:], lam, gna_ref[...], ga_ref[...].astype(F32))
    out_ref[...] = out.astype(BF16)


def _attn_prompt(q, k, v, ga, gna, lam_tile, nb, seq):
    nq = seq // ATT_T
    qf, kf = _alibi_features(seq)
    return pl.pallas_call(
        _attn_kernel,
        grid=(nb, H_A, nq),
        in_specs=[
            pl.BlockSpec((ATT_T, 128), lambda b, h, i: (b * nq + i, h)),
            pl.BlockSpec((None, ATT_T, 128), lambda b, h, i: (h, i, 0)),
            pl.BlockSpec((seq, 128), lambda b, h, i: (b, h)),
            pl.BlockSpec((None, seq, 128), lambda b, h, i: (h, 0, 0)),
            pl.BlockSpec((seq, 128), lambda b, h, i: (b, h)),
            pl.BlockSpec((ATT_T, 128), lambda b, h, i: (b * nq + i, h)),
            pl.BlockSpec((1, 128), lambda b, h, i: (0, h)),
            pl.BlockSpec((8, 128), lambda b, h, i: (0, 0)),
        ],
        out_specs=pl.BlockSpec((ATT_T, 128), lambda b, h, i: (b * nq + i, h)),
        out_shape=jax.ShapeDtypeStruct((nb * seq, W_A), BF16),
        scratch_shapes=[pltpu.VMEM((2 * ATT_T, 256), BF16),
                        pltpu.VMEM((2, ATT_T, 2 * ATT_T), F32)],
        compiler_params=pltpu.CompilerParams(
            dimension_semantics=("arbitrary", "arbitrary", "arbitrary"),
            vmem_limit_bytes=VMEM_LIMIT),
        name="attn_prompt",
    )(q, qf, k, kf, v, ga, gna, lam_tile)


def _mlstm_chunk(first, heads, gates_cached, q_ref, k_ref, v_ref, o_ref, gb_ref, zif_ref, gnb_ref,
                 y_ref, c_ref, n_ref, m_ref, bcum_s, rows_s):
    L = CHUNK

    x = zif_ref[...]
    row = lax.broadcasted_iota(jnp.int32, (L, L), 0)
    colv = lax.broadcasted_iota(jnp.int32, (L, L), 1)
    causal = row >= colv
    if not gates_cached:
        logf = _log_sigmoid(x)
        tri = jnp.where(causal, 1.0, 0.0).astype(BF16)
        f1, f2, f3 = _split3(logf)
        bcum_s[...] = _dot(tri, f1) + (_dot(tri, f2) + _dot(tri, f3))
        rows_s[0:8, :] = x.T[0:8, :]
        rows_s[8:16, :] = bcum_s[...].T[0:8, :]
    bmat = bcum_s[...]
    xt = rows_s[0:8, :]
    bt = rows_s[8:16, :]

    for h in heads:
        @pl.when(first)
        def _():
            c_ref[h] = jnp.zeros((DV_B, DK_B), F32)
            n_ref[h:h + 1, :] = jnp.zeros((1, DK_B), F32)
            m_ref[h:h + 1, :] = jnp.zeros((1, 128), F32)

        sl = slice(h * 128, (h + 1) * 128)
        qh = q_ref[:, sl]
        kh = k_ref[:, sl]
        vh = v_ref[:, sl]
        icol = x[:, h:h + 1]
        bcol = bmat[:, H_B + h:H_B + h + 1]
        irow = xt[h:h + 1, :]
        brow = bt[H_B + h:H_B + h + 1, :]
        m_prev = m_ref[h:h + 1, 0:1]
        nrow = n_ref[h:h + 1, :]
        c_old = c_ref[h]

        d = jnp.where(causal, (bcol - brow) + irow, NEG)
        inter = bcol + m_prev
        m_t = jnp.maximum(inter, jnp.max(d, axis=1, keepdims=True))
        w_intra = jnp.exp(d - m_t)
        w_inter = jnp.exp(inter - m_t)
        sw = _nt_dot(qh, kh) * w_intra
        num = _dot(sw.astype(BF16), vh) + w_inter * _nt_dot(qh, c_old.astype(BF16))
        den = jnp.sum(sw, axis=1, keepdims=True) \
            + w_inter * jnp.sum(qh.astype(F32) * nrow, axis=1, keepdims=True)
        hh = num / jnp.maximum(jnp.abs(den), jnp.exp(-m_t))
        y = _mlstm_out_gate(hh, o_ref[:, sl].astype(F32), gnb_ref[:, sl], gb_ref[:, sl].astype(F32))
        y_ref[:, sl] = y.astype(BF16)

        m_new = m_t[L - 1:L, :]
        b_last = bcol[L - 1:L, :]
        w_s = jnp.exp(((b_last - bcol) + icol) - m_new)
        decay = jnp.exp((b_last + m_prev) - m_new)
        wv = w_s * vh.astype(F32)
        c_ref[h] = decay * c_old + _dot(wv.T.astype(BF16), kh)
        n_ref[h:h + 1, :] = decay * nrow + jnp.sum(w_s * kh.astype(F32), axis=0, keepdims=True)
        m_ref[h:h + 1, :] = jnp.broadcast_to(m_new, (1, 128))


N_MLSTM_IN = 7
N_MLSTM_OUT = 4
MLSTM_HEAD_GROUPS = ((0, 1), (2, 3))


def _paged_kernel(pt_ref, q_ref, kn_ref, vn_ref, ga_ref, gna_ref, lam_ref, exp_ref, *rest,
                  n_chunk_steps, chunks_per_seq):
    pp = PAGES_PER_STEP
    k_refs = rest[:pp]
    v_refs = rest[pp:2 * pp]
    mlstm_in = rest[2 * pp:2 * pp + N_MLSTM_IN]
    out_ref = rest[2 * pp + N_MLSTM_IN]
    mlstm_out = rest[2 * pp + N_MLSTM_IN + 1:2 * pp + N_MLSTM_IN + 1 + N_MLSTM_OUT]
    kb_s, vb_s, m_s, l_s, acc_s, bcum_s, rows_s = rest[2 * pp + N_MLSTM_IN + 1 + N_MLSTM_OUT:]
    g = pl.program_id(1)
    ng = pl.num_programs(1)
    ntok = pp * PAGE

    step = pl.program_id(0) * ng + g

    ngroups = len(MLSTM_HEAD_GROUPS)
    first = (step // ngroups) % chunks_per_seq == 0
    for gi, heads in enumerate(MLSTM_HEAD_GROUPS):
        @pl.when(jnp.logical_and(step < n_chunk_steps, step % ngroups == gi))
        def _():
            _mlstm_chunk(first, heads, gi > 0, *mlstm_in, *mlstm_out, bcum_s, rows_s)

    @pl.when(g == 0)
    def _():
        m_s[...] = jnp.full_like(m_s, NEG)
        l_s[...] = jnp.zeros_like(l_s)
        acc_s[...] = jnp.zeros_like(acc_s)

    vrows = PAGE * H_A
    for i in range(pp):
        kb_s[:, i * PAGE:(i + 1) * PAGE] = k_refs[i][...].astype(BF16)
        vb_s[i * vrows:(i + 1) * vrows, :] = v_refs[i][...].astype(BF16)

    q = q_ref[...].astype(F32)
    sub = lax.broadcasted_iota(jnp.int32, (8, A_QK), 0)
    lane = lax.broadcasted_iota(jnp.int32, (8, A_QK), 1)
    qbd32 = jnp.where((lane >> 6) == sub, jnp.broadcast_to(q, (8, A_QK)), 0.0)
    qbd = qbd32.astype(BF16)

    j = lax.broadcasted_iota(jnp.int32, (8, 1), 0)
    slope = jnp.where(j < 2, ALIBI_SLOPES[0] * LOG2E,
                      jnp.where(j < 4, ALIBI_SLOPES[1] * LOG2E,
                                jnp.where(j < 6, ALIBI_SLOPES[2] * LOG2E,
                                          ALIBI_SLOPES[3] * LOG2E))).astype(F32)
    kpos = g * ntok + lax.broadcasted_iota(jnp.int32, (1, ntok), 1)
    dist = (ng * ntok - kpos).astype(F32)

    s = _dot(qbd, kb_s[...]) - slope * dist
    m_old = m_s[:, 0:1]
    l_old = l_s[:, 0:1]
    m_new = jnp.maximum(m_old, jnp.max(s, axis=1, keepdims=True))
    alpha = jnp.exp2(m_old - m_new)
    p = jnp.exp2(s - m_new)
    l_new = alpha * l_old + jnp.sum(p, axis=1, keepdims=True)
    pst = jnp.concatenate([p[:, i * PAGE:(i + 1) * PAGE] for i in range(pp)], axis=0)
    spread = _dot(pst.astype(BF16), exp_ref[...])
    own_head = (lax.broadcasted_iota(jnp.int32, (8, vrows), 1) & (H_A - 1)) \
        == (lax.broadcasted_iota(jnp.int32, (8, vrows), 0) >> 1)
    p4 = jnp.concatenate([jnp.where(own_head, spread[8 * i:8 * (i + 1), :], 0.0) for i in range(pp)],
                         axis=1)
    acc = alpha * acc_s[...] + _dot(p4.astype(BF16), vb_s[...])
    head_of_row = lax.broadcasted_iota(jnp.int32, (8, DV_A), 0) >> 1
    m_s[...] = jnp.broadcast_to(m_new, m_s.shape)
    l_s[...] = jnp.broadcast_to(l_new, l_s.shape)
    acc_s[...] = acc

    @pl.when(g == ng - 1)
    def _():
        kn = kn_ref[...].astype(F32)
        s_new = jnp.sum(qbd32 * kn, axis=1, keepdims=True)
        m_fin = jnp.maximum(m_new, s_new)
        a2 = jnp.exp2(m_new - m_fin)
        p_new = jnp.exp2(s_new - m_fin)
        l_fin = a2 * l_new + p_new
        vn = jnp.zeros((8, DV_A), F32)
        for h in range(H_A):
            vrow = vn_ref[:, h * DV_A:(h + 1) * DV_A].astype(F32)
            vn = jnp.where(head_of_row == h, jnp.broadcast_to(vrow, (8, DV_A)), vn)
        o = (a2 * acc + p_new * vn) / l_fin
        lam = lam_ref[0:1, :]
        for h in range(H_A):
            sl = slice(h * DV_A, (h + 1) * DV_A)
            out = _diff_norm_gate(o[2 * h:2 * h + 1, :], o[2 * h + 1:2 * h + 2, :], lam,
                                  gna_ref[:, sl], ga_ref[:, sl].astype(F32))
            out_ref[:, sl] = out.astype(BF16)


def _paged_attn_mlstm(page_table, q3, kn3, vn3, ga3, gna, lam_tile, ck, cv,
                      mq, mk, mv, mo, gb, zif, gnb, nb, seq):
    nreq, npages = page_table.shape
    pp = PAGES_PER_STEP
    ng = npages // pp
    nc = seq // CHUNK
    ngroups = len(MLSTM_HEAD_GROUPS)
    n_chunk_steps = nb * nc * ngroups
    assert n_chunk_steps <= nreq * ng, "every mLSTM (chunk, head group) needs a grid step"

    expand = (jnp.arange(PAGE * H_A)[None, :] // H_A == jnp.arange(PAGE)[:, None]).astype(BF16)

    def req(width):
        return pl.BlockSpec((None, 1, width), lambda r, g, pt: (r, 0, 0))

    def page(i):
        return pl.BlockSpec((None, 512, PAGE), lambda r, g, pt: (pt[r, g * pp + i], 0, 0))

    def chunk(r, g):
        return jnp.minimum(r * ng + g, n_chunk_steps - 1) // ngroups

    def crows(width):
        return pl.BlockSpec((CHUNK, width), lambda r, g, pt: (chunk(r, g), 0))

    def cstate(*shape):
        return pl.BlockSpec((None,) + shape, lambda r, g, pt: (chunk(r, g) // nc,) + (0,) * len(shape))

    return pl.pallas_call(
        functools.partial(_paged_kernel, n_chunk_steps=n_chunk_steps, chunks_per_seq=nc),
        grid_spec=pltpu.PrefetchScalarGridSpec(
            num_scalar_prefetch=1,
            grid=(nreq, ng),
            in_specs=[req(512), req(512), req(512), req(512),
                      pl.BlockSpec((1, W_A), lambda r, g, pt: (0, 0)),
                      pl.BlockSpec((8, 128), lambda r, g, pt: (0, 0)),
                      pl.BlockSpec((PAGE, PAGE * H_A), lambda r, g, pt: (0, 0))]
                     + [page(i) for i in range(pp)] + [page(i) for i in range(pp)]
                     + [crows(512), crows(512), crows(512), crows(512), crows(512), crows(GATE_PAD),
                        pl.BlockSpec((1, W_B), lambda r, g, pt: (0, 0))],
            out_specs=[req(512), crows(512), cstate(H_B, DV_B, DK_B), cstate(H_B, DK_B), cstate(H_B, 128)],
            scratch_shapes=[
                pltpu.VMEM((A_QK, pp * PAGE), BF16),
                pltpu.VMEM((pp * PAGE * H_A, DV_A), BF16),
                pltpu.VMEM((8, 128), F32),
                pltpu.VMEM((8, 128), F32),
                pltpu.VMEM((8, DV_A), F32),
                pltpu.VMEM((CHUNK, 128), F32),
                pltpu.VMEM((16, CHUNK), F32),
            ],
        ),
        out_shape=[
            jax.ShapeDtypeStruct((nreq, 1, W_A), BF16),
            jax.ShapeDtypeStruct((nb * seq, W_B), BF16),
            jax.ShapeDtypeStruct((nb, H_B, DV_B, DK_B), F32),
            jax.ShapeDtypeStruct((nb, H_B, DK_B), F32),
            jax.ShapeDtypeStruct((nb, H_B, 128), F32),
        ],
        compiler_params=pltpu.CompilerParams(
            dimension_semantics=("arbitrary", "arbitrary"), vmem_limit_bytes=VMEM_LIMIT),
        name="paged_attn_mlstm",
    )(page_table, q3, kn3, vn3, ga3, gna, lam_tile, expand, *([ck] * pp), *([cv] * pp),
      mq, mk, mv, mo, gb, zif, gnb)


def _mstep_kernel(q_ref, k_ref, v_ref, o_ref, gb_ref, zif_ref, c_ref, n_ref, m_ref, gnb_ref,
                  y_ref, co_ref, no_ref, mo_ref):
    rb = STEP_RB
    x = zif_ref[...]
    sub = lax.broadcasted_iota(jnp.int32, (rb, 128), 0)
    lane = lax.broadcasted_iota(jnp.int32, (rb, 128), 1)
    m_out = jnp.zeros((rb, 128), F32)
    for h in range(H_B):
        sl = slice(h * 128, (h + 1) * 128)
        qb = q_ref[:, sl]
        qh = qb.astype(F32)
        kh = k_ref[:, sl].astype(F32)
        vh = v_ref[:, sl].astype(F32)
        i_c = x[:, h:h + 1]
        b = _log_sigmoid(x[:, H_B + h:H_B + h + 1])
        m_prev = m_ref[:, h:h + 1]
        inter = b + m_prev
        m_t = jnp.maximum(inter, (b - b) + i_c)
        w_intra = jnp.exp(((b - b) + i_c) - m_t)
        w_inter = jnp.exp(inter - m_t)
        sw = jnp.sum(qh * kh, axis=1, keepdims=True) * w_intra
        cq = jnp.zeros((rb, 128), F32)
        for r in range(rb):
            res = _nt_dot(qb, c_ref[r, h].astype(BF16))
            cq = jnp.where(sub == r, res, cq)
        nh = n_ref[:, sl]
        num = sw * vh + w_inter * cq
        den = sw + w_inter * jnp.sum(nh * qh, axis=1, keepdims=True)
        hh = num / jnp.maximum(jnp.abs(den), jnp.exp(-m_t))
        y = _mlstm_out_gate(hh, o_ref[:, sl].astype(F32), gnb_ref[:, sl], gb_ref[:, sl].astype(F32))
        y_ref[:, sl] = y.astype(BF16)

        w_s = jnp.exp(((b - b) + i_c) - m_t)
        decay = jnp.exp((b + m_prev) - m_t)
        wv = w_s * vh
        for r in range(rb):
            vcol = jnp.broadcast_to(wv[r:r + 1, :], (DV_B, DK_B)).T
            co_ref[r, h] = decay[r:r + 1, :] * c_ref[r, h] + vcol * kh[r:r + 1, :]
        no_ref[:, sl] = decay * nh + w_s * kh
        m_out = jnp.where(lane == h, m_t, m_out)
    mo_ref[...] = m_out


def _mlstm_step(mq, mk, mv, mo, gb, zif, c0, n0, m0, gnb):
    nreq = mq.shape[0]
    rb = STEP_RB

    def rows(width):
        return pl.BlockSpec((rb, width), lambda i: (i, 0))

    cspec = pl.BlockSpec((rb, H_B, DV_B, DK_B), lambda i: (i, 0, 0, 0))
    return pl.pallas_call(
        _mstep_kernel,
        grid=(nreq // rb,),
        in_specs=[rows(512), rows(512), rows(512), rows(512), rows(512), rows(GATE_PAD),
                  cspec, rows(512), rows(H_B), pl.BlockSpec((1, W_B), lambda i: (0, 0))],
        out_specs=[rows(512), cspec, rows(512), rows(128)],
        out_shape=[
            jax.ShapeDtypeStruct((nreq, W_B), BF16),
            jax.ShapeDtypeStruct((nreq, H_B, DV_B, DK_B), F32),
            jax.ShapeDtypeStruct((nreq, H_B * DK_B), F32),
            jax.ShapeDtypeStruct((nreq, 128), F32),
        ],
        compiler_params=pltpu.CompilerParams(
            dimension_semantics=("arbitrary",), vmem_limit_bytes=VMEM_LIMIT),
        name="mlstm_step",
    )(mq, mk, mv, mo, gb, zif, c0, n0, m0, gnb)


def _out_kernel(ma_ref, mb_ref, w_ref, x_ref, gate_ref, y_ref):
    acc = _dot(ma_ref[...], w_ref[0:W_A, :]) + _dot(mb_ref[...], w_ref[W_A:W_A + W_B, :])
    y_ref[...] = x_ref[...] + gate_ref[...] * acc


def _out_proj(mix_a, mix_b, w_out, x2d, mod, per_row, tm):
    m_rows = x2d.shape[0]
    nt = m_rows // tm
    if per_row:
        gate_spec = pl.BlockSpec((tm, D_MODEL), lambda i: (i, 2))
    else:
        tiles_per_batch = nt // mod.shape[0]
        gate_spec = pl.BlockSpec((None, 1, D_MODEL), lambda i: (i // tiles_per_batch, 0, 2))
    return pl.pallas_call(
        _out_kernel,
        grid=(nt,),
        in_specs=[
            pl.BlockSpec((tm, W_A), lambda i: (i, 0)),
            pl.BlockSpec((tm, W_B), lambda i: (i, 0)),
            pl.BlockSpec((W_A + W_B, D_MODEL), lambda i: (0, 0)),
            pl.BlockSpec((tm, D_MODEL), lambda i: (i, 0)),
            gate_spec,
        ],
        out_specs=pl.BlockSpec((tm, D_MODEL), lambda i: (i, 0)),
        out_shape=jax.ShapeDtypeStruct((m_rows, D_MODEL), F32),
        compiler_params=pltpu.CompilerParams(
            dimension_semantics=("arbitrary",), vmem_limit_bytes=VMEM_LIMIT),
        name="out_rows" if per_row else "out_bcast",
    )(mix_a, mix_b, w_out, x2d, mod)


def kernel(x_prompt, x_sample, c_prompt, c_sample, cache_k, cache_v, state_C, state_n, state_m,
           page_table, norm_g, w_ada, b_ada, w_in, b_in, g_q, g_k, lam_q, lam_k, gn_a, gn_b, w_out):
    assert w_in.shape[0] == 1, "single-layer model"
    nb, seq, _ = x_prompt.shape
    nreq = x_sample.shape[0]
    assert x_sample.shape[1] == 1

    w = w_in[0]
    bvec = b_in[0]
    gate_lo = N_MAIN
    gb_lo = N_MAIN + 2 * H_B
    wm = w[:, :N_MAIN].astype(BF16)
    wif = jnp.pad(w[:, gate_lo:gb_lo], ((0, 0), (0, GATE_PAD - 2 * H_B))).astype(BF16)
    wgb = w[:, gb_lo:].astype(BF16)
    bm = bvec[None, :N_MAIN]
    bif = jnp.pad(bvec[gate_lo:gb_lo], (0, GATE_PAD - 2 * H_B))[None, :]
    bgb = bvec[None, gb_lo:]
    gq = jnp.tile(g_q[0], 2 * H_A)[None, :]
    gk = jnp.tile(g_k[0], 2 * H_A)[None, :]
    grp = jnp.arange(A_QK) // DK_A
    seg = (grp[:, None] == grp[None, :]).astype(BF16)
    weights = (norm_g[0][None, :], wm, wif, wgb, bm, bif, bgb, gq, gk, seg)
    gna = gn_a[0].reshape(1, W_A)
    gnb = gn_b[0].reshape(1, W_B)
    wo = w_out[0].astype(BF16)

    pad_rows = (-(nb + nreq)) % 8
    c_all = jnp.concatenate([c_prompt, c_sample, jnp.zeros((pad_rows, D_MODEL), F32)], axis=0)
    mod, lam_tile = _ada(c_all, w_ada[0], b_ada[0][None, :], lam_q[0], lam_k[0])
    mod_p = mod[:nb].reshape(nb, 1, 3 * D_MODEL)
    mod_s = mod[nb:nb + nreq]

    xp = x_prompt.reshape(nb * seq, D_MODEL)
    (q, kf, kb, vf, vb, ga, mq, mk, mv, mo, zif, gb) = _proj(xp, mod_p, False, PROJ_TM, weights)
    xs = x_sample.reshape(nreq, D_MODEL)
    (qs, kfs, kbs, vfs, vbs, gas, mqs, mks, mvs, mos, zifs, gbs) = _proj(xs, mod_s, True, nreq, weights)

    mix_a = _attn_prompt(q, kb, vb, ga, gna, lam_tile, nb, seq)
    ck = jnp.transpose(cache_k[0], (0, 2, 3, 4, 1)).reshape(cache_k.shape[1], A_QK, PAGE)
    cv = cache_v[0].reshape(cache_v.shape[1], PAGE * H_A, DV_A)
    r3 = lambda a: a.reshape(nreq, 1, a.shape[-1])
    mix_as, mix_b, c_p, n_p, m_p = _paged_attn_mlstm(
        page_table, r3(qs), r3(kbs), r3(vbs), r3(gas), gna, lam_tile, ck, cv,
        mq, mk, mv, mo, gb, zif, gnb, nb, seq)
    mix_bs, c_s, n_s, m_s = _mlstm_step(mqs, mks, mvs, mos, gbs, zifs, state_C[0],
                                        state_n[0].reshape(nreq, H_B * DK_B), state_m[0], gnb)

    y_p = _out_proj(mix_a, mix_b, wo, xp, mod_p, False, OUT_TM)
    y_s = _out_proj(mix_as.reshape(nreq, W_A), mix_bs, wo, xs, mod_s, True, nreq)

    return (
        y_p.reshape(nb, seq, D_MODEL),
        y_s.reshape(nreq, 1, D_MODEL),
        jnp.transpose(kf.reshape(nb, H_A, 2, DK_A, seq), (0, 4, 1, 2, 3))[None],
        vf.reshape(1, nb, seq, H_A, DV_A),
        kfs.reshape(1, nreq, 1, H_A, 2, DK_A),
        vfs.reshape(1, nreq, 1, H_A, DV_A),
        c_p[None],
        n_p[None],
        m_p[:, :, 0][None],
        c_s[None],
        n_s.reshape(1, nreq, H_B, DK_B),
        m_s[:, :H_B][None],
    )
```

```python
import functools
import math

import jax
import jax.numpy as jnp
from jax import lax
from jax.experimental import pallas as pl
from jax.experimental.pallas import tpu as pltpu

F32 = jnp.float32
BF16 = jnp.bfloat16

D_MODEL = 1024
H_A = 4
assert H_A & (H_A - 1) == 0
DK_A = 64
DV_A = 128
A_QK = H_A * 2 * DK_A
W_A = H_A * DV_A
H_B = 4
DK_B = 128
DV_B = 128
W_B = H_B * DV_B
N_MAIN = 8 * 512
GATE_PAD = 128
PAGE = 128
CHUNK = 128
EPS = 1e-6
NEG = -1e30
LAM_INIT = 0.8 - 0.6 * math.exp(-0.3 * 0)
ALIBI_SLOPES = tuple(2.0 ** (-8.0 * (h + 1) / H_A) for h in range(H_A))
LOG2E = math.log2(math.e)
POS_RADIX = 64

VMEM_LIMIT = 56 * 1024 * 1024

PROJ_TM = 512
OUT_TM = 1024
ATT_T = 512
PAGES_PER_STEP = 32
STEP_RB = 8


def _nt_dot(a, b):
    return lax.dot_general(a, b, (((1,), (1,)), ((), ())), preferred_element_type=F32)


def _dot(a, b):
    return jnp.dot(a, b, preferred_element_type=F32)


def _sigmoid(x):
    return 1.0 / (1.0 + jnp.exp(-x))


def _silu(x):
    return x * _sigmoid(x)


def _log_sigmoid(x):
    return jnp.minimum(x, 0.0) - jnp.log1p(jnp.exp(-jnp.abs(x)))


def _split3(a):
    a1 = a.astype(BF16)
    r1 = a - a1.astype(F32)
    a2 = r1.astype(BF16)
    a3 = (r1 - a2.astype(F32)).astype(BF16)
    return a1, a2, a3


def _ada_kernel(c_ref, w_ref, b_ref, lq_ref, lk_ref, mod_ref, lam_ref):
    a = _silu(c_ref[...])
    w = w_ref[...]
    a1, a2, _ = _split3(a)
    w1, w2, _ = _split3(w)
    mod_ref[...] = (_dot(a1, w1) + (_dot(a1, w2) + _dot(a2, w1))) + b_ref[...]

    @pl.when(pl.program_id(0) == 0)
    def _():
        s = jnp.sum(lq_ref[...] * lk_ref[...], axis=1, keepdims=True)
        e = jnp.exp(s)
        lam = e[0:1, :] - e[1:2, :] + LAM_INIT
        lam_ref[...] = jnp.broadcast_to(lam, lam_ref.shape)


def _ada(c_all, w_ada, b_ada, lam_q, lam_k):
    rows = c_all.shape[0]
    nblk = 3
    return pl.pallas_call(
        _ada_kernel,
        grid=(nblk,),
        in_specs=[
            pl.BlockSpec((rows, D_MODEL), lambda j: (0, 0)),
            pl.BlockSpec((D_MODEL, D_MODEL), lambda j: (0, j)),
            pl.BlockSpec((1, D_MODEL), lambda j: (0, j)),
            pl.BlockSpec((2, DK_A), lambda j: (0, 0)),
            pl.BlockSpec((2, DK_A), lambda j: (0, 0)),
        ],
        out_specs=[
            pl.BlockSpec((rows, D_MODEL), lambda j: (0, j)),
            pl.BlockSpec((8, 128), lambda j: (0, 0)),
        ],
        out_shape=[
            jax.ShapeDtypeStruct((rows, 3 * D_MODEL), F32),
            jax.ShapeDtypeStruct((8, 128), F32),
        ],
        compiler_params=pltpu.CompilerParams(
            dimension_semantics=("arbitrary",), vmem_limit_bytes=VMEM_LIMIT),
        name="ada",
    )(c_all, w_ada, b_ada, lam_q, lam_k)


def _proj_kernel(x_ref, shift_ref, scale_ref, ng_ref, wm_ref, wif_ref, wgb_ref,
                 bm_ref, bif_ref, bgb_ref, gq_ref, gk_ref, seg_ref,
                 q_ref, kf_ref, kb_ref, vf_ref, vb_ref, ga_ref,
                 mq_ref, mk_ref, mv_ref, mo_ref, zif_ref, gb_ref, *, k_transposed):
    x = x_ref[...]
    ms = jnp.mean(x * x, axis=-1, keepdims=True)
    h = x * lax.rsqrt(ms + EPS) * ng_ref[...]
    h = h * (1.0 + scale_ref[...]) + shift_ref[...]
    hb = h.astype(BF16)

    def col(j):
        sl = slice(j * 512, (j + 1) * 512)
        return _dot(hb, wm_ref[:, sl]) + bm_ref[:, sl]

    def headnorm(z, g):
        ss = _dot((z * z).astype(BF16), seg_ref[...])
        return z * lax.rsqrt(ss * (1.0 / DK_A) + EPS) * g

    q_ref[...] = (headnorm(col(0), gq_ref[...]) * (DK_A ** -0.5 * LOG2E)).astype(BF16)
    k = headnorm(col(1), gk_ref[...])
    kf_ref[...] = k.T if k_transposed else k
    kb_ref[...] = k.astype(BF16)
    v = col(2)
    for hh in range(H_A):
        vf_ref[pl.ds(hh, x.shape[0], stride=H_A), :] = v[:, hh * DV_A:(hh + 1) * DV_A]
    vb_ref[...] = v.astype(BF16)
    ga_ref[...] = col(3).astype(BF16)
    mq_ref[...] = col(4).astype(BF16)
    mk_ref[...] = (col(5) * (DK_B ** -0.5)).astype(BF16)
    mv_ref[...] = col(6).astype(BF16)
    mo_ref[...] = _sigmoid(col(7)).astype(BF16)
    zif_ref[...] = _dot(hb, wif_ref[...]) + bif_ref[...]
    gb_ref[...] = (_dot(hb, wgb_ref[...]) + bgb_ref[...]).astype(BF16)


def _proj(x2d, mod, per_row, tm, weights):
    (ng, wm, wif, wgb, bm, bif, bgb, gq, gk, seg) = weights
    m_rows = x2d.shape[0]
    nt = m_rows // tm
    if per_row:
        shift_spec = pl.BlockSpec((tm, D_MODEL), lambda i: (i, 0))
        scale_spec = pl.BlockSpec((tm, D_MODEL), lambda i: (i, 1))
    else:
        tiles_per_batch = nt // mod.shape[0]
        shift_spec = pl.BlockSpec((None, 1, D_MODEL), lambda i: (i // tiles_per_batch, 0, 0))
        scale_spec = pl.BlockSpec((None, 1, D_MODEL), lambda i: (i // tiles_per_batch, 0, 1))

    def const(shape):
        return pl.BlockSpec(shape, lambda i: (0, 0))

    def rows(width):
        return pl.BlockSpec((tm, width), lambda i: (i, 0))

    out_dtypes = [BF16, F32, BF16, F32, BF16, BF16, BF16, BF16, BF16, BF16, F32, BF16]
    out_widths = [512, 512, 512, DV_A, 512, 512, 512, 512, 512, 512, GATE_PAD, 512]
    out_rows = [1, 1, 1, H_A, 1, 1, 1, 1, 1, 1, 1, 1]
    out_specs = [pl.BlockSpec((tm * r, w), lambda i: (i, 0)) for w, r in zip(out_widths, out_rows)]
    out_shape = [jax.ShapeDtypeStruct((m_rows * r, w), d)
                 for w, r, d in zip(out_widths, out_rows, out_dtypes)]
    if not per_row:
        out_specs[1] = pl.BlockSpec((None, A_QK, tm),
                                    lambda i: (i // tiles_per_batch, 0, i % tiles_per_batch))
        out_shape[1] = jax.ShapeDtypeStruct((mod.shape[0], A_QK, m_rows // mod.shape[0]), F32)
    return pl.pallas_call(
        functools.partial(_proj_kernel, k_transposed=not per_row),
        grid=(nt,),
        in_specs=[
            rows(D_MODEL), shift_spec, scale_spec, const((1, D_MODEL)),
            const((D_MODEL, N_MAIN)), const((D_MODEL, GATE_PAD)), const((D_MODEL, W_B)),
            const((1, N_MAIN)), const((1, GATE_PAD)), const((1, W_B)),
            const((1, A_QK)), const((1, A_QK)), const((A_QK, A_QK)),
        ],
        out_specs=out_specs,
        out_shape=out_shape,
        compiler_params=pltpu.CompilerParams(
            dimension_semantics=("arbitrary",), vmem_limit_bytes=VMEM_LIMIT),
        name="proj_rows" if per_row else "proj_bcast",
    )(x2d, mod, mod, ng, wm, wif, wgb, bm, bif, bgb, gq, gk, seg)


def _diff_norm_gate(o0, o1, lam, gna, ga):
    d = o0 - lam * o1
    ya = d * lax.rsqrt(jnp.mean(d * d, axis=-1, keepdims=True) + EPS) * gna
    ya = ya * (1.0 - LAM_INIT)
    return ya * _silu(ga)


def _mlstm_out_gate(hh, o, gnb, gb):
    hg = o * hh
    yb = hg * lax.rsqrt(jnp.mean(hg * hg, axis=-1, keepdims=True) + EPS) * gnb
    return yb * _silu(gb)


def _alibi_features(seq):
    pos = jnp.arange(seq, dtype=jnp.int32)
    digits = [(pos // POS_RADIX).astype(F32), (pos % POS_RADIX).astype(F32)]
    ones = jnp.ones((seq,), F32)
    qf, kf = [], []
    for slope in ALIBI_SLOPES:
        pieces = [p.astype(F32) for p in _split3(jnp.float32(slope * LOG2E))]
        qcols, kcols = [], []
        for c in pieces:
            qcols += [POS_RADIX * c * ones, c * ones]
            kcols += digits
        for c in pieces:
            qcols += digits
            kcols += [-POS_RADIX * c * ones, -c * ones]
        pad = ((0, 0), (0, 128 - len(qcols)))
        qf.append(jnp.pad(jnp.stack(qcols, axis=1), pad))
        kf.append(jnp.pad(jnp.stack(kcols, axis=1), pad))
    return jnp.stack(qf).astype(BF16), jnp.stack(kf).astype(BF16)


def _attn_kernel(q_ref, qf_ref, k_ref, kf_ref, v_ref, ga_ref, gna_ref, lam_ref, out_ref, lhs_s, s_s):
    t = ATT_T
    qi = pl.program_id(2)

    q = q_ref[...].astype(F32)
    lane = lax.broadcasted_iota(jnp.int32, (t, 2 * DK_A), 1)
    lhs_s[0:t, 0:128] = jnp.where(lane < DK_A, q, 0.0).astype(BF16)
    lhs_s[t:2 * t, 0:128] = jnp.where(lane >= DK_A, q, 0.0).astype(BF16)
    lhs_s[0:t, 128:256] = qf_ref[...]
    lhs_s[t:2 * t, 128:256] = qf_ref[...]

    krow = lax.broadcasted_iota(jnp.int32, (t, 2 * t), 0)
    qcol = lax.broadcasted_iota(jnp.int32, (t, 2 * t), 1)
    future = krow > jnp.where(qcol >= t, qcol - t, qcol)

    def scores(kj, slot):
        off = pl.multiple_of(kj * t, t)
        kx = jnp.concatenate([k_ref[pl.ds(off, t), :], kf_ref[pl.ds(off, t), :]], axis=1)
        s_s[slot] = _nt_dot(kx, lhs_s[...])

    def consume(kj, slot, carry, masked):
        m, l, acc = carry
        off = pl.multiple_of(kj * t, t)
        s = s_s[slot]
        if masked:
            s = jnp.where(future, NEG, s)
        m_new = jnp.maximum(m, jnp.max(s, axis=0, keepdims=True))
        alpha = jnp.exp2(m - m_new)
        p = jnp.exp2(s - m_new)
        l = alpha * l + jnp.sum(p, axis=0, keepdims=True)
        pv = lax.dot_general(v_ref[pl.ds(off, t), :], p.astype(BF16), (((0,), (0,)), ((), ())),
                             preferred_element_type=F32)
        return m_new, l, alpha * acc + pv

    def pair(i, carry):
        scores(2 * i + 1, 1)
        carry = consume(2 * i, 0, carry, False)
        scores(2 * i + 2, 0)
        return consume(2 * i + 1, 1, carry, False)

    def tail_even(carry):
        return consume(qi, 0, carry, True)

    def tail_odd(carry):
        scores(qi, 1)
        return consume(qi, 1, consume(qi - 1, 0, carry, False), True)

    init = (jnp.full((1, 2 * t), NEG, F32), jnp.zeros((1, 2 * t), F32), jnp.zeros((DV_A, 2 * t), F32))
    scores(0, 0)
    carry = lax.fori_loop(0, qi // 2, pair, init)
    m, l, acc = lax.cond(qi % 2 == 1, tail_odd, tail_even, carry)
    o = (acc / l).T
    lam = lam_ref[0:1, :]
    out = _diff_norm_gate(o[:t], o[t:], lam, gna_ref[...], ga_ref[...].astype(F32))
    out_ref[...] = out.astype(BF16)


def _attn_prompt(q, k, v, ga, gna, lam_tile, nb, seq):
    nq = seq // ATT_T
    qf, kf = _alibi_features(seq)
    return pl.pallas_call(
        _attn_kernel,
        grid=(nb, H_A, nq),
        in_specs=[
            pl.BlockSpec((ATT_T, 128), lambda b, h, i: (b * nq + i, h)),
            pl.BlockSpec((None, ATT_T, 128), lambda b, h, i: (h, i, 0)),
            pl.BlockSpec((seq, 128), lambda b, h, i: (b, h)),
            pl.BlockSpec((None, seq, 128), lambda b, h, i: (h, 0, 0)),
            pl.BlockSpec((seq, 128), lambda b, h, i: (b, h)),
            pl.BlockSpec((ATT_T, 128), lambda b, h, i: (b * nq + i, h)),
            pl.BlockSpec((1, 128), lambda b, h, i: (0, h)),
            pl.BlockSpec((8, 128), lambda b, h, i: (0, 0)),
        ],
        out_specs=pl.BlockSpec((ATT_T, 128), lambda b, h, i: (b * nq + i, h)),
        out_shape=jax.ShapeDtypeStruct((nb * seq, W_A), BF16),
        scratch_shapes=[pltpu.VMEM((2 * ATT_T, 256), BF16),
                        pltpu.VMEM((2, ATT_T, 2 * ATT_T), F32)],
        compiler_params=pltpu.CompilerParams(
            dimension_semantics=("arbitrary", "arbitrary", "arbitrary"),
            vmem_limit_bytes=VMEM_LIMIT),
        name="attn_prompt",
    )(q, qf, k, kf, v, ga, gna, lam_tile)


def _mlstm_chunk(first, heads, gates_cached, q_ref, k_ref, v_ref, o_ref, gb_ref, zif_ref, gnb_ref,
                 y_ref, c_ref, n_ref, m_ref, bcum_s, rows_s):
    L = CHUNK

    x = zif_ref[...]
    row = lax.broadcasted_iota(jnp.int32, (L, L), 0)
    colv = lax.broadcasted_iota(jnp.int32, (L, L), 1)
    causal = row >= colv
    if not gates_cached:
        logf = _log_sigmoid(x)
        tri = jnp.where(causal, 1.0, 0.0).astype(BF16)
        f1, f2, f3 = _split3(logf)
        bcum_s[...] = _dot(tri, f1) + (_dot(tri, f2) + _dot(tri, f3))
        rows_s[0:8, :] = x.T[0:8, :]
        rows_s[8:16, :] = bcum_s[...].T[0:8, :]
    bmat = bcum_s[...]
    xt = rows_s[0:8, :]
    bt = rows_s[8:16, :]

    for h in heads:
        @pl.when(first)
        def _():
            c_ref[h] = jnp.zeros((DV_B, DK_B), F32)
            n_ref[h:h + 1, :] = jnp.zeros((1, DK_B), F32)
            m_ref[h:h + 1, :] = jnp.zeros((1, 128), F32)

        sl = slice(h * 128, (h + 1) * 128)
        qh = q_ref[:, sl]
        kh = k_ref[:, sl]
        vh = v_ref[:, sl]
        icol = x[:, h:h + 1]
        bcol = bmat[:, H_B + h:H_B + h + 1]
        irow = xt[h:h + 1, :]
        brow = bt[H_B + h:H_B + h + 1, :]
        m_prev = m_ref[h:h + 1, 0:1]
        nrow = n_ref[h:h + 1, :]
        c_old = c_ref[h]

        d = jnp.where(causal, (bcol - brow) + irow, NEG)
        inter = bcol + m_prev
        m_t = jnp.maximum(inter, jnp.max(d, axis=1, keepdims=True))
        w_intra = jnp.exp(d - m_t)
        w_inter = jnp.exp(inter - m_t)
        sw = _nt_dot(qh, kh) * w_intra
        num = _dot(sw.astype(BF16), vh) + w_inter * _nt_dot(qh, c_old.astype(BF16))
        den = jnp.sum(sw, axis=1, keepdims=True) \
            + w_inter * jnp.sum(qh.astype(F32) * nrow, axis=1, keepdims=True)
        hh = num / jnp.maximum(jnp.abs(den), jnp.exp(-m_t))
        y = _mlstm_out_gate(hh, o_ref[:, sl].astype(F32), gnb_ref[:, sl], gb_ref[:, sl].astype(F32))
        y_ref[:, sl] = y.astype(BF16)

        m_new = m_t[L - 1:L, :]
        b_last = bcol[L - 1:L, :]
        w_s = jnp.exp(((b_last - bcol) + icol) - m_new)
        decay = jnp.exp((b_last + m_prev) - m_new)
        wv = w_s * vh.astype(F32)
        c_ref[h] = decay * c_old + _dot(wv.T.astype(BF16), kh)
        n_ref[h:h + 1, :] = decay * nrow + jnp.sum(w_s * kh.astype(F32), axis=0, keepdims=True)
        m_ref[h:h + 1, :] = jnp.broadcast_to(m_new, (1, 128))


N_MLSTM_IN = 7
N_MLSTM_OUT = 4
MLSTM_HEAD_GROUPS = ((0, 1, 2, 3),)


def _paged_kernel(pt_ref, q_ref, kn_ref, vn_ref, ga_ref, gna_ref, lam_ref, exp_ref, *rest,
                  n_chunk_steps, chunks_per_seq):
    pp = PAGES_PER_STEP
    k_refs = rest[:pp]
    v_refs = rest[pp:2 * pp]
    mlstm_in = rest[2 * pp:2 * pp + N_MLSTM_IN]
    out_ref = rest[2 * pp + N_MLSTM_IN]
    mlstm_out = rest[2 * pp + N_MLSTM_IN + 1:2 * pp + N_MLSTM_IN + 1 + N_MLSTM_OUT]
    kb_s, vb_s, m_s, l_s, acc_s, bcum_s, rows_s = rest[2 * pp + N_MLSTM_IN + 1 + N_MLSTM_OUT:]
    g = pl.program_id(1)
    ng = pl.num_programs(1)
    ntok = pp * PAGE

    step = pl.program_id(0) * ng + g

    ngroups = len(MLSTM_HEAD_GROUPS)
    first = (step // ngroups) % chunks_per_seq == 0
    for gi, heads in enumerate(MLSTM_HEAD_GROUPS):
        @pl.when(jnp.logical_and(step < n_chunk_steps, step % ngroups == gi))
        def _():
            _mlstm_chunk(first, heads, gi > 0, *mlstm_in, *mlstm_out, bcum_s, rows_s)

    @pl.when(g == 0)
    def _():
        m_s[...] = jnp.full_like(m_s, NEG)
        l_s[...] = jnp.zeros_like(l_s)
        acc_s[...] = jnp.zeros_like(acc_s)

    vrows = PAGE * H_A
    for i in range(pp):
        kb_s[:, i * PAGE:(i + 1) * PAGE] = k_refs[i][...].astype(BF16)
        vb_s[i * vrows:(i + 1) * vrows, :] = v_refs[i][...].astype(BF16)

    q = q_ref[...].astype(F32)
    sub = lax.broadcasted_iota(jnp.int32, (8, A_QK), 0)
    lane = lax.broadcasted_iota(jnp.int32, (8, A_QK), 1)
    qbd32 = jnp.where((lane >> 6) == sub, jnp.broadcast_to(q, (8, A_QK)), 0.0)
    qbd = qbd32.astype(BF16)

    j = lax.broadcasted_iota(jnp.int32, (8, 1), 0)
    slope = jnp.where(j < 2, ALIBI_SLOPES[0] * LOG2E,
                      jnp.where(j < 4, ALIBI_SLOPES[1] * LOG2E,
                                jnp.where(j < 6, ALIBI_SLOPES[2] * LOG2E,
                                          ALIBI_SLOPES[3] * LOG2E))).astype(F32)
    kpos = g * ntok + lax.broadcasted_iota(jnp.int32, (1, ntok), 1)
    dist = (ng * ntok - kpos).astype(F32)

    s = _dot(qbd, kb_s[...]) - slope * dist
    m_old = m_s[:, 0:1]
    l_old = l_s[:, 0:1]
    m_new = jnp.maximum(m_old, jnp.max(s, axis=1, keepdims=True))
    alpha = jnp.exp2(m_old - m_new)
    p = jnp.exp2(s - m_new)
    l_new = alpha * l_old + jnp.sum(p, axis=1, keepdims=True)
    pst = jnp.concatenate([p[:, i * PAGE:(i + 1) * PAGE] for i in range(pp)], axis=0)
    spread = _dot(pst.astype(BF16), exp_ref[...])
    own_head = (lax.broadcasted_iota(jnp.int32, (8, vrows), 1) & (H_A - 1)) \
        == (lax.broadcasted_iota(jnp.int32, (8, vrows), 0) >> 1)
    p4 = jnp.concatenate([jnp.where(own_head, spread[8 * i:8 * (i + 1), :], 0.0) for i in range(pp)],
                         axis=1)
    acc = alpha * acc_s[...] + _dot(p4.astype(BF16), vb_s[...])
    head_of_row = lax.broadcasted_iota(jnp.int32, (8, DV_A), 0) >> 1
    m_s[...] = jnp.broadcast_to(m_new, m_s.shape)
    l_s[...] = jnp.broadcast_to(l_new, l_s.shape)
    acc_s[...] = acc

    @pl.when(g == ng - 1)
    def _():
        kn = kn_ref[...].astype(F32)
        s_new = jnp.sum(qbd32 * kn, axis=1, keepdims=True)
        m_fin = jnp.maximum(m_new, s_new)
        a2 = jnp.exp2(m_new - m_fin)
        p_new = jnp.exp2(s_new - m_fin)
        l_fin = a2 * l_new + p_new
        vn = jnp.zeros((8, DV_A), F32)
        for h in range(H_A):
            vrow = vn_ref[:, h * DV_A:(h + 1) * DV_A].astype(F32)
            vn = jnp.where(head_of_row == h, jnp.broadcast_to(vrow, (8, DV_A)), vn)
        o = (a2 * acc + p_new * vn) / l_fin
        lam = lam_ref[0:1, :]
        for h in range(H_A):
            sl = slice(h * DV_A, (h + 1) * DV_A)
            out = _diff_norm_gate(o[2 * h:2 * h + 1, :], o[2 * h + 1:2 * h + 2, :], lam,
                                  gna_ref[:, sl], ga_ref[:, sl].astype(F32))
            out_ref[:, sl] = out.astype(BF16)


def _paged_attn_mlstm(page_table, q3, kn3, vn3, ga3, gna, lam_tile, ck, cv,
                      mq, mk, mv, mo, gb, zif, gnb, nb, seq):
    nreq, npages = page_table.shape
    pp = PAGES_PER_STEP
    ng = npages // pp
    nc = seq // CHUNK
    ngroups = len(MLSTM_HEAD_GROUPS)
    n_chunk_steps = nb * nc * ngroups
    assert n_chunk_steps <= nreq * ng, "every mLSTM (chunk, head group) needs a grid step"

    expand = (jnp.arange(PAGE * H_A)[None, :] // H_A == jnp.arange(PAGE)[:, None]).astype(BF16)

    def req(width):
        return pl.BlockSpec((None, 1, width), lambda r, g, pt: (r, 0, 0))

    def page(i):
        return pl.BlockSpec((None, 512, PAGE), lambda r, g, pt: (pt[r, g * pp + i], 0, 0))

    def chunk(r, g):
        return jnp.minimum(r * ng + g, n_chunk_steps - 1) // ngroups

    def crows(width):
        return pl.BlockSpec((CHUNK, width), lambda r, g, pt: (chunk(r, g), 0))

    def cstate(*shape):
        return pl.BlockSpec((None,) + shape, lambda r, g, pt: (chunk(r, g) // nc,) + (0,) * len(shape))

    return pl.pallas_call(
        functools.partial(_paged_kernel, n_chunk_steps=n_chunk_steps, chunks_per_seq=nc),
        grid_spec=pltpu.PrefetchScalarGridSpec(
            num_scalar_prefetch=1,
            grid=(nreq, ng),
            in_specs=[req(512), req(512), req(512), req(512),
                      pl.BlockSpec((1, W_A), lambda r, g, pt: (0, 0)),
                      pl.BlockSpec((8, 128), lambda r, g, pt: (0, 0)),
                      pl.BlockSpec((PAGE, PAGE * H_A), lambda r, g, pt: (0, 0))]
                     + [page(i) for i in range(pp)] + [page(i) for i in range(pp)]
                     + [crows(512), crows(512), crows(512), crows(512), crows(512), crows(GATE_PAD),
                        pl.BlockSpec((1, W_B), lambda r, g, pt: (0, 0))],
            out_specs=[req(512), crows(512), cstate(H_B, DV_B, DK_B), cstate(H_B, DK_B), cstate(H_B, 128)],
            scratch_shapes=[
                pltpu.VMEM((A_QK, pp * PAGE), BF16),
                pltpu.VMEM((pp * PAGE * H_A, DV_A), BF16),
                pltpu.VMEM((8, 128), F32),
                pltpu.VMEM((8, 128), F32),
                pltpu.VMEM((8, DV_A), F32),
                pltpu.VMEM((CHUNK, 128), F32),
                pltpu.VMEM((16, CHUNK), F32),
            ],
        ),
        out_shape=[
            jax.ShapeDtypeStruct((nreq, 1, W_A), BF16),
            jax.ShapeDtypeStruct((nb * seq, W_B), BF16),
            jax.ShapeDtypeStruct((nb, H_B, DV_B, DK_B), F32),
            jax.ShapeDtypeStruct((nb, H_B, DK_B), F32),
            jax.ShapeDtypeStruct((nb, H_B, 128), F32),
        ],
        compiler_params=pltpu.CompilerParams(
            dimension_semantics=("arbitrary", "arbitrary"), vmem_limit_bytes=VMEM_LIMIT),
        name="paged_attn_mlstm",
    )(page_table, q3, kn3, vn3, ga3, gna, lam_tile, expand, *([ck] * pp), *([cv] * pp),
      mq, mk, mv, mo, gb, zif, gnb)


def _mstep_kernel(q_ref, k_ref, v_ref, o_ref, gb_ref, zif_ref, c_ref, n_ref, m_ref, gnb_ref,
                  y_ref, co_ref, no_ref, mo_ref):
    rb = STEP_RB
    x = zif_ref[...]
    sub = lax.broadcasted_iota(jnp.int32, (rb, 128), 0)
    lane = lax.broadcasted_iota(jnp.int32, (rb, 128), 1)
    m_out = jnp.zeros((rb, 128), F32)
    for h in range(H_B):
        sl = slice(h * 128, (h + 1) * 128)
        qb = q_ref[:, sl]
        qh = qb.astype(F32)
        kh = k_ref[:, sl].astype(F32)
        vh = v_ref[:, sl].astype(F32)
        i_c = x[:, h:h + 1]
        b = _log_sigmoid(x[:, H_B + h:H_B + h + 1])
        m_prev = m_ref[:, h:h + 1]
        inter = b + m_prev
        m_t = jnp.maximum(inter, (b - b) + i_c)
        w_intra = jnp.exp(((b - b) + i_c) - m_t)
        w_inter = jnp.exp(inter - m_t)
        sw = jnp.sum(qh * kh, axis=1, keepdims=True) * w_intra
        cq = jnp.zeros((rb, 128), F32)
        for r in range(rb):
            res = _nt_dot(qb, c_ref[r, h].astype(BF16))
            cq = jnp.where(sub == r, res, cq)
        nh = n_ref[:, sl]
        num = sw * vh + w_inter * cq
        den = sw + w_inter * jnp.sum(nh * qh, axis=1, keepdims=True)
        hh = num / jnp.maximum(jnp.abs(den), jnp.exp(-m_t))
        y = _mlstm_out_gate(hh, o_ref[:, sl].astype(F32), gnb_ref[:, sl], gb_ref[:, sl].astype(F32))
        y_ref[:, sl] = y.astype(BF16)

        w_s = jnp.exp(((b - b) + i_c) - m_t)
        decay = jnp.exp((b + m_prev) - m_t)
        wv = w_s * vh
        for r in range(rb):
            vcol = jnp.broadcast_to(wv[r:r + 1, :], (DV_B, DK_B)).T
            co_ref[r, h] = decay[r:r + 1, :] * c_ref[r, h] + vcol * kh[r:r + 1, :]
        no_ref[:, sl] = decay * nh + w_s * kh
        m_out = jnp.where(lane == h, m_t, m_out)
    mo_ref[...] = m_out


def _mlstm_step(mq, mk, mv, mo, gb, zif, c0, n0, m0, gnb):
    nreq = mq.shape[0]
    rb = STEP_RB

    def rows(width):
        return pl.BlockSpec((rb, width), lambda i: (i, 0))

    cspec = pl.BlockSpec((rb, H_B, DV_B, DK_B), lambda i: (i, 0, 0, 0))
    return pl.pallas_call(
        _mstep_kernel,
        grid=(nreq // rb,),
        in_specs=[rows(512), rows(512), rows(512), rows(512), rows(512), rows(GATE_PAD),
                  cspec, rows(512), rows(H_B), pl.BlockSpec((1, W_B), lambda i: (0, 0))],
        out_specs=[rows(512), cspec, rows(512), rows(128)],
        out_shape=[
            jax.ShapeDtypeStruct((nreq, W_B), BF16),
            jax.ShapeDtypeStruct((nreq, H_B, DV_B, DK_B), F32),
            jax.ShapeDtypeStruct((nreq, H_B * DK_B), F32),
            jax.ShapeDtypeStruct((nreq, 128), F32),
        ],
        compiler_params=pltpu.CompilerParams(
            dimension_semantics=("arbitrary",), vmem_limit_bytes=VMEM_LIMIT),
        name="mlstm_step",
    )(mq, mk, mv, mo, gb, zif, c0, n0, m0, gnb)


def _out_kernel(ma_ref, mb_ref, w_ref, x_ref, gate_ref, y_ref):
    acc = _dot(ma_ref[...], w_ref[0:W_A, :]) + _dot(mb_ref[...], w_ref[W_A:W_A + W_B, :])
    y_ref[...] = x_ref[...] + gate_ref[...] * acc


def _out_proj(mix_a, mix_b, w_out, x2d, mod, per_row, tm):
    m_rows = x2d.shape[0]
    nt = m_rows // tm
    if per_row:
        gate_spec = pl.BlockSpec((tm, D_MODEL), lambda i: (i, 2))
    else:
        tiles_per_batch = nt // mod.shape[0]
        gate_spec = pl.BlockSpec((None, 1, D_MODEL), lambda i: (i // tiles_per_batch, 0, 2))
    return pl.pallas_call(
        _out_kernel,
        grid=(nt,),
        in_specs=[
            pl.BlockSpec((tm, W_A), lambda i: (i, 0)),
            pl.BlockSpec((tm, W_B), lambda i: (i, 0)),
            pl.BlockSpec((W_A + W_B, D_MODEL), lambda i: (0, 0)),
            pl.BlockSpec((tm, D_MODEL), lambda i: (i, 0)),
            gate_spec,
        ],
        out_specs=pl.BlockSpec((tm, D_MODEL), lambda i: (i, 0)),
        out_shape=jax.ShapeDtypeStruct((m_rows, D_MODEL), F32),
        compiler_params=pltpu.CompilerParams(
            dimension_semantics=("arbitrary",), vmem_limit_bytes=VMEM_LIMIT),
        name="out_rows" if per_row else "out_bcast",
    )(mix_a, mix_b, w_out, x2d, mod)


def kernel(x_prompt, x_sample, c_prompt, c_sample, cache_k, cache_v, state_C, state_n, state_m,
           page_table, norm_g, w_ada, b_ada, w_in, b_in, g_q, g_k, lam_q, lam_k, gn_a, gn_b, w_out):
    assert w_in.shape[0] == 1, "single-layer model"
    nb, seq, _ = x_prompt.shape
    nreq = x_sample.shape[0]
    assert x_sample.shape[1] == 1

    w = w_in[0]
    bvec = b_in[0]
    gate_lo = N_MAIN
    gb_lo = N_MAIN + 2 * H_B
    wm = w[:, :N_MAIN].astype(BF16)
    wif = jnp.pad(w[:, gate_lo:gb_lo], ((0, 0), (0, GATE_PAD - 2 * H_B))).astype(BF16)
    wgb = w[:, gb_lo:].astype(BF16)
    bm = bvec[None, :N_MAIN]
    bif = jnp.pad(bvec[gate_lo:gb_lo], (0, GATE_PAD - 2 * H_B))[None, :]
    bgb = bvec[None, gb_lo:]
    gq = jnp.tile(g_q[0], 2 * H_A)[None, :]
    gk = jnp.tile(g_k[0], 2 * H_A)[None, :]
    grp = jnp.arange(A_QK) // DK_A
    seg = (grp[:, None] == grp[None, :]).astype(BF16)
    weights = (norm_g[0][None, :], wm, wif, wgb, bm, bif, bgb, gq, gk, seg)
    gna = gn_a[0].reshape(1, W_A)
    gnb = gn_b[0].reshape(1, W_B)
    wo = w_out[0].astype(BF16)

    pad_rows = (-(nb + nreq)) % 8
    c_all = jnp.concatenate([c_prompt, c_sample, jnp.zeros((pad_rows, D_MODEL), F32)], axis=0)
    mod, lam_tile = _ada(c_all, w_ada[0], b_ada[0][None, :], lam_q[0], lam_k[0])
    mod_p = mod[:nb].reshape(nb, 1, 3 * D_MODEL)
    mod_s = mod[nb:nb + nreq]

    xp = x_prompt.reshape(nb * seq, D_MODEL)
    (q, kf, kb, vf, vb, ga, mq, mk, mv, mo, zif, gb) = _proj(xp, mod_p, False, PROJ_TM, weights)
    xs = x_sample.reshape(nreq, D_MODEL)
    (qs, kfs, kbs, vfs, vbs, gas, mqs, mks, mvs, mos, zifs, gbs) = _proj(xs, mod_s, True, nreq, weights)

    mix_a = _attn_prompt(q, kb, vb, ga, gna, lam_tile, nb, seq)
    ck = jnp.transpose(cache_k[0], (0, 2, 3, 4, 1)).reshape(cache_k.shape[1], A_QK, PAGE)
    cv = cache_v[0].reshape(cache_v.shape[1], PAGE * H_A, DV_A)
    r3 = lambda a: a.reshape(nreq, 1, a.shape[-1])
    mix_as, mix_b, c_p, n_p, m_p = _paged_attn_mlstm(
        page_table, r3(qs), r3(kbs), r3(vbs), r3(gas), gna, lam_tile, ck, cv,
        mq, mk, mv, mo, gb, zif, gnb, nb, seq)
    mix_bs, c_s, n_s, m_s = _mlstm_step(mqs, mks, mvs, mos, gbs, zifs, state_C[0],
                                        state_n[0].reshape(nreq, H_B * DK_B), state_m[0], gnb)

    y_p = _out_proj(mix_a, mix_b, wo, xp, mod_p, False, OUT_TM)
    y_s = _out_proj(mix_as.reshape(nreq, W_A), mix_bs, wo, xs, mod_s, True, nreq)

    return (
        y_p.reshape(nb, seq, D_MODEL),
        y_s.reshape(nreq, 1, D_MODEL),
        jnp.transpose(kf.reshape(nb, H_A, 2, DK_A, seq), (0, 4, 1, 2, 3))[None],
        vf.reshape(1, nb, seq, H_A, DV_A),
        kfs.reshape(1, nreq, 1, H_A, 2, DK_A),
        vfs.reshape(1, nreq, 1, H_A, DV_A),
        c_p[None],
        n_p[None],
        m_p[:, :, 0][None],
        c_s[None],
        n_s.reshape(1, nreq, H_B, DK_B),
        m_s[:, :H_B][None],
    )
```
